```python
import jax, jax.numpy as jnp
from jax import lax
import numpy as np

D_MODEL = 1024
BATCH = 16
SEQ = 2048
DEPTH = 1

EPS = 1e-6
N_MEM = 256
BRANCH_WIDTH = 512
N_BRANCH = 3
HG_HEADS = 4
HG_DK = 128
HG_DV = 128
HG_WIDTH = HG_HEADS * HG_DV
HG_CHUNK = 64
SC_WIDTH = BRANCH_WIDTH
SC_KERNEL = 3
MX_HEADS = 4
MX_HD = 128
MX_WIDTH = MX_HEADS * MX_HD
IN_SPLITS = tuple(BRANCH_WIDTH * i for i in range(1, 10))
IN_COLS = 9 * BRANCH_WIDTH + N_BRANCH * D_MODEL
PEER_HEADS = 8
PEER_NKEYS = 128
PEER_N = PEER_NKEYS * PEER_NKEYS
PEER_DQ = 256
PEER_TOPK = 16
PEER_TOKEN_BLOCK = 128

kernel_name = "hybrid_hgrn2_shortconv_memattn_peer_encoder"


def rms_norm(x, g):
    xf = x.astype(jnp.float32)
    y = xf * lax.rsqrt(jnp.mean(xf * xf, axis=-1, keepdims=True) + EPS)
    return (y * g.astype(jnp.float32)).astype(x.dtype)


def hgrn2_chunk_scan(q, k, v, log_f):
    B, S, H, _ = q.shape
    C = HG_CHUNK
    n = S // C
    to_chunks = lambda t: t.reshape(B, n, C, H, t.shape[-1]).transpose(0, 3, 1, 2, 4)
    q, k, v, log_f = map(to_chunks, (q, k, v, log_f))
    G = jnp.cumsum(log_f, axis=3)
    G_ref = G[:, :, :, C // 2 - 1:C // 2, :]
    q_r = q * jnp.exp(G - G_ref)
    k_r = k * jnp.exp(G_ref - G)
    scores = jnp.einsum('bhncd,bhnsd->bhncs', q_r, k_r)
    mask = jnp.tril(jnp.ones((C, C), dtype=bool))
    intra = jnp.einsum('bhncs,bhnse->bhnce', jnp.where(mask, scores, 0.0), v)
    G_last = G[:, :, :, -1:, :]
    dS = jnp.einsum('bhncd,bhnce->bhnde', k * jnp.exp(G_last - G), v)
    decay = jnp.exp(G_last[:, :, :, 0, :])

    def step(S_prev, inp):
        dS_c, dec_c = inp
        return dec_c[..., None] * S_prev + dS_c, S_prev

    S0 = jnp.zeros((B, H, q.shape[-1], v.shape[-1]), jnp.float32)
    _, S_start = lax.scan(step, S0, (jnp.moveaxis(dS, 2, 0), jnp.moveaxis(decay, 2, 0)))
    S_start = jnp.moveaxis(S_start, 0, 2)
    inter = jnp.einsum('bhncd,bhnde->bhnce', q * jnp.exp(G), S_start)
    o = intra + inter
    return o.transpose(0, 2, 3, 1, 4).reshape(B, S, H, v.shape[-1])


def hgrn2_direction(q, v, z, lb, reverse):
    B, S = z.shape[:2]
    zf = z.astype(jnp.float32)
    log_f = jnp.log(lb + (1.0 - lb) * jax.nn.sigmoid(zf))
    k = (1.0 - lb) * jax.nn.sigmoid(-zf)
    heads = lambda t: t.reshape(B, S, HG_HEADS, HG_DK)
    args = (heads(q), heads(k), v.reshape(B, S, HG_HEADS, HG_DV), heads(log_f))
    if reverse:
        args = tuple(jnp.flip(t, axis=1) for t in args)
    o = hgrn2_chunk_scan(*args)
    return jnp.flip(o, axis=1) if reverse else o


def token_mixers(h, mem, w_in, lb, hg_norm_g, sc_conv_w, mem_norm_g, w_mem_kv, w_branch, w_out):
    B, S, _ = h.shape
    proj = h @ w_in
    hq, hi, hf_fwd, hf_bwd, hg, sb, sc, sh, mq, gates = jnp.split(proj, IN_SPLITS, axis=-1)

    q = jax.nn.silu(hq.astype(jnp.float32))
    v = hi.astype(jnp.float32)
    o = hgrn2_direction(q, v, hf_fwd, lb[0], False) + hgrn2_direction(q, v, hf_bwd, lb[1], True)
    o = o * lax.rsqrt(jnp.mean(o * o, axis=-1, keepdims=True) + EPS)
    y_hg = (o.reshape(B, S, HG_WIDTH) * hg_norm_g.astype(jnp.float32)
            * jax.nn.silu(hg.astype(jnp.float32))).astype(h.dtype)

    u = sc * sh
    conv = lax.conv_general_dilated(u, sc_conv_w[:, None, :], window_strides=(1,), padding='SAME',
                                    dimension_numbers=('NWC', 'WIO', 'NWC'),
                                    feature_group_count=SC_WIDTH)
    y_sc = sb * conv

    m = rms_norm(mem, mem_norm_g)
    mk, mv = jnp.split(m @ w_mem_kv, 2, axis=-1)
    qh = mq.reshape(B, S, MX_HEADS, MX_HD)
    kh = mk.reshape(B, N_MEM, MX_HEADS, MX_HD)
    vh = mv.reshape(B, N_MEM, MX_HEADS, MX_HD)
    logits = jnp.einsum('bshd,bmhd->bhsm', qh, kh).astype(jnp.float32) * (MX_HD ** -0.5)
    p = jax.nn.softmax(logits, axis=-1).astype(h.dtype)
    y_mx = jnp.einsum('bhsm,bmhd->bshd', p, vh).reshape(B, S, MX_WIDTH)

    branches = jnp.stack([y_hg, y_sc, y_mx], axis=2)
    projected = jnp.einsum('bsnc,ncd->bsnd', branches, w_branch)
    gate = jax.nn.sigmoid(gates.reshape(B, S, N_BRANCH, D_MODEL))
    merged = jnp.sum(gate * projected, axis=2)
    return merged @ w_out


def peer(h, w_q, sub_keys, u, v):
    B, S, D = h.shape
    T = B * S
    hf = h.reshape(T, D)
    q = (hf @ w_q).reshape(T, PEER_HEADS, 2, PEER_DQ // 2).astype(jnp.float32)
    s = jnp.einsum('thpd,hpkd->thpk', q, sub_keys.astype(jnp.float32))
    top_s, top_i = lax.top_k(s, PEER_TOPK)
    cand_s = (top_s[:, :, 0, :, None] + top_s[:, :, 1, None, :]).reshape(T, PEER_HEADS, -1)
    cand_i = (top_i[:, :, 0, :, None] * PEER_NKEYS + top_i[:, :, 1, None, :]).reshape(T, PEER_HEADS, -1)
    best_s, best_pos = lax.top_k(cand_s, PEER_TOPK)
    expert_idx = jnp.take_along_axis(cand_i, best_pos, axis=-1)
    gate = jax.nn.softmax(best_s, axis=-1).astype(h.dtype)

    nb = T // PEER_TOKEN_BLOCK

    def block(args):
        xb, idxb, gb = args
        ub = jnp.take(u, idxb, axis=0)
        act = jax.nn.gelu(jnp.einsum('td,thkd->thk', xb, ub), approximate=False)
        vb = jnp.take(v, idxb, axis=0)
        return jnp.einsum('thk,thkd->td', gb * act, vb)

    out = lax.map(block, (hf.reshape(nb, PEER_TOKEN_BLOCK, D),
                          expert_idx.reshape(nb, PEER_TOKEN_BLOCK, PEER_HEADS, PEER_TOPK),
                          gate.reshape(nb, PEER_TOKEN_BLOCK, PEER_HEADS, PEER_TOPK)))
    return out.reshape(B, S, D)


def setup_inputs(seed: int = 0) -> dict:
    key = jax.random.key(seed)
    ks = jax.random.split(key, 20)
    f32 = jnp.float32
    nrm = lambda k, shape, scale: jax.random.normal(k, shape, f32) * scale
    gain = lambda k, shape: 1.0 + 0.05 * jax.random.normal(k, shape, f32)
    return {
        "x": nrm(ks[0], (BATCH, SEQ, D_MODEL), 1.0),
        "mem": nrm(ks[1], (BATCH, N_MEM, D_MODEL), 1.0),
        "norm_mix_g": gain(ks[2], (DEPTH, D_MODEL)),
        "w_in": nrm(ks[3], (DEPTH, D_MODEL, IN_COLS), D_MODEL ** -0.5),
        "hg_lb": nrm(ks[4], (DEPTH + 1, 2, HG_WIDTH), 0.1),
        "hg_norm_g": gain(ks[5], (DEPTH, HG_WIDTH)),
        "sc_conv_w": nrm(ks[6], (DEPTH, SC_KERNEL, SC_WIDTH), SC_KERNEL ** -0.5),
        "mem_norm_g": gain(ks[7], (DEPTH, D_MODEL)),
        "w_mem_kv": nrm(ks[8], (DEPTH, D_MODEL, 2 * MX_WIDTH), D_MODEL ** -0.5),
        "w_branch": nrm(ks[9], (DEPTH, N_BRANCH, BRANCH_WIDTH, D_MODEL), BRANCH_WIDTH ** -0.5),
        "w_out": nrm(ks[10], (DEPTH, D_MODEL, D_MODEL), D_MODEL ** -0.5),
        "norm_ffn_g": gain(ks[11], (DEPTH, D_MODEL)),
        "peer_w_q": nrm(ks[12], (DEPTH, D_MODEL, PEER_HEADS * PEER_DQ), D_MODEL ** -0.5),
        "peer_sub_keys": nrm(ks[13], (DEPTH, PEER_HEADS, 2, PEER_NKEYS, PEER_DQ // 2), (PEER_DQ // 2) ** -0.5),
        "peer_u": nrm(ks[14], (DEPTH, PEER_N, D_MODEL), D_MODEL ** -0.5),
        "peer_v": nrm(ks[15], (DEPTH, PEER_N, D_MODEL), D_MODEL ** -0.5),
        "final_norm_g": gain(ks[16], (D_MODEL,)),
    }


def reference(x, mem, norm_mix_g, w_in, hg_lb, hg_norm_g, sc_conv_w, mem_norm_g, w_mem_kv,
              w_branch, w_out, norm_ffn_g, peer_w_q, peer_sub_keys, peer_u, peer_v, final_norm_g):
    lb_table = jnp.cumsum(jax.nn.softmax(hg_lb.astype(jnp.float32), axis=0), axis=0)
    for l in range(DEPTH):
        h = rms_norm(x, norm_mix_g[l])
        x = x + token_mixers(h, mem, w_in[l], lb_table[l], hg_norm_g[l], sc_conv_w[l],
                             mem_norm_g[l], w_mem_kv[l], w_branch[l], w_out[l])
        h = rms_norm(x, norm_ffn_g[l])
        x = x + peer(h, peer_w_q[l], peer_sub_keys[l], peer_u[l], peer_v[l])
    return rms_norm(x, final_norm_g)
```

```python
import functools

import jax
import jax.numpy as jnp
from jax import lax
from jax.experimental import pallas as pl
from jax.experimental.pallas import tpu as pltpu

F32 = jnp.float32
BF16 = jnp.bfloat16
HIGHEST = lax.Precision.HIGHEST

EPS = 1e-6
LANES = 128
HG_HEADS = 4
HG_CHUNK = 64
MX_HEADS = 4
BRANCH_WIDTH = 512
N_MEM = 256
PEER_HEADS = 8
PEER_NKEYS = 128
PEER_TOPK = 16
PEER_SLOTS = PEER_HEADS * PEER_TOPK
PAIR_ROWS = 16
VMEM_LIMIT = 48 * 1024 * 1024

NT_DIMS = (((1,), (1,)), ((), ()))
TN_DIMS = (((0,), (0,)), ((), ()))


def _rms(x, g):
    return x * lax.rsqrt(jnp.mean(x * x, axis=-1, keepdims=True) + EPS) * g


def _norm_matmul_kernel(x_ref, g_ref, w_ref, o_ref, h_ref):
    @pl.when(pl.program_id(1) == 0)
    def _():
        h_ref[...] = _rms(x_ref[...], g_ref[...]).astype(BF16)

    o_ref[...] = jnp.dot(h_ref[...], w_ref[...], preferred_element_type=F32)


def _norm_matmul(x, g, w_bf16, tm, tn):
    T, D = x.shape
    N = w_bf16.shape[1]
    tm = min(tm, T)
    return pl.pallas_call(
        _norm_matmul_kernel,
        grid=(T // tm, N // tn),
        in_specs=[
            pl.BlockSpec((tm, D), lambda i, j: (i, 0)),
            pl.BlockSpec((1, D), lambda i, j: (0, 0)),
            pl.BlockSpec((D, tn), lambda i, j: (0, j)),
        ],
        out_specs=pl.BlockSpec((tm, tn), lambda i, j: (i, j)),
        out_shape=jax.ShapeDtypeStruct((T, N), F32),
        scratch_shapes=[pltpu.VMEM((tm, D), BF16)],
        compiler_params=pltpu.CompilerParams(
            dimension_semantics=("parallel", "arbitrary"), vmem_limit_bytes=VMEM_LIMIT),
    )(x, g.reshape(1, D), w_bf16)


def _hgrn_kernel(q_ref, v_ref, zf_ref, zb_ref, g_ref, lb_ref, ng_ref, o_ref,
                 sf_ref, sb_ref, of_ref, ob_ref):
    S = q_ref.shape[0]
    C = HG_CHUNK
    n = S // C
    row = lax.broadcasted_iota(jnp.int32, (C, C), 0)
    col = lax.broadcasted_iota(jnp.int32, (C, C), 1)
    ones_c = jnp.ones((C, LANES), F32)
    sf_ref[...] = jnp.zeros_like(sf_ref)
    sb_ref[...] = jnp.zeros_like(sb_ref)

    def chunk(c0, z_ref, lb, s_ref, out_ref, reverse):
        sl = pl.ds(c0, C)
        z = z_ref[sl, :]
        lf = jnp.log(lb + (1.0 - lb) * jax.nn.sigmoid(z))
        kk = (1.0 - lb) * jax.nn.sigmoid(-z)
        q = jax.nn.silu(q_ref[sl, :])
        v = v_ref[sl, :]
        keep = (col >= row) if reverse else (col <= row)
        G = jnp.dot(keep.astype(F32), lf, precision=HIGHEST, preferred_element_type=F32)
        mid = C // 2 if reverse else C // 2 - 1
        last = 0 if reverse else C - 1
        g_mid = G[mid:mid + 1, :]
        g_last = G[last:last + 1, :]
        q_r = q * jnp.exp(G - g_mid)
        k_r = kk * jnp.exp(g_mid - G)
        scores = lax.dot_general(q_r, k_r, NT_DIMS, preferred_element_type=F32)
        intra = jnp.dot(jnp.where(keep, scores, 0.0), v, preferred_element_type=F32)
        d_s = lax.dot_general(kk * jnp.exp(g_last - G), v, TN_DIMS, preferred_element_type=F32)
        g_tot = lax.dot_general(lf, ones_c, TN_DIMS, precision=HIGHEST, preferred_element_type=F32)
        s_prev = s_ref[...]
        inter = jnp.dot(q * jnp.exp(G), s_prev, preferred_element_type=F32)
        out_ref[sl, :] = intra + inter
        s_ref[...] = jnp.exp(g_tot) * s_prev + d_s

    lb_f = lb_ref[0:1, :]
    lb_b = lb_ref[1:2, :]

    def body(c, carry):
        chunk(pl.multiple_of(c * C, C), zf_ref, lb_f, sf_ref, of_ref, False)
        chunk(pl.multiple_of((n - 1 - c) * C, C), zb_ref, lb_b, sb_ref, ob_ref, True)
        return carry

    lax.fori_loop(0, n, body, 0)
    o = of_ref[...] + ob_ref[...]
    o = o * lax.rsqrt(jnp.mean(o * o, axis=-1, keepdims=True) + EPS)
    o_ref[...] = o * ng_ref[...] * jax.nn.silu(g_ref[...])


def _hgrn(proj, lb, ng, B, S):
    T = B * S
    H = HG_HEADS
    sec = lambda s: pl.BlockSpec((S, LANES), lambda b, h, s=s: (b, s * H + h))
    return pl.pallas_call(
        _hgrn_kernel,
        grid=(B, H),
        in_specs=[sec(0), sec(1), sec(2), sec(3), sec(4),
                  pl.BlockSpec((2, LANES), lambda b, h: (0, h)),
                  pl.BlockSpec((1, LANES), lambda b, h: (0, h))],
        out_specs=pl.BlockSpec((S, LANES), lambda b, h: (b, h)),
        out_shape=jax.ShapeDtypeStruct((T, H * LANES), F32),
        scratch_shapes=[pltpu.VMEM((LANES, LANES), F32), pltpu.VMEM((LANES, LANES), F32),
                        pltpu.VMEM((S, LANES), F32), pltpu.VMEM((S, LANES), F32)],
        compiler_params=pltpu.CompilerParams(
            dimension_semantics=("parallel", "parallel"), vmem_limit_bytes=VMEM_LIMIT),
    )(proj, proj, proj, proj, proj, lb, ng.reshape(1, -1))


def _mix_kernel(x_ref, sb_ref, sc_ref, sh_ref, scp_ref, shp_ref, scn_ref, shn_ref, mq_ref,
                gates_ref, yhg_ref, kv_ref, cw_ref, wb_ref, wo_ref, o_ref):
    i = pl.program_id(1)
    ni = pl.num_programs(1)
    ts = x_ref.shape[0]
    W = BRANCH_WIDTH
    u = sc_ref[...] * sh_ref[...]
    u_before = jnp.where(i > 0, scp_ref[7:8, :] * shp_ref[7:8, :], 0.0)
    u_after = jnp.where(i < ni - 1, scn_ref[0:1, :] * shn_ref[0:1, :], 0.0)
    rows = lax.broadcasted_iota(jnp.int32, (ts, W), 0)
    u_prev = jnp.where(rows == 0, u_before, pltpu.roll(u, 1, 0))
    u_next = jnp.where(rows == ts - 1, u_after, pltpu.roll(u, ts - 1, 0))
    cw = cw_ref[...]
    y_sc = sb_ref[...] * (cw[0:1, :] * u_prev + cw[1:2, :] * u + cw[2:3, :] * u_next)

    heads = []
    for h in range(MX_HEADS):
        qh = mq_ref[:, h * LANES:(h + 1) * LANES]
        kh = kv_ref[:, h * LANES:(h + 1) * LANES]
        vh = kv_ref[:, W + h * LANES:W + (h + 1) * LANES]
        logits = lax.dot_general(qh, kh, NT_DIMS, preferred_element_type=F32) * (LANES ** -0.5)
        e = jnp.exp(logits - jnp.max(logits, axis=-1, keepdims=True))
        p = e / jnp.sum(e, axis=-1, keepdims=True)
        heads.append(jnp.dot(p, vh, preferred_element_type=F32))
    y_mx = jnp.concatenate(heads, axis=1)

    D = x_ref.shape[1]
    merged = jnp.zeros((ts, D), F32)
    for n, y in enumerate((yhg_ref[...], y_sc, y_mx)):
        pr = jnp.dot(y.astype(BF16), wb_ref[n], preferred_element_type=F32)
        merged = merged + jax.nn.sigmoid(gates_ref[:, n * D:(n + 1) * D]) * pr
    out = jnp.dot(merged.astype(BF16), wo_ref[...], preferred_element_type=F32)
    o_ref[...] = x_ref[...] + out


def _mix(x, proj, gates, y_hg, kv, conv_w, wb_bf16, wo_bf16, B, S, ts):
    T, D = x.shape
    W = BRANCH_WIDTH
    ts = min(ts, S)
    nt = S // ts
    r8 = ts // 8
    tile = lambda b, i: b * nt + i
    sec = lambda s: pl.BlockSpec((ts, W), lambda b, i, s=s: (tile(b, i), s))
    before = lambda s: pl.BlockSpec(
        (8, W), lambda b, i, s=s: (jnp.maximum(tile(b, i) * r8 - 1, 0), s))
    after = lambda s: pl.BlockSpec(
        (8, W), lambda b, i, s=s: (jnp.minimum((tile(b, i) + 1) * r8, T // 8 - 1), s))
    return pl.pallas_call(
        _mix_kernel,
        grid=(B, nt),
        in_specs=[
            pl.BlockSpec((ts, D), lambda b, i: (tile(b, i), 0)),
            sec(5), sec(6), sec(7), before(6), before(7), after(6), after(7), sec(8),
            pl.BlockSpec((ts, 3 * D), lambda b, i: (tile(b, i), 0)),
            pl.BlockSpec((ts, W), lambda b, i: (tile(b, i), 0)),
            pl.BlockSpec((N_MEM, 2 * W), lambda b, i: (b, 0)),
            pl.BlockSpec((3, W), lambda b, i: (0, 0)),
            pl.BlockSpec((3, W, D), lambda b, i: (0, 0, 0)),
            pl.BlockSpec((D, D), lambda b, i: (0, 0)),
        ],
        out_specs=pl.BlockSpec((ts, D), lambda b, i: (tile(b, i), 0)),
        out_shape=jax.ShapeDtypeStruct((T, D), F32),
        compiler_params=pltpu.CompilerParams(
            dimension_semantics=("parallel", "parallel"), vmem_limit_bytes=VMEM_LIMIT),
    )(x, proj, proj, proj, proj, proj, proj, proj, proj, gates, y_hg, kv, conv_w, wb_bf16, wo_bf16)


def _topk_rows(s, k, payload=None):
    n = s.shape[0]
    iota = lax.broadcasted_iota(jnp.int32, s.shape, 0)
    vals, ids = [], []
    for _ in range(k):
        m = jnp.max(s, axis=0, keepdims=True)
        am = jnp.min(jnp.where(s == m, iota, n), axis=0, keepdims=True)
        hit = iota == am
        vals.append(m)
        if payload is None:
            ids.append(am)
        else:
            ids.append(jnp.sum(jnp.where(hit, payload, 0), axis=0, keepdims=True))
        s = jnp.where(hit, -jnp.inf, s)
    return jnp.concatenate(vals, axis=0), jnp.concatenate(ids, axis=0)


def _route_kernel(x_ref, g_ref, wq_ref, keys_ref, hhi_ref, hlo_ref, idx_ref, gate_ref):
    K = PEER_TOPK
    h = _rms(x_ref[...], g_ref[...])
    h_hi = h.astype(BF16)
    hhi_ref[...] = h_hi
    hlo_ref[...] = (h - h_hi.astype(F32)).astype(BF16)
    q = jnp.dot(h_hi, wq_ref[...], preferred_element_type=F32)
    idx_rows, gate_rows = [], []
    for hd in range(PEER_HEADS):
        top_s, top_i = [], []
        for p in range(2):
            c0 = (hd * 2 + p) * LANES
            s_t = lax.dot_general(keys_ref[hd, p], q[:, c0:c0 + LANES], NT_DIMS,
                                  preferred_element_type=F32)
            vs, is_ = _topk_rows(s_t, K)
            top_s.append(vs)
            top_i.append(is_)
        cand_s = jnp.concatenate([top_s[0][a:a + 1, :] + top_s[1] for a in range(K)], axis=0)
        cand_i = jnp.concatenate(
            [top_i[0][a:a + 1, :] * PEER_NKEYS + top_i[1] for a in range(K)], axis=0)
        best_s, best_i = _topk_rows(cand_s, K, payload=cand_i)
        e = jnp.exp(best_s - best_s[0:1, :])
        gate_rows.append(e / jnp.sum(e, axis=0, keepdims=True))
        idx_rows.append(best_i)
    idx_ref[...] = jnp.concatenate(idx_rows, axis=0).T
    gate_ref[...] = jnp.concatenate(gate_rows, axis=0).T


def _route(x1, g, wq_bf16, keys, tt):
    T, D = x1.shape
    tt = min(tt, T)
    NQ = wq_bf16.shape[1]
    tok = lambda w: pl.BlockSpec((tt, w), lambda i: (i, 0))
    return pl.pallas_call(
        _route_kernel,
        grid=(T // tt,),
        in_specs=[tok(D), pl.BlockSpec((1, D), lambda i: (0, 0)),
                  pl.BlockSpec((D, NQ), lambda i: (0, 0)),
                  pl.BlockSpec(keys.shape, lambda i: (0, 0, 0, 0))],
        out_specs=[tok(D), tok(D), tok(PEER_SLOTS), tok(PEER_SLOTS)],
        out_shape=[jax.ShapeDtypeStruct((T, D), BF16), jax.ShapeDtypeStruct((T, D), BF16),
                   jax.ShapeDtypeStruct((T, PEER_SLOTS), jnp.int32),
                   jax.ShapeDtypeStruct((T, PEER_SLOTS), F32)],
        compiler_params=pltpu.CompilerParams(
            dimension_semantics=("parallel",), vmem_limit_bytes=VMEM_LIMIT),
    )(x1, g.reshape(1, D), wq_bf16, keys)


def _gather_tiles(idx_s, t, tab_ref, m_ref):
    for k in range(PEER_SLOTS):
        e = idx_s[t, k]
        r = pl.multiple_of(lax.shift_right_logical(e, 1) * PAIR_ROWS, PAIR_ROWS)
        m_ref[k * PAIR_ROWS:(k + 1) * PAIR_ROWS, :] = tab_ref[pl.ds(r, PAIR_ROWS), :]


def _parity_onehot(idx_v):
    par = (idx_v & 1).astype(F32)
    return jnp.concatenate([1.0 - par, par], axis=1)


def _peer_u_kernel(idx_s, idx_v, x_ref, gate_ref, tab_ref, e_ref, g_ref, cm_ref, w_ref,
                   m_ref, sel_ref, act_ref):
    tu = idx_v.shape[0]
    sel_ref[...] = jnp.dot(_parity_onehot(idx_v[...]).astype(BF16), e_ref[...],
                           preferred_element_type=F32)

    def token(t, carry):
        _gather_tiles(idx_s, t, tab_ref, m_ref)
        r = lax.dot_general(x_ref[t], m_ref[...], NT_DIMS, preferred_element_type=F32)
        r = r * cm_ref[...] * sel_ref[pl.ds(t, 1), :]
        r8 = r[0:8, :] + r[8:16, :]
        hi = r8.astype(BF16)
        lo = (r8 - hi.astype(F32)).astype(BF16)
        a = (jnp.dot(hi, g_ref[...], preferred_element_type=F32)
             + jnp.dot(lo, g_ref[...], preferred_element_type=F32))
        act_ref[pl.ds(t, 1), :] = jnp.sum(a, axis=0, keepdims=True)
        return carry

    lax.fori_loop(0, tu, token, 0)
    a = act_ref[...]
    w_ref[...] = gate_ref[...] * (0.5 * a * (1.0 + lax.erf(a * (2.0 ** -0.5))))


def _peer_v_kernel(idx_s, idx_v, w_ref, tab_ref, e_ref, cm_ref, o_ref, m_ref, wexp_ref):
    tu = idx_v.shape[0]
    w = w_ref[...]
    oh = _parity_onehot(idx_v[...])
    w2 = jnp.concatenate([w, w], axis=1) * oh
    hi = w2.astype(BF16)
    lo = (w2 - hi.astype(F32)).astype(BF16)
    wexp_ref[...] = (jnp.dot(hi, e_ref[...], preferred_element_type=F32)
                     + jnp.dot(lo, e_ref[...], preferred_element_type=F32))

    def token(t, carry):
        _gather_tiles(idx_s, t, tab_ref, m_ref)
        w8 = wexp_ref[pl.ds(t, 1), :] * cm_ref[...]
        hi8 = w8.astype(BF16)
        lo8 = (w8 - hi8.astype(F32)).astype(BF16)
        o = jnp.dot(jnp.concatenate([hi8, lo8], axis=0), m_ref[...], preferred_element_type=F32)
        o_ref[t] = o[0:8, :] + o[8:16, :]
        return carry

    lax.fori_loop(0, tu, token, 0)


def _peer_consts():
    ncol = PEER_SLOTS * PAIR_ROWS
    col = jnp.arange(ncol)
    slot = col // PAIR_ROWS
    half = (col // 8) % 2
    row = jnp.arange(2 * PEER_SLOTS)
    expand = ((row[:, None] % PEER_SLOTS == slot[None, :])
              & (row[:, None] // PEER_SLOTS == half[None, :])).astype(BF16)
    group = (slot[:, None] == jnp.arange(PEER_SLOTS)[None, :]).astype(BF16)
    chunk8 = (jnp.arange(8)[:, None] == (col % 8)[None, :]).astype(F32)
    return expand, group, chunk8


def _peer_table(w):
    n, d = w.shape
    return w.astype(BF16).reshape(n * (d // LANES), LANES)


def _peer_specs(tu):
    smem = pl.BlockSpec((tu, PEER_SLOTS), lambda i: (i, 0), memory_space=pltpu.SMEM)
    tok = pl.BlockSpec((tu, PEER_SLOTS), lambda i: (i, 0))
    full = lambda a: pl.BlockSpec(a.shape, lambda i: (0,) * a.ndim)
    resident = lambda a: pl.BlockSpec(a.shape, lambda i: (0,) * a.ndim,
                                      pipeline_mode=pl.Buffered(1))
    return smem, tok, full, resident


def _peer_u(idx, x16, gate, tab, tu):
    T = idx.shape[0]
    tu = min(tu, T)
    expand, group, chunk8 = _peer_consts()
    cm16 = jnp.concatenate([chunk8, chunk8], axis=0)
    smem, tok, full, resident = _peer_specs(tu)
    ncol = PEER_SLOTS * PAIR_ROWS
    return pl.pallas_call(
        _peer_u_kernel,
        grid=(T // tu,),
        in_specs=[smem, tok, pl.BlockSpec((tu, 16, LANES), lambda i: (i, 0, 0)), tok,
                  resident(tab), full(expand), full(group), full(cm16)],
        out_specs=tok,
        out_shape=jax.ShapeDtypeStruct((T, PEER_SLOTS), F32),
        scratch_shapes=[pltpu.VMEM((ncol, LANES), BF16), pltpu.VMEM((tu, ncol), F32),
                        pltpu.VMEM((tu, PEER_SLOTS), F32)],
        compiler_params=pltpu.CompilerParams(
            dimension_semantics=("parallel",), vmem_limit_bytes=VMEM_LIMIT),
    )(idx, idx, x16, gate, tab, expand, group, cm16)


def _peer_v(idx, w, tab, tu):
    T = idx.shape[0]
    tu = min(tu, T)
    expand, _, chunk8 = _peer_consts()
    smem, tok, full, resident = _peer_specs(tu)
    ncol = PEER_SLOTS * PAIR_ROWS
    return pl.pallas_call(
        _peer_v_kernel,
        grid=(T // tu,),
        in_specs=[smem, tok, tok, resident(tab), full(expand), full(chunk8)],
        out_specs=pl.BlockSpec((tu, 8, LANES), lambda i: (i, 0, 0)),
        out_shape=jax.ShapeDtypeStruct((T, 8, LANES), F32),
        scratch_shapes=[pltpu.VMEM((ncol, LANES), BF16), pltpu.VMEM((tu, ncol), F32)],
        compiler_params=pltpu.CompilerParams(
            dimension_semantics=("parallel",), vmem_limit_bytes=VMEM_LIMIT),
    )(idx, idx, w, tab, expand, chunk8)


def _final_kernel(x_ref, p_ref, g_ref, o_ref):
    o_ref[...] = _rms(x_ref[...] + p_ref[...], g_ref[...])


def _final(x1, peer_out, g, tm):
    T, D = x1.shape
    tm = min(tm, T)
    tok = pl.BlockSpec((tm, D), lambda i: (i, 0))
    return pl.pallas_call(
        _final_kernel,
        grid=(T // tm,),
        in_specs=[tok, tok, pl.BlockSpec((1, D), lambda i: (0, 0))],
        out_specs=tok,
        out_shape=jax.ShapeDtypeStruct((T, D), F32),
        compiler_params=pltpu.CompilerParams(dimension_semantics=("parallel",)),
    )(x1, peer_out, g.reshape(1, D))


def kernel(x, mem, norm_mix_g, w_in, hg_lb, hg_norm_g, sc_conv_w, mem_norm_g, w_mem_kv,
           w_branch, w_out, norm_ffn_g, peer_w_q, peer_sub_keys, peer_u, peer_v, final_norm_g):
    B, S, D = x.shape
    T = B * S
    assert w_in.shape[0] == 1, "single-layer block"
    n_main = 9 * BRANCH_WIDTH
    lb_table = jnp.cumsum(jax.nn.softmax(hg_lb.astype(F32), axis=0), axis=0)
    xf = x.reshape(T, D)
    memf = mem.reshape(B * N_MEM, D)
    w_in_b = w_in[0].astype(BF16)
    proj = _norm_matmul(xf, norm_mix_g[0], w_in_b[:, :n_main], 1024, 768)
    gates = _norm_matmul(xf, norm_mix_g[0], w_in_b[:, n_main:], 1024, 768)
    kv = _norm_matmul(memf, mem_norm_g[0], w_mem_kv[0].astype(BF16), 1024, 512)
    y_hg = _hgrn(proj, lb_table[0], hg_norm_g[0], B, S)
    x1 = _mix(xf, proj, gates, y_hg, kv, sc_conv_w[0], w_branch[0].astype(BF16),
              w_out[0].astype(BF16), B, S, 256)
    h_hi, h_lo, idx, gate = _route(x1, norm_ffn_g[0], peer_w_q[0].astype(BF16),
                                   peer_sub_keys[0], 256)
    x16 = jnp.concatenate([h_hi.reshape(T, 8, LANES), h_lo.reshape(T, 8, LANES)], axis=1)
    w = _peer_u(idx, x16, gate, _peer_table(peer_u[0]), 64)
    peer_out = _peer_v(idx, w, _peer_table(peer_v[0]), 64)
    return _final(x1, peer_out.reshape(T, D), final_norm_g, 512).reshape(B, S, D)
```

```python
import functools

import jax
import jax.numpy as jnp
from jax import lax
from jax.experimental import pallas as pl
from jax.experimental.pallas import tpu as pltpu

F32 = jnp.float32
BF16 = jnp.bfloat16
HIGHEST = lax.Precision.HIGHEST

EPS = 1e-6
LANES = 128
HG_HEADS = 4
HG_CHUNK = 64
MX_HEADS = 4
BRANCH_WIDTH = 512
N_MEM = 256
PEER_HEADS = 8
PEER_NKEYS = 128
PEER_TOPK = 16
PEER_SLOTS = PEER_HEADS * PEER_TOPK
PAIR_ROWS = 16
TOKENS_PER_ITER = 2
VMEM_LIMIT = 48 * 1024 * 1024

NT_DIMS = (((1,), (1,)), ((), ()))
TN_DIMS = (((0,), (0,)), ((), ()))


def _rms(x, g):
    return x * lax.rsqrt(jnp.mean(x * x, axis=-1, keepdims=True) + EPS) * g


def _norm_matmul_kernel(x_ref, g_ref, w_ref, o_ref, h_ref):
    @pl.when(pl.program_id(1) == 0)
    def _():
        h_ref[...] = _rms(x_ref[...], g_ref[...]).astype(BF16)

    o_ref[...] = jnp.dot(h_ref[...], w_ref[...], preferred_element_type=F32)


def _norm_matmul(x, g, w_bf16, tm, tn):
    T, D = x.shape
    N = w_bf16.shape[1]
    tm = min(tm, T)
    return pl.pallas_call(
        _norm_matmul_kernel,
        grid=(T // tm, N // tn),
        in_specs=[
            pl.BlockSpec((tm, D), lambda i, j: (i, 0)),
            pl.BlockSpec((1, D), lambda i, j: (0, 0)),
            pl.BlockSpec((D, tn), lambda i, j: (0, j)),
        ],
        out_specs=pl.BlockSpec((tm, tn), lambda i, j: (i, j)),
        out_shape=jax.ShapeDtypeStruct((T, N), F32),
        scratch_shapes=[pltpu.VMEM((tm, D), BF16)],
        compiler_params=pltpu.CompilerParams(
            dimension_semantics=("parallel", "arbitrary"), vmem_limit_bytes=VMEM_LIMIT),
    )(x, g.reshape(1, D), w_bf16)


def _hgrn_kernel(q_ref, v_ref, zf_ref, zb_ref, g_ref, lb_ref, ng_ref, o_ref,
                 sf_ref, sb_ref, of_ref, ob_ref):
    S = q_ref.shape[0]
    C = HG_CHUNK
    n = S // C
    row = lax.broadcasted_iota(jnp.int32, (C, C), 0)
    col = lax.broadcasted_iota(jnp.int32, (C, C), 1)
    ones_c = jnp.ones((C, LANES), F32)
    sf_ref[...] = jnp.zeros_like(sf_ref)
    sb_ref[...] = jnp.zeros_like(sb_ref)

    def chunk(c0, z_ref, lb, s_ref, out_ref, reverse):
        sl = pl.ds(c0, C)
        z = z_ref[sl, :]
        lf = jnp.log(lb + (1.0 - lb) * jax.nn.sigmoid(z))
        kk = (1.0 - lb) * jax.nn.sigmoid(-z)
        q = jax.nn.silu(q_ref[sl, :])
        v = v_ref[sl, :]
        keep = (col >= row) if reverse else (col <= row)
        G = jnp.dot(keep.astype(F32), lf, precision=HIGHEST, preferred_element_type=F32)
        mid = C // 2 if reverse else C // 2 - 1
        last = 0 if reverse else C - 1
        g_mid = G[mid:mid + 1, :]
        g_last = G[last:last + 1, :]
        q_r = q * jnp.exp(G - g_mid)
        k_r = kk * jnp.exp(g_mid - G)
        scores = lax.dot_general(q_r, k_r, NT_DIMS, preferred_element_type=F32)
        intra = jnp.dot(jnp.where(keep, scores, 0.0), v, preferred_element_type=F32)
        d_s = lax.dot_general(kk * jnp.exp(g_last - G), v, TN_DIMS, preferred_element_type=F32)
        g_tot = lax.dot_general(lf, ones_c, TN_DIMS, precision=HIGHEST, preferred_element_type=F32)
        s_prev = s_ref[...]
        inter = jnp.dot(q * jnp.exp(G), s_prev, preferred_element_type=F32)
        out_ref[sl, :] = intra + inter
        s_ref[...] = jnp.exp(g_tot) * s_prev + d_s

    lb_f = lb_ref[0:1, :]
    lb_b = lb_ref[1:2, :]

    def body(c, carry):
        chunk(pl.multiple_of(c * C, C), zf_ref, lb_f, sf_ref, of_ref, False)
        chunk(pl.multiple_of((n - 1 - c) * C, C), zb_ref, lb_b, sb_ref, ob_ref, True)
        return carry

    lax.fori_loop(0, n, body, 0)
    o = of_ref[...] + ob_ref[...]
    o = o * lax.rsqrt(jnp.mean(o * o, axis=-1, keepdims=True) + EPS)
    o_ref[...] = o * ng_ref[...] * jax.nn.silu(g_ref[...])


def _hgrn(proj, lb, ng, B, S):
    T = B * S
    H = HG_HEADS
    sec = lambda s: pl.BlockSpec((S, LANES), lambda b, h, s=s: (b, s * H + h))
    return pl.pallas_call(
        _hgrn_kernel,
        grid=(B, H),
        in_specs=[sec(0), sec(1), sec(2), sec(3), sec(4),
                  pl.BlockSpec((2, LANES), lambda b, h: (0, h)),
                  pl.BlockSpec((1, LANES), lambda b, h: (0, h))],
        out_specs=pl.BlockSpec((S, LANES), lambda b, h: (b, h)),
        out_shape=jax.ShapeDtypeStruct((T, H * LANES), F32),
        scratch_shapes=[pltpu.VMEM((LANES, LANES), F32), pltpu.VMEM((LANES, LANES), F32),
                        pltpu.VMEM((S, LANES), F32), pltpu.VMEM((S, LANES), F32)],
        compiler_params=pltpu.CompilerParams(
            dimension_semantics=("parallel", "parallel"), vmem_limit_bytes=VMEM_LIMIT),
    )(proj, proj, proj, proj, proj, lb, ng.reshape(1, -1))


def _mix_kernel(x_ref, sb_ref, sc_ref, sh_ref, scp_ref, shp_ref, scn_ref, shn_ref, mq_ref,
                gates_ref, yhg_ref, kv_ref, cw_ref, wb_ref, wo_ref, o_ref):
    i = pl.program_id(1)
    ni = pl.num_programs(1)
    ts = x_ref.shape[0]
    W = BRANCH_WIDTH
    u = sc_ref[...] * sh_ref[...]
    u_before = jnp.where(i > 0, scp_ref[7:8, :] * shp_ref[7:8, :], 0.0)
    u_after = jnp.where(i < ni - 1, scn_ref[0:1, :] * shn_ref[0:1, :], 0.0)
    rows = lax.broadcasted_iota(jnp.int32, (ts, W), 0)
    u_prev = jnp.where(rows == 0, u_before, pltpu.roll(u, 1, 0))
    u_next = jnp.where(rows == ts - 1, u_after, pltpu.roll(u, ts - 1, 0))
    cw = cw_ref[...]
    y_sc = sb_ref[...] * (cw[0:1, :] * u_prev + cw[1:2, :] * u + cw[2:3, :] * u_next)

    heads = []
    for h in range(MX_HEADS):
        qh = mq_ref[:, h * LANES:(h + 1) * LANES]
        kh = kv_ref[:, h * LANES:(h + 1) * LANES]
        vh = kv_ref[:, W + h * LANES:W + (h + 1) * LANES]
        logits = lax.dot_general(qh, kh, NT_DIMS, preferred_element_type=F32) * (LANES ** -0.5)
        e = jnp.exp(logits - jnp.max(logits, axis=-1, keepdims=True))
        p = e / jnp.sum(e, axis=-1, keepdims=True)
        heads.append(jnp.dot(p, vh, preferred_element_type=F32))
    y_mx = jnp.concatenate(heads, axis=1)

    D = x_ref.shape[1]
    merged = jnp.zeros((ts, D), F32)
    for n, y in enumerate((yhg_ref[...], y_sc, y_mx)):
        pr = jnp.dot(y.astype(BF16), wb_ref[n], preferred_element_type=F32)
        merged = merged + jax.nn.sigmoid(gates_ref[:, n * D:(n + 1) * D]) * pr
    out = jnp.dot(merged.astype(BF16), wo_ref[...], preferred_element_type=F32)
    o_ref[...] = x_ref[...] + out


def _mix(x, proj, gates, y_hg, kv, conv_w, wb_bf16, wo_bf16, B, S, ts):
    T, D = x.shape
    W = BRANCH_WIDTH
    ts = min(ts, S)
    nt = S // ts
    r8 = ts // 8
    tile = lambda b, i: b * nt + i
    sec = lambda s: pl.BlockSpec((ts, W), lambda b, i, s=s: (tile(b, i), s))
    before = lambda s: pl.BlockSpec(
        (8, W), lambda b, i, s=s: (jnp.maximum(tile(b, i) * r8 - 1, 0), s))
    after = lambda s: pl.BlockSpec(
        (8, W), lambda b, i, s=s: (jnp.minimum((tile(b, i) + 1) * r8, T // 8 - 1), s))
    return pl.pallas_call(
        _mix_kernel,
        grid=(B, nt),
        in_specs=[
            pl.BlockSpec((ts, D), lambda b, i: (tile(b, i), 0)),
            sec(5), sec(6), sec(7), before(6), before(7), after(6), after(7), sec(8),
            pl.BlockSpec((ts, 3 * D), lambda b, i: (tile(b, i), 0)),
            pl.BlockSpec((ts, W), lambda b, i: (tile(b, i), 0)),
            pl.BlockSpec((N_MEM, 2 * W), lambda b, i: (b, 0)),
            pl.BlockSpec((3, W), lambda b, i: (0, 0)),
            pl.BlockSpec((3, W, D), lambda b, i: (0, 0, 0)),
            pl.BlockSpec((D, D), lambda b, i: (0, 0)),
        ],
        out_specs=pl.BlockSpec((ts, D), lambda b, i: (tile(b, i), 0)),
        out_shape=jax.ShapeDtypeStruct((T, D), F32),
        compiler_params=pltpu.CompilerParams(
            dimension_semantics=("parallel", "parallel"), vmem_limit_bytes=VMEM_LIMIT),
    )(x, proj, proj, proj, proj, proj, proj, proj, proj, gates, y_hg, kv, conv_w, wb_bf16, wo_bf16)


def _topk_rows(s, k, payload=None):
    n = s.shape[0]
    iota = lax.broadcasted_iota(jnp.int32, s.shape, 0)
    vals, ids = [], []
    for _ in range(k):
        m = jnp.max(s, axis=0, keepdims=True)
        am = jnp.min(jnp.where(s == m, iota, n), axis=0, keepdims=True)
        hit = iota == am
        vals.append(m)
        if payload is None:
            ids.append(am)
        else:
            ids.append(jnp.sum(jnp.where(hit, payload, 0), axis=0, keepdims=True))
        s = jnp.where(hit, -jnp.inf, s)
    return jnp.concatenate(vals, axis=0), jnp.concatenate(ids, axis=0)


def _route_kernel(x_ref, g_ref, wq_ref, keys_ref, hhi_ref, hlo_ref, idx_ref, tile_ref, gate_ref):
    K = PEER_TOPK
    h = _rms(x_ref[...], g_ref[...])
    h_hi = h.astype(BF16)
    hhi_ref[...] = h_hi
    hlo_ref[...] = (h - h_hi.astype(F32)).astype(BF16)
    q = jnp.dot(h_hi, wq_ref[...], preferred_element_type=F32)
    idx_rows, gate_rows = [], []
    for hd in range(PEER_HEADS):
        top_s, top_i = [], []
        for p in range(2):
            c0 = (hd * 2 + p) * LANES
            s_t = lax.dot_general(keys_ref[hd, p], q[:, c0:c0 + LANES], NT_DIMS,
                                  preferred_element_type=F32)
            vs, is_ = _topk_rows(s_t, K)
            top_s.append(vs)
            top_i.append(is_)
        cand_s = jnp.concatenate([top_s[0][a:a + 1, :] + top_s[1] for a in range(K)], axis=0)
        cand_i = jnp.concatenate(
            [top_i[0][a:a + 1, :] * PEER_NKEYS + top_i[1] for a in range(K)], axis=0)
        best_s, best_i = _topk_rows(cand_s, K, payload=cand_i)
        e = jnp.exp(best_s - best_s[0:1, :])
        gate_rows.append(e / jnp.sum(e, axis=0, keepdims=True))
        idx_rows.append(best_i)
    idx = jnp.concatenate(idx_rows, axis=0).T
    idx_ref[...] = idx
    tile_ref[...] = lax.shift_right_logical(idx, 1)
    gate_ref[...] = jnp.concatenate(gate_rows, axis=0).T


def _route(x1, g, wq_bf16, keys, tt):
    T, D = x1.shape
    tt = min(tt, T)
    NQ = wq_bf16.shape[1]
    tok = lambda w: pl.BlockSpec((tt, w), lambda i: (i, 0))
    return pl.pallas_call(
        _route_kernel,
        grid=(T // tt,),
        in_specs=[tok(D), pl.BlockSpec((1, D), lambda i: (0, 0)),
                  pl.BlockSpec((D, NQ), lambda i: (0, 0)),
                  pl.BlockSpec(keys.shape, lambda i: (0, 0, 0, 0))],
        out_specs=[tok(D), tok(D), tok(PEER_SLOTS), tok(PEER_SLOTS), tok(PEER_SLOTS)],
        out_shape=[jax.ShapeDtypeStruct((T, D), BF16), jax.ShapeDtypeStruct((T, D), BF16),
                   jax.ShapeDtypeStruct((T, PEER_SLOTS), jnp.int32),
                   jax.ShapeDtypeStruct((T, PEER_SLOTS), jnp.int32),
                   jax.ShapeDtypeStruct((T, PEER_SLOTS), F32)],
        compiler_params=pltpu.CompilerParams(
            dimension_semantics=("parallel",), vmem_limit_bytes=VMEM_LIMIT),
    )(x1, g.reshape(1, D), wq_bf16, keys)


def _gather_tiles(tile_s, t, tab_ref):
    return jnp.concatenate([tab_ref[tile_s[t, k]] for k in range(PEER_SLOTS)], axis=0)


def _parity_onehot(idx_v):
    par = (idx_v & 1).astype(F32)
    return jnp.concatenate([1.0 - par, par], axis=1)


def _hi_lo_rows(a):
    hi = a.astype(BF16)
    lo = (a - hi.astype(F32)).astype(BF16)
    return jnp.concatenate([hi, lo], axis=0)


def _peer_u_kernel(tile_s, idx_v, x_ref, gate_ref, tab_ref, e_ref, g_ref, cm_ref, w_ref,
                   sel_ref, act_ref):
    tu = idx_v.shape[0]
    sel_ref[...] = jnp.dot(_parity_onehot(idx_v[...]).astype(BF16), e_ref[...],
                           preferred_element_type=F32)

    def token(t):
        m = _gather_tiles(tile_s, t, tab_ref)
        r = lax.dot_general(x_ref[t], m, NT_DIMS, preferred_element_type=F32)
        r = r * cm_ref[...] * sel_ref[pl.ds(t, 1), :]
        a = jnp.dot(_hi_lo_rows(r[0:8, :] + r[8:16, :]), g_ref[...], preferred_element_type=F32)
        act_ref[pl.ds(t, 1), :] = jnp.sum(a, axis=0, keepdims=True)

    def body(i, carry):
        for j in range(TOKENS_PER_ITER):
            token(i * TOKENS_PER_ITER + j)
        return carry

    lax.fori_loop(0, tu // TOKENS_PER_ITER, body, 0)
    a = act_ref[...]
    w_ref[...] = gate_ref[...] * (0.5 * a * (1.0 + lax.erf(a * (2.0 ** -0.5))))


def _peer_v_kernel(tile_s, idx_v, w_ref, tab_ref, e_ref, cm_ref, o_ref, wexp_ref):
    tu = idx_v.shape[0]
    w = w_ref[...]
    w2 = jnp.concatenate([w, w], axis=1) * _parity_onehot(idx_v[...])
    hi = w2.astype(BF16)
    lo = (w2 - hi.astype(F32)).astype(BF16)
    wexp_ref[...] = (jnp.dot(hi, e_ref[...], preferred_element_type=F32)
                     + jnp.dot(lo, e_ref[...], preferred_element_type=F32))

    def token(t):
        m = _gather_tiles(tile_s, t, tab_ref)
        w8 = wexp_ref[pl.ds(t, 1), :] * cm_ref[...]
        o = jnp.dot(_hi_lo_rows(w8), m, preferred_element_type=F32)
        o_ref[t] = o[0:8, :] + o[8:16, :]

    def body(i, carry):
        for j in range(TOKENS_PER_ITER):
            token(i * TOKENS_PER_ITER + j)
        return carry

    lax.fori_loop(0, tu // TOKENS_PER_ITER, body, 0)


def _peer_consts():
    ncol = PEER_SLOTS * PAIR_ROWS
    col = jnp.arange(ncol)
    slot = col // PAIR_ROWS
    half = (col // 8) % 2
    row = jnp.arange(2 * PEER_SLOTS)
    expand = ((row[:, None] % PEER_SLOTS == slot[None, :])
              & (row[:, None] // PEER_SLOTS == half[None, :])).astype(BF16)
    group = (slot[:, None] == jnp.arange(PEER_SLOTS)[None, :]).astype(BF16)
    chunk8 = (jnp.arange(8)[:, None] == (col % 8)[None, :]).astype(F32)
    return expand, group, chunk8


def _peer_table(w):
    n, d = w.shape
    return w.astype(BF16).reshape(n // 2, 2 * (d // LANES), LANES)


def _peer_specs(tu):
    smem = pl.BlockSpec((tu, PEER_SLOTS), lambda i: (i, 0), memory_space=pltpu.SMEM)
    tok = pl.BlockSpec((tu, PEER_SLOTS), lambda i: (i, 0))
    full = lambda a: pl.BlockSpec(a.shape, lambda i: (0,) * a.ndim)
    resident = lambda a: pl.BlockSpec(a.shape, lambda i: (0,) * a.ndim,
                                      pipeline_mode=pl.Buffered(1))
    return smem, tok, full, resident


def _peer_u(tile, idx, x16, gate, tab, tu):
    T = idx.shape[0]
    tu = min(tu, T)
    expand, group, chunk8 = _peer_consts()
    cm16 = jnp.concatenate([chunk8, chunk8], axis=0)
    smem, tok, full, resident = _peer_specs(tu)
    ncol = PEER_SLOTS * PAIR_ROWS
    return pl.pallas_call(
        _peer_u_kernel,
        grid=(T // tu,),
        in_specs=[smem, tok, pl.BlockSpec((tu, 16, LANES), lambda i: (i, 0, 0)), tok,
                  resident(tab), full(expand), full(group), full(cm16)],
        out_specs=tok,
        out_shape=jax.ShapeDtypeStruct((T, PEER_SLOTS), F32),
        scratch_shapes=[pltpu.VMEM((tu, ncol), F32), pltpu.VMEM((tu, PEER_SLOTS), F32)],
        compiler_params=pltpu.CompilerParams(
            dimension_semantics=("parallel",), vmem_limit_bytes=VMEM_LIMIT),
    )(tile, idx, x16, gate, tab, expand, group, cm16)


def _peer_v(tile, idx, w, tab, tu):
    T = idx.shape[0]
    tu = min(tu, T)
    expand, _, chunk8 = _peer_consts()
    smem, tok, full, resident = _peer_specs(tu)
    ncol = PEER_SLOTS * PAIR_ROWS
    return pl.pallas_call(
        _peer_v_kernel,
        grid=(T // tu,),
        in_specs=[smem, tok, tok, resident(tab), full(expand), full(chunk8)],
        out_specs=pl.BlockSpec((tu, 8, LANES), lambda i: (i, 0, 0)),
        out_shape=jax.ShapeDtypeStruct((T, 8, LANES), F32),
        scratch_shapes=[pltpu.VMEM((tu, ncol), F32)],
        compiler_params=pltpu.CompilerParams(
            dimension_semantics=("parallel",), vmem_limit_bytes=VMEM_LIMIT),
    )(tile, idx, w, tab, expand, chunk8)


def _final_kernel(x_ref, p_ref, g_ref, o_ref):
    o_ref[...] = _rms(x_ref[...] + p_ref[...], g_ref[...])


def _final(x1, peer_out, g, tm):
    T, D = x1.shape
    tm = min(tm, T)
    tok = pl.BlockSpec((tm, D), lambda i: (i, 0))
    return pl.pallas_call(
        _final_kernel,
        grid=(T // tm,),
        in_specs=[tok, tok, pl.BlockSpec((1, D), lambda i: (0, 0))],
        out_specs=tok,
        out_shape=jax.ShapeDtypeStruct((T, D), F32),
        compiler_params=pltpu.CompilerParams(dimension_semantics=("parallel",)),
    )(x1, peer_out, g.reshape(1, D))


def kernel(x, mem, norm_mix_g, w_in, hg_lb, hg_norm_g, sc_conv_w, mem_norm_g, w_mem_kv,
           w_branch, w_out, norm_ffn_g, peer_w_q, peer_sub_keys, peer_u, peer_v, final_norm_g):
    B, S, D = x.shape
    T = B * S
    assert w_in.shape[0] == 1, "single-layer block"
    n_main = 9 * BRANCH_WIDTH
    lb_table = jnp.cumsum(jax.nn.softmax(hg_lb.astype(F32), axis=0), axis=0)
    xf = x.reshape(T, D)
    memf = mem.reshape(B * N_MEM, D)
    w_in_b = w_in[0].astype(BF16)
    proj = _norm_matmul(xf, norm_mix_g[0], w_in_b[:, :n_main], 1024, 768)
    gates = _norm_matmul(xf, norm_mix_g[0], w_in_b[:, n_main:], 1024, 768)
    kv = _norm_matmul(memf, mem_norm_g[0], w_mem_kv[0].astype(BF16), 1024, 512)
    y_hg = _hgrn(proj, lb_table[0], hg_norm_g[0], B, S)
    x1 = _mix(xf, proj, gates, y_hg, kv, sc_conv_w[0], w_branch[0].astype(BF16),
              w_out[0].astype(BF16), B, S, 256)
    h_hi, h_lo, idx, tile, gate = _route(x1, norm_ffn_g[0], peer_w_q[0].astype(BF16),
                                         peer_sub_keys[0], 256)
    x16 = jnp.concatenate([h_hi.reshape(T, 8, LANES), h_lo.reshape(T, 8, LANES)], axis=1)
    w = _peer_u(tile, idx, x16, gate, _peer_table(peer_u[0]), 64)
    peer_out = _peer_v(tile, idx, w, _peer_table(peer_v[0]), 64)
    return _final(x1, peer_out.reshape(T, D), final_norm_g, 512).reshape(B, S, D)
```

```python
import functools

import jax
import jax.numpy as jnp
from jax import lax
from jax.experimental import pallas as pl
from jax.experimental.pallas import tpu as pltpu

F32 = jnp.float32
BF16 = jnp.bfloat16
HIGHEST = lax.Precision.HIGHEST

EPS = 1e-6
LANES = 128
HG_HEADS = 4
HG_CHUNK = 64
MX_HEADS = 4
BRANCH_WIDTH = 512
N_MEM = 256
PEER_HEADS = 8
PEER_NKEYS = 128
PEER_TOPK = 16
PEER_SLOTS = PEER_HEADS * PEER_TOPK
PAIR_ROWS = 16
TOKENS_PER_ITER = 8
VMEM_LIMIT = 48 * 1024 * 1024

NT_DIMS = (((1,), (1,)), ((), ()))
TN_DIMS = (((0,), (0,)), ((), ()))


def _rms(x, g):
    return x * lax.rsqrt(jnp.mean(x * x, axis=-1, keepdims=True) + EPS) * g


def _norm_matmul_kernel(x_ref, g_ref, w_ref, o_ref, h_ref):
    @pl.when(pl.program_id(1) == 0)
    def _():
        h_ref[...] = _rms(x_ref[...], g_ref[...]).astype(BF16)

    o_ref[...] = jnp.dot(h_ref[...], w_ref[...], preferred_element_type=F32)


def _norm_matmul(x, g, w_bf16, tm, tn):
    T, D = x.shape
    N = w_bf16.shape[1]
    tm = min(tm, T)
    return pl.pallas_call(
        _norm_matmul_kernel,
        grid=(T // tm, N // tn),
        in_specs=[
            pl.BlockSpec((tm, D), lambda i, j: (i, 0)),
            pl.BlockSpec((1, D), lambda i, j: (0, 0)),
            pl.BlockSpec((D, tn), lambda i, j: (0, j)),
        ],
        out_specs=pl.BlockSpec((tm, tn), lambda i, j: (i, j)),
        out_shape=jax.ShapeDtypeStruct((T, N), F32),
        scratch_shapes=[pltpu.VMEM((tm, D), BF16)],
        compiler_params=pltpu.CompilerParams(
            dimension_semantics=("parallel", "arbitrary"), vmem_limit_bytes=VMEM_LIMIT),
    )(x, g.reshape(1, D), w_bf16)


def _hgrn_kernel(q_ref, v_ref, zf_ref, zb_ref, g_ref, lb_ref, ng_ref, o_ref,
                 sf_ref, sb_ref, of_ref, ob_ref):
    S = q_ref.shape[0]
    C = HG_CHUNK
    n = S // C
    row = lax.broadcasted_iota(jnp.int32, (C, C), 0)
    col = lax.broadcasted_iota(jnp.int32, (C, C), 1)
    ones_c = jnp.ones((C, LANES), F32)
    sf_ref[...] = jnp.zeros_like(sf_ref)
    sb_ref[...] = jnp.zeros_like(sb_ref)

    def chunk(c0, z_ref, lb, s_ref, out_ref, reverse):
        sl = pl.ds(c0, C)
        z = z_ref[sl, :]
        lf = jnp.log(lb + (1.0 - lb) * jax.nn.sigmoid(z))
        kk = (1.0 - lb) * jax.nn.sigmoid(-z)
        q = jax.nn.silu(q_ref[sl, :])
        v = v_ref[sl, :]
        keep = (col >= row) if reverse else (col <= row)
        G = jnp.dot(keep.astype(F32), lf, precision=HIGHEST, preferred_element_type=F32)
        mid = C // 2 if reverse else C // 2 - 1
        last = 0 if reverse else C - 1
        g_mid = G[mid:mid + 1, :]
        g_last = G[last:last + 1, :]
        q_r = q * jnp.exp(G - g_mid)
        k_r = kk * jnp.exp(g_mid - G)
        scores = lax.dot_general(q_r, k_r, NT_DIMS, preferred_element_type=F32)
        intra = jnp.dot(jnp.where(keep, scores, 0.0), v, preferred_element_type=F32)
        d_s = lax.dot_general(kk * jnp.exp(g_last - G), v, TN_DIMS, preferred_element_type=F32)
        g_tot = lax.dot_general(lf, ones_c, TN_DIMS, precision=HIGHEST, preferred_element_type=F32)
        s_prev = s_ref[...]
        inter = jnp.dot(q * jnp.exp(G), s_prev, preferred_element_type=F32)
        out_ref[sl, :] = intra + inter
        s_ref[...] = jnp.exp(g_tot) * s_prev + d_s

    lb_f = lb_ref[0:1, :]
    lb_b = lb_ref[1:2, :]

    def body(c, carry):
        chunk(pl.multiple_of(c * C, C), zf_ref, lb_f, sf_ref, of_ref, False)
        chunk(pl.multiple_of((n - 1 - c) * C, C), zb_ref, lb_b, sb_ref, ob_ref, True)
        return carry

    lax.fori_loop(0, n, body, 0)
    o = of_ref[...] + ob_ref[...]
    o = o * lax.rsqrt(jnp.mean(o * o, axis=-1, keepdims=True) + EPS)
    o_ref[...] = o * ng_ref[...] * jax.nn.silu(g_ref[...])


def _hgrn(proj, lb, ng, B, S):
    T = B * S
    H = HG_HEADS
    sec = lambda s: pl.BlockSpec((S, LANES), lambda b, h, s=s: (b, s * H + h))
    return pl.pallas_call(
        _hgrn_kernel,
        grid=(B, H),
        in_specs=[sec(0), sec(1), sec(2), sec(3), sec(4),
                  pl.BlockSpec((2, LANES), lambda b, h: (0, h)),
                  pl.BlockSpec((1, LANES), lambda b, h: (0, h))],
        out_specs=pl.BlockSpec((S, LANES), lambda b, h: (b, h)),
        out_shape=jax.ShapeDtypeStruct((T, H * LANES), F32),
        scratch_shapes=[pltpu.VMEM((LANES, LANES), F32), pltpu.VMEM((LANES, LANES), F32),
                        pltpu.VMEM((S, LANES), F32), pltpu.VMEM((S, LANES), F32)],
        compiler_params=pltpu.CompilerParams(
            dimension_semantics=("parallel", "parallel"), vmem_limit_bytes=VMEM_LIMIT),
    )(proj, proj, proj, proj, proj, lb, ng.reshape(1, -1))


def _mix_kernel(x_ref, sb_ref, sc_ref, sh_ref, scp_ref, shp_ref, scn_ref, shn_ref, mq_ref,
                gates_ref, yhg_ref, kv_ref, cw_ref, wb_ref, wo_ref, o_ref):
    i = pl.program_id(1)
    ni = pl.num_programs(1)
    ts = x_ref.shape[0]
    W = BRANCH_WIDTH
    u = sc_ref[...] * sh_ref[...]
    u_before = jnp.where(i > 0, scp_ref[7:8, :] * shp_ref[7:8, :], 0.0)
    u_after = jnp.where(i < ni - 1, scn_ref[0:1, :] * shn_ref[0:1, :], 0.0)
    rows = lax.broadcasted_iota(jnp.int32, (ts, W), 0)
    u_prev = jnp.where(rows == 0, u_before, pltpu.roll(u, 1, 0))
    u_next = jnp.where(rows == ts - 1, u_after, pltpu.roll(u, ts - 1, 0))
    cw = cw_ref[...]
    y_sc = sb_ref[...] * (cw[0:1, :] * u_prev + cw[1:2, :] * u + cw[2:3, :] * u_next)

    heads = []
    for h in range(MX_HEADS):
        qh = mq_ref[:, h * LANES:(h + 1) * LANES]
        kh = kv_ref[:, h * LANES:(h + 1) * LANES]
        vh = kv_ref[:, W + h * LANES:W + (h + 1) * LANES]
        logits = lax.dot_general(qh, kh, NT_DIMS, preferred_element_type=F32) * (LANES ** -0.5)
        e = jnp.exp(logits - jnp.max(logits, axis=-1, keepdims=True))
        p = e / jnp.sum(e, axis=-1, keepdims=True)
        heads.append(jnp.dot(p, vh, preferred_element_type=F32))
    y_mx = jnp.concatenate(heads, axis=1)

    D = x_ref.shape[1]
    merged = jnp.zeros((ts, D), F32)
    for n, y in enumerate((yhg_ref[...], y_sc, y_mx)):
        pr = jnp.dot(y.astype(BF16), wb_ref[n], preferred_element_type=F32)
        merged = merged + jax.nn.sigmoid(gates_ref[:, n * D:(n + 1) * D]) * pr
    out = jnp.dot(merged.astype(BF16), wo_ref[...], preferred_element_type=F32)
    o_ref[...] = x_ref[...] + out


def _mix(x, proj, gates, y_hg, kv, conv_w, wb_bf16, wo_bf16, B, S, ts):
    T, D = x.shape
    W = BRANCH_WIDTH
    ts = min(ts, S)
    nt = S // ts
    r8 = ts // 8
    tile = lambda b, i: b * nt + i
    sec = lambda s: pl.BlockSpec((ts, W), lambda b, i, s=s: (tile(b, i), s))
    before = lambda s: pl.BlockSpec(
        (8, W), lambda b, i, s=s: (jnp.maximum(tile(b, i) * r8 - 1, 0), s))
    after = lambda s: pl.BlockSpec(
        (8, W), lambda b, i, s=s: (jnp.minimum((tile(b, i) + 1) * r8, T // 8 - 1), s))
    return pl.pallas_call(
        _mix_kernel,
        grid=(B, nt),
        in_specs=[
            pl.BlockSpec((ts, D), lambda b, i: (tile(b, i), 0)),
            sec(5), sec(6), sec(7), before(6), before(7), after(6), after(7), sec(8),
            pl.BlockSpec((ts, 3 * D), lambda b, i: (tile(b, i), 0)),
            pl.BlockSpec((ts, W), lambda b, i: (tile(b, i), 0)),
            pl.BlockSpec((N_MEM, 2 * W), lambda b, i: (b, 0)),
            pl.BlockSpec((3, W), lambda b, i: (0, 0)),
            pl.BlockSpec((3, W, D), lambda b, i: (0, 0, 0)),
            pl.BlockSpec((D, D), lambda b, i: (0, 0)),
        ],
        out_specs=pl.BlockSpec((ts, D), lambda b, i: (tile(b, i), 0)),
        out_shape=jax.ShapeDtypeStruct((T, D), F32),
        compiler_params=pltpu.CompilerParams(
            dimension_semantics=("parallel", "parallel"), vmem_limit_bytes=VMEM_LIMIT),
    )(x, proj, proj, proj, proj, proj, proj, proj, proj, gates, y_hg, kv, conv_w, wb_bf16, wo_bf16)


def _topk_rows(s, k, payload=None):
    n = s.shape[0]
    iota = lax.broadcasted_iota(jnp.int32, s.shape, 0)
    vals, ids = [], []
    for _ in range(k):
        m = jnp.max(s, axis=0, keepdims=True)
        am = jnp.min(jnp.where(s == m, iota, n), axis=0, keepdims=True)
        hit = iota == am
        vals.append(m)
        if payload is None:
            ids.append(am)
        else:
            ids.append(jnp.sum(jnp.where(hit, payload, 0), axis=0, keepdims=True))
        s = jnp.where(hit, -jnp.inf, s)
    return jnp.concatenate(vals, axis=0), jnp.concatenate(ids, axis=0)


def _route_kernel(x_ref, g_ref, wq_ref, keys_ref, hhi_ref, hlo_ref, idx_ref, tile_ref, gate_ref):
    K = PEER_TOPK
    h = _rms(x_ref[...], g_ref[...])
    h_hi = h.astype(BF16)
    hhi_ref[...] = h_hi
    hlo_ref[...] = (h - h_hi.astype(F32)).astype(BF16)
    q = jnp.dot(h_hi, wq_ref[...], preferred_element_type=F32)
    idx_rows, gate_rows = [], []
    for hd in range(PEER_HEADS):
        top_s, top_i = [], []
        for p in range(2):
            c0 = (hd * 2 + p) * LANES
            s_t = lax.dot_general(keys_ref[hd, p], q[:, c0:c0 + LANES], NT_DIMS,
                                  preferred_element_type=F32)
            vs, is_ = _topk_rows(s_t, K)
            top_s.append(vs)
            top_i.append(is_)
        cand_s = jnp.concatenate([top_s[0][a:a + 1, :] + top_s[1] for a in range(K)], axis=0)
        cand_i = jnp.concatenate(
            [top_i[0][a:a + 1, :] * PEER_NKEYS + top_i[1] for a in range(K)], axis=0)
        best_s, best_i = _topk_rows(cand_s, K, payload=cand_i)
        e = jnp.exp(best_s - best_s[0:1, :])
        gate_rows.append(e / jnp.sum(e, axis=0, keepdims=True))
        idx_rows.append(best_i)
    idx = jnp.concatenate(idx_rows, axis=0).T
    idx_ref[...] = idx
    tile_ref[...] = lax.shift_right_logical(idx, 1)
    gate_ref[...] = jnp.concatenate(gate_rows, axis=0).T


def _route(x1, g, wq_bf16, keys, tt):
    T, D = x1.shape
    tt = min(tt, T)
    NQ = wq_bf16.shape[1]
    tok = lambda w: pl.BlockSpec((tt, w), lambda i: (i, 0))
    return pl.pallas_call(
        _route_kernel,
        grid=(T // tt,),
        in_specs=[tok(D), pl.BlockSpec((1, D), lambda i: (0, 0)),
                  pl.BlockSpec((D, NQ), lambda i: (0, 0)),
                  pl.BlockSpec(keys.shape, lambda i: (0, 0, 0, 0))],
        out_specs=[tok(D), tok(D), tok(PEER_SLOTS), tok(PEER_SLOTS), tok(PEER_SLOTS)],
        out_shape=[jax.ShapeDtypeStruct((T, D), BF16), jax.ShapeDtypeStruct((T, D), BF16),
                   jax.ShapeDtypeStruct((T, PEER_SLOTS), jnp.int32),
                   jax.ShapeDtypeStruct((T, PEER_SLOTS), jnp.int32),
                   jax.ShapeDtypeStruct((T, PEER_SLOTS), F32)],
        compiler_params=pltpu.CompilerParams(
            dimension_semantics=("parallel",), vmem_limit_bytes=VMEM_LIMIT),
    )(x1, g.reshape(1, D), wq_bf16, keys)


def _gather_tiles(tile_s, t, tab_ref):
    return jnp.concatenate([tab_ref[tile_s.at[k][t]] for k in range(PEER_SLOTS)], axis=0)


def _tile_fetch(tile_hbm, tile_s, sem):
    return pltpu.make_async_copy(tile_hbm.at[pl.program_id(0)], tile_s, sem)


def _slot_major(tile, tu):
    T, n = tile.shape
    return tile.reshape(T // tu, tu, n).transpose(0, 2, 1)


def _parity_onehot(idx_v):
    par = (idx_v & 1).astype(F32)
    return jnp.concatenate([1.0 - par, par], axis=1)


def _hi_lo_rows(a):
    hi = a.astype(BF16)
    lo = (a - hi.astype(F32)).astype(BF16)
    return jnp.concatenate([hi, lo], axis=0)


def _peer_u_kernel(tile_hbm, idx_v, x_ref, gate_ref, tab_ref, e_ref, g_ref, cm_ref, w_ref,
                   sel_ref, r_ref, tile_s, sem):
    tu = idx_v.shape[0]
    fetch = _tile_fetch(tile_hbm, tile_s, sem)
    fetch.start()
    sel_ref[...] = jnp.dot(_parity_onehot(idx_v[...]).astype(BF16), e_ref[...],
                           preferred_element_type=F32)
    fetch.wait()

    def token(t):
        m = _gather_tiles(tile_s, t, tab_ref)
        r = lax.dot_general(x_ref[t], m, NT_DIMS, preferred_element_type=F32)
        r8 = (r[0:8, :] + r[8:16, :]) * cm_ref[...] * sel_ref[pl.ds(t, 1), :]
        r_ref[pl.ds(pl.multiple_of(t * 8, 8), 8), :] = r8

    def body(i, carry):
        for j in range(TOKENS_PER_ITER):
            token(i * TOKENS_PER_ITER + j)
        return carry

    lax.fori_loop(0, tu // TOKENS_PER_ITER, body, 0)
    r_all = r_ref[...]
    hi = r_all.astype(BF16)
    lo = (r_all - hi.astype(F32)).astype(BF16)
    a = (jnp.dot(hi, g_ref[...], preferred_element_type=F32)
         + jnp.dot(lo, g_ref[...], preferred_element_type=F32))
    a = jnp.sum(a.reshape(tu, 8, PEER_SLOTS), axis=1)
    w_ref[...] = gate_ref[...] * (0.5 * a * (1.0 + lax.erf(a * (2.0 ** -0.5))))


def _peer_v_kernel(tile_hbm, idx_v, w_ref, tab_ref, e_ref, cm_ref, o_ref, wexp_ref, tile_s, sem):
    tu = idx_v.shape[0]
    fetch = _tile_fetch(tile_hbm, tile_s, sem)
    fetch.start()
    w = w_ref[...]
    w2 = jnp.concatenate([w, w], axis=1) * _parity_onehot(idx_v[...])
    hi = w2.astype(BF16)
    lo = (w2 - hi.astype(F32)).astype(BF16)
    wexp_ref[...] = (jnp.dot(hi, e_ref[...], preferred_element_type=F32)
                     + jnp.dot(lo, e_ref[...], preferred_element_type=F32))
    fetch.wait()

    def token(t):
        m = _gather_tiles(tile_s, t, tab_ref)
        w8 = wexp_ref[pl.ds(t, 1), :] * cm_ref[...]
        o = jnp.dot(_hi_lo_rows(w8), m, preferred_element_type=F32)
        o_ref[t] = o[0:8, :] + o[8:16, :]

    def body(i, carry):
        for j in range(TOKENS_PER_ITER):
            token(i * TOKENS_PER_ITER + j)
        return carry

    lax.fori_loop(0, tu // TOKENS_PER_ITER, body, 0)


def _peer_consts():
    ncol = PEER_SLOTS * PAIR_ROWS
    col = jnp.arange(ncol)
    slot = col // PAIR_ROWS
    half = (col // 8) % 2
    row = jnp.arange(2 * PEER_SLOTS)
    expand = ((row[:, None] % PEER_SLOTS == slot[None, :])
              & (row[:, None] // PEER_SLOTS == half[None, :])).astype(BF16)
    group = (slot[:, None] == jnp.arange(PEER_SLOTS)[None, :]).astype(BF16)
    chunk8 = (jnp.arange(8)[:, None] == (col % 8)[None, :]).astype(F32)
    return expand, group, chunk8


def _peer_table(w):
    n, d = w.shape
    return w.astype(BF16).reshape(n // 2, 2 * (d // LANES), LANES)


def _peer_specs(tu):
    smem = pl.BlockSpec(memory_space=pl.ANY)
    tok = pl.BlockSpec((tu, PEER_SLOTS), lambda i: (i, 0))
    full = lambda a: pl.BlockSpec(a.shape, lambda i: (0,) * a.ndim)
    resident = lambda a: pl.BlockSpec(a.shape, lambda i: (0,) * a.ndim,
                                      pipeline_mode=pl.Buffered(1))
    return smem, tok, full, resident


def _peer_u(tile, idx, x16, gate, tab, tu):
    T = idx.shape[0]
    tu = min(tu, T)
    expand, group, chunk8 = _peer_consts()
    smem, tok, full, resident = _peer_specs(tu)
    ncol = PEER_SLOTS * PAIR_ROWS
    return pl.pallas_call(
        _peer_u_kernel,
        grid=(T // tu,),
        in_specs=[smem, tok, pl.BlockSpec((tu, 16, LANES), lambda i: (i, 0, 0)), tok,
                  resident(tab), full(expand), full(group), full(chunk8)],
        out_specs=tok,
        out_shape=jax.ShapeDtypeStruct((T, PEER_SLOTS), F32),
        scratch_shapes=[pltpu.VMEM((tu, ncol), F32), pltpu.VMEM((tu * 8, ncol), F32),
                        pltpu.SMEM((PEER_SLOTS, tu), jnp.int32), pltpu.SemaphoreType.DMA(())],
        compiler_params=pltpu.CompilerParams(
            dimension_semantics=("parallel",), vmem_limit_bytes=VMEM_LIMIT),
    )(_slot_major(tile, tu), idx, x16, gate, tab, expand, group, chunk8)


def _peer_v(tile, idx, w, tab, tu):
    T = idx.shape[0]
    tu = min(tu, T)
    expand, _, chunk8 = _peer_consts()
    smem, tok, full, resident = _peer_specs(tu)
    ncol = PEER_SLOTS * PAIR_ROWS
    return pl.pallas_call(
        _peer_v_kernel,
        grid=(T // tu,),
        in_specs=[smem, tok, tok, resident(tab), full(expand), full(chunk8)],
        out_specs=pl.BlockSpec((tu, 8, LANES), lambda i: (i, 0, 0)),
        out_shape=jax.ShapeDtypeStruct((T, 8, LANES), F32),
        scratch_shapes=[pltpu.VMEM((tu, ncol), F32),
                        pltpu.SMEM((PEER_SLOTS, tu), jnp.int32), pltpu.SemaphoreType.DMA(())],
        compiler_params=pltpu.CompilerParams(
            dimension_semantics=("parallel",), vmem_limit_bytes=VMEM_LIMIT),
    )(_slot_major(tile, tu), idx, w, tab, expand, chunk8)


def _final_kernel(x_ref, p_ref, g_ref, o_ref):
    o_ref[...] = _rms(x_ref[...] + p_ref[...], g_ref[...])


def _final(x1, peer_out, g, tm):
    T, D = x1.shape
    tm = min(tm, T)
    tok = pl.BlockSpec((tm, D), lambda i: (i, 0))
    return pl.pallas_call(
        _final_kernel,
        grid=(T // tm,),
        in_specs=[tok, tok, pl.BlockSpec((1, D), lambda i: (0, 0))],
        out_specs=tok,
        out_shape=jax.ShapeDtypeStruct((T, D), F32),
        compiler_params=pltpu.CompilerParams(dimension_semantics=("parallel",)),
    )(x1, peer_out, g.reshape(1, D))


def kernel(x, mem, norm_mix_g, w_in, hg_lb, hg_norm_g, sc_conv_w, mem_norm_g, w_mem_kv,
           w_branch, w_out, norm_ffn_g, peer_w_q, peer_sub_keys, peer_u, peer_v, final_norm_g):
    B, S, D = x.shape
    T = B * S
    assert w_in.shape[0] == 1, "single-layer block"
    n_main = 9 * BRANCH_WIDTH
    lb_table = jnp.cumsum(jax.nn.softmax(hg_lb.astype(F32), axis=0), axis=0)
    xf = x.reshape(T, D)
    memf = mem.reshape(B * N_MEM, D)
    w_in_b = w_in[0].astype(BF16)
    proj = _norm_matmul(xf, norm_mix_g[0], w_in_b[:, :n_main], 1024, 768)
    gates = _norm_matmul(xf, norm_mix_g[0], w_in_b[:, n_main:], 1024, 768)
    kv = _norm_matmul(memf, mem_norm_g[0], w_mem_kv[0].astype(BF16), 1024, 512)
    y_hg = _hgrn(proj, lb_table[0], hg_norm_g[0], B, S)
    x1 = _mix(xf, proj, gates, y_hg, kv, sc_conv_w[0], w_branch[0].astype(BF16),
              w_out[0].astype(BF16), B, S, 256)
    h_hi, h_lo, idx, tile, gate = _route(x1, norm_ffn_g[0], peer_w_q[0].astype(BF16),
                                         peer_sub_keys[0], 256)
    x16 = jnp.concatenate([h_hi.reshape(T, 8, LANES), h_lo.reshape(T, 8, LANES)], axis=1)
    w = _peer_u(tile, idx, x16, gate, _peer_table(peer_u[0]), 64)
    peer_out = _peer_v(tile, idx, w, _peer_table(peer_v[0]), 64)
    return _final(x1, peer_out.reshape(T, D), final_norm_g, 512).reshape(B, S, D)
```

```python
import functools

import jax
import jax.numpy as jnp
from jax import lax
from jax.experimental import pallas as pl
from jax.experimental.pallas import tpu as pltpu

F32 = jnp.float32
BF16 = jnp.bfloat16
HIGHEST = lax.Precision.HIGHEST

EPS = 1e-6
LANES = 128
HG_HEADS = 4
HG_CHUNK = 64
HG_CHUNKS_PER_ITER = 2
HG_STATE_UNROLL = 4
MX_HEADS = 4
BRANCH_WIDTH = 512
N_MEM = 256
PEER_HEADS = 8
PEER_NKEYS = 128
PEER_TOPK = 16
PEER_SLOTS = PEER_HEADS * PEER_TOPK
PAIR_ROWS = 16
TOKENS_PER_ITER = 8
VMEM_LIMIT = 48 * 1024 * 1024

NT_DIMS = (((1,), (1,)), ((), ()))
TN_DIMS = (((0,), (0,)), ((), ()))


def _rms(x, g):
    return x * lax.rsqrt(jnp.mean(x * x, axis=-1, keepdims=True) + EPS) * g


def _norm_matmul_kernel(x_ref, g_ref, w_ref, o_ref, h_ref):
    @pl.when(pl.program_id(1) == 0)
    def _():
        h_ref[...] = _rms(x_ref[...], g_ref[...]).astype(BF16)

    o_ref[...] = jnp.dot(h_ref[...], w_ref[...], preferred_element_type=F32)


def _norm_matmul(x, g, w_bf16, tm, tn):
    T, D = x.shape
    N = w_bf16.shape[1]
    tm = min(tm, T)
    return pl.pallas_call(
        _norm_matmul_kernel,
        grid=(T // tm, N // tn),
        in_specs=[
            pl.BlockSpec((tm, D), lambda i, j: (i, 0)),
            pl.BlockSpec((1, D), lambda i, j: (0, 0)),
            pl.BlockSpec((D, tn), lambda i, j: (0, j)),
        ],
        out_specs=pl.BlockSpec((tm, tn), lambda i, j: (i, j)),
        out_shape=jax.ShapeDtypeStruct((T, N), F32),
        scratch_shapes=[pltpu.VMEM((tm, D), BF16)],
        compiler_params=pltpu.CompilerParams(
            dimension_semantics=("parallel", "arbitrary"), vmem_limit_bytes=VMEM_LIMIT),
    )(x, g.reshape(1, D), w_bf16)


def _hgrn_kernel(q_ref, v_ref, zf_ref, zb_ref, g_ref, lb_ref, ng_ref, o_ref,
                 st_ref, out_ref, qs_ref, kk_ref, cum_ref, sc_ref, qe_ref, dst_ref, dec_ref):
    S = q_ref.shape[0]
    C = HG_CHUNK
    n = S // C
    row = lax.broadcasted_iota(jnp.int32, (C, C), 0)
    col = lax.broadcasted_iota(jnp.int32, (C, C), 1)
    keeps = (col <= row, col >= row)
    z_refs = (zf_ref, zb_ref)
    unroll = HG_CHUNKS_PER_ITER

    def chunks(fn):
        def body(i, carry):
            for j in range(unroll):
                sl = pl.ds(pl.multiple_of((i * unroll + j) * C, C), C)
                for d in range(2):
                    fn(i * unroll + j, sl, d)
            return carry
        lax.fori_loop(0, n // unroll, body, 0)

    qs_ref[...] = jax.nn.silu(q_ref[...])

    def decay_pass(c, sl, d):
        lb = lb_ref[d:d + 1, :]
        z = z_refs[d][sl, :]
        lf = jnp.log(lb + (1.0 - lb) * jax.nn.sigmoid(z))
        kk_ref[d, sl, :] = (1.0 - lb) * jax.nn.sigmoid(-z)
        tri = keeps[d].astype(BF16)
        hi = lf.astype(BF16)
        r1 = lf - hi.astype(F32)
        mid = r1.astype(BF16)
        lo = (r1 - mid.astype(F32)).astype(BF16)
        cum_ref[d, sl, :] = (jnp.dot(tri, hi, preferred_element_type=F32)
                             + jnp.dot(tri, mid, preferred_element_type=F32)
                             + jnp.dot(tri, lo, preferred_element_type=F32))

    def score_pass(c, sl, d):
        G = cum_ref[d, sl, :]
        kk = kk_ref[d, sl, :]
        q = qs_ref[sl, :]
        mid = C // 2 if d else C // 2 - 1
        last = 0 if d else C - 1
        g_mid = G[mid:mid + 1, :]
        g_last = G[last:last + 1, :]
        q_r = q * jnp.exp(G - g_mid)
        k_r = kk * jnp.exp(g_mid - G)
        scores = lax.dot_general(q_r, k_r, NT_DIMS, preferred_element_type=F32)
        sc_ref[d, sl, :] = jnp.where(keeps[d], scores, 0.0)
        qe_ref[d, sl, :] = q * jnp.exp(G)
        dst_ref[d, c] = lax.dot_general(v_ref[sl, :], kk * jnp.exp(g_last - G), TN_DIMS,
                                        preferred_element_type=F32)
        dec_ref[d, c] = jnp.broadcast_to(jnp.exp(g_last), (8, LANES))

    def intra_pass(c, sl, d):
        out_ref[d, sl, :] = jnp.dot(sc_ref[d, sl, :], v_ref[sl, :], preferred_element_type=F32)

    chunks(decay_pass)
    chunks(score_pass)
    chunks(intra_pass)
    st_ref[...] = jnp.zeros_like(st_ref)

    def state_body(i, carry):
        for j in range(HG_STATE_UNROLL):
            step = i * HG_STATE_UNROLL + j
            for d, c in ((0, step), (1, n - 1 - step)):
                sl = pl.ds(pl.multiple_of(c * C, C), C)
                st = st_ref[d]
                out_ref[d, sl, :] += lax.dot_general(qe_ref[d, sl, :], st, NT_DIMS,
                                                     preferred_element_type=F32)
                st_ref[d] = st * dec_ref[d, c][0:1, :] + dst_ref[d, c]
        return carry

    lax.fori_loop(0, n // HG_STATE_UNROLL, state_body, 0)
    o = out_ref[0] + out_ref[1]
    o = o * lax.rsqrt(jnp.mean(o * o, axis=-1, keepdims=True) + EPS)
    o_ref[...] = o * ng_ref[...] * jax.nn.silu(g_ref[...])


def _hgrn(proj, lb, ng, B, S):
    T = B * S
    H = HG_HEADS
    sec = lambda s: pl.BlockSpec((S, LANES), lambda b, h, s=s: (b, s * H + h))
    return pl.pallas_call(
        _hgrn_kernel,
        grid=(B, H),
        in_specs=[sec(0), sec(1), sec(2), sec(3), sec(4),
                  pl.BlockSpec((2, LANES), lambda b, h: (0, h)),
                  pl.BlockSpec((1, LANES), lambda b, h: (0, h))],
        out_specs=pl.BlockSpec((S, LANES), lambda b, h: (b, h)),
        out_shape=jax.ShapeDtypeStruct((T, H * LANES), F32),
        scratch_shapes=[pltpu.VMEM((2, LANES, LANES), F32),
                        pltpu.VMEM((2, S, LANES), F32),
                        pltpu.VMEM((S, LANES), F32),
                        pltpu.VMEM((2, S, LANES), F32),
                        pltpu.VMEM((2, S, LANES), F32),
                        pltpu.VMEM((2, S, HG_CHUNK), F32),
                        pltpu.VMEM((2, S, LANES), F32),
                        pltpu.VMEM((2, S // HG_CHUNK, LANES, LANES), F32),
                        pltpu.VMEM((2, S // HG_CHUNK, 8, LANES), F32)],
        compiler_params=pltpu.CompilerParams(
            dimension_semantics=("parallel", "parallel"), vmem_limit_bytes=VMEM_LIMIT),
    )(proj, proj, proj, proj, proj, lb, ng.reshape(1, -1))


def _mix_kernel(x_ref, sb_ref, sc_ref, sh_ref, scp_ref, shp_ref, scn_ref, shn_ref, mq_ref,
                gates_ref, yhg_ref, kv_ref, cw_ref, wb_ref, wo_ref, o_ref):
    i = pl.program_id(1)
    ni = pl.num_programs(1)
    ts = x_ref.shape[0]
    W = BRANCH_WIDTH
    u = sc_ref[...] * sh_ref[...]
    u_before = jnp.where(i > 0, scp_ref[7:8, :] * shp_ref[7:8, :], 0.0)
    u_after = jnp.where(i < ni - 1, scn_ref[0:1, :] * shn_ref[0:1, :], 0.0)
    rows = lax.broadcasted_iota(jnp.int32, (ts, W), 0)
    u_prev = jnp.where(rows == 0, u_before, pltpu.roll(u, 1, 0))
    u_next = jnp.where(rows == ts - 1, u_after, pltpu.roll(u, ts - 1, 0))
    cw = cw_ref[...]
    y_sc = sb_ref[...] * (cw[0:1, :] * u_prev + cw[1:2, :] * u + cw[2:3, :] * u_next)

    heads = []
    for h in range(MX_HEADS):
        qh = mq_ref[:, h * LANES:(h + 1) * LANES]
        kh = kv_ref[:, h * LANES:(h + 1) * LANES]
        vh = kv_ref[:, W + h * LANES:W + (h + 1) * LANES]
        logits = lax.dot_general(qh, kh, NT_DIMS, preferred_element_type=F32) * (LANES ** -0.5)
        e = jnp.exp(logits - jnp.max(logits, axis=-1, keepdims=True))
        p = e / jnp.sum(e, axis=-1, keepdims=True)
        heads.append(jnp.dot(p, vh, preferred_element_type=F32))
    y_mx = jnp.concatenate(heads, axis=1)

    D = x_ref.shape[1]
    merged = jnp.zeros((ts, D), F32)
    for n, y in enumerate((yhg_ref[...], y_sc, y_mx)):
        pr = jnp.dot(y.astype(BF16), wb_ref[n], preferred_element_type=F32)
        merged = merged + jax.nn.sigmoid(gates_ref[:, n * D:(n + 1) * D]) * pr
    out = jnp.dot(merged.astype(BF16), wo_ref[...], preferred_element_type=F32)
    o_ref[...] = x_ref[...] + out


def _mix(x, proj, gates, y_hg, kv, conv_w, wb_bf16, wo_bf16, B, S, ts):
    T, D = x.shape
    W = BRANCH_WIDTH
    ts = min(ts, S)
    nt = S // ts
    r8 = ts // 8
    tile = lambda b, i: b * nt + i
    sec = lambda s: pl.BlockSpec((ts, W), lambda b, i, s=s: (tile(b, i), s))
    before = lambda s: pl.BlockSpec(
        (8, W), lambda b, i, s=s: (jnp.maximum(tile(b, i) * r8 - 1, 0), s))
    after = lambda s: pl.BlockSpec(
        (8, W), lambda b, i, s=s: (jnp.minimum((tile(b, i) + 1) * r8, T // 8 - 1), s))
    return pl.pallas_call(
        _mix_kernel,
        grid=(B, nt),
        in_specs=[
            pl.BlockSpec((ts, D), lambda b, i: (tile(b, i), 0)),
            sec(5), sec(6), sec(7), before(6), before(7), after(6), after(7), sec(8),
            pl.BlockSpec((ts, 3 * D), lambda b, i: (tile(b, i), 0)),
            pl.BlockSpec((ts, W), lambda b, i: (tile(b, i), 0)),
            pl.BlockSpec((N_MEM, 2 * W), lambda b, i: (b, 0)),
            pl.BlockSpec((3, W), lambda b, i: (0, 0)),
            pl.BlockSpec((3, W, D), lambda b, i: (0, 0, 0)),
            pl.BlockSpec((D, D), lambda b, i: (0, 0)),
        ],
        out_specs=pl.BlockSpec((ts, D), lambda b, i: (tile(b, i), 0)),
        out_shape=jax.ShapeDtypeStruct((T, D), F32),
        compiler_params=pltpu.CompilerParams(
            dimension_semantics=("parallel", "parallel"), vmem_limit_bytes=VMEM_LIMIT),
    )(x, proj, proj, proj, proj, proj, proj, proj, proj, gates, y_hg, kv, conv_w, wb_bf16, wo_bf16)


def _topk_rows(s, k, payload=None):
    n = s.shape[0]
    iota = lax.broadcasted_iota(jnp.int32, s.shape, 0)
    vals, ids = [], []
    for _ in range(k):
        m = jnp.max(s, axis=0, keepdims=True)
        am = jnp.min(jnp.where(s == m, iota, n), axis=0, keepdims=True)
        hit = iota == am
        vals.append(m)
        if payload is None:
            ids.append(am)
        else:
            ids.append(jnp.sum(jnp.where(hit, payload, 0), axis=0, keepdims=True))
        s = jnp.where(hit, -jnp.inf, s)
    return jnp.concatenate(vals, axis=0), jnp.concatenate(ids, axis=0)


def _pair_candidates(top_s, top_i):
    K = PEER_TOPK
    assert K == 16
    (s0, s1), (i0, i1) = top_s, top_i
    blocks_s = [s0[0:1, :] + s1]
    blocks_i = [i0[0:1, :] * PEER_NKEYS + i1]
    b_row = lax.broadcasted_iota(jnp.int32, (8, s0.shape[1]), 0)
    for a in range(1, 8):
        blocks_s.append(jnp.where(b_row < K // (a + 1), s0[a:a + 1, :] + s1[0:8, :], -jnp.inf))
        blocks_i.append(i0[a:a + 1, :] * PEER_NKEYS + i1[0:8, :])
    blocks_s.append(s0[8:16, :] + s1[0:1, :])
    blocks_i.append(i0[8:16, :] * PEER_NKEYS + i1[0:1, :])
    return jnp.concatenate(blocks_s, axis=0), jnp.concatenate(blocks_i, axis=0)


def _route_kernel(x_ref, g_ref, wq_ref, keys_ref, hhi_ref, hlo_ref, idx_ref, tile_ref, gate_ref):
    K = PEER_TOPK
    h = _rms(x_ref[...], g_ref[...])
    h_hi = h.astype(BF16)
    hhi_ref[...] = h_hi
    hlo_ref[...] = (h - h_hi.astype(F32)).astype(BF16)
    q = jnp.dot(h_hi, wq_ref[...], preferred_element_type=F32)
    idx_rows, gate_rows = [], []
    for hd in range(PEER_HEADS):
        top_s, top_i = [], []
        for p in range(2):
            c0 = (hd * 2 + p) * LANES
            s_t = lax.dot_general(keys_ref[hd, p], q[:, c0:c0 + LANES], NT_DIMS,
                                  preferred_element_type=F32)
            vs, is_ = _topk_rows(s_t, K)
            top_s.append(vs)
            top_i.append(is_)
        cand_s, cand_i = _pair_candidates(top_s, top_i)
        best_s, best_i = _topk_rows(cand_s, K, payload=cand_i)
        e = jnp.exp(best_s - best_s[0:1, :])
        gate_rows.append(e / jnp.sum(e, axis=0, keepdims=True))
        idx_rows.append(best_i)
    idx = jnp.concatenate(idx_rows, axis=0).T
    idx_ref[...] = idx
    tile_ref[...] = lax.shift_right_logical(idx, 1)
    gate_ref[...] = jnp.concatenate(gate_rows, axis=0).T


def _route(x1, g, wq_bf16, keys, tt):
    T, D = x1.shape
    tt = min(tt, T)
    NQ = wq_bf16.shape[1]
    tok = lambda w: pl.BlockSpec((tt, w), lambda i: (i, 0))
    return pl.pallas_call(
        _route_kernel,
        grid=(T // tt,),
        in_specs=[tok(D), pl.BlockSpec((1, D), lambda i: (0, 0)),
                  pl.BlockSpec((D, NQ), lambda i: (0, 0)),
                  pl.BlockSpec(keys.shape, lambda i: (0, 0, 0, 0))],
        out_specs=[tok(D), tok(D), tok(PEER_SLOTS), tok(PEER_SLOTS), tok(PEER_SLOTS)],
        out_shape=[jax.ShapeDtypeStruct((T, D), BF16), jax.ShapeDtypeStruct((T, D), BF16),
                   jax.ShapeDtypeStruct((T, PEER_SLOTS), jnp.int32),
                   jax.ShapeDtypeStruct((T, PEER_SLOTS), jnp.int32),
                   jax.ShapeDtypeStruct((T, PEER_SLOTS), F32)],
        compiler_params=pltpu.CompilerParams(
            dimension_semantics=("parallel",), vmem_limit_bytes=VMEM_LIMIT),
    )(x1, g.reshape(1, D), wq_bf16, keys)


def _gather_tiles(tile_s, t, tab_ref):
    return jnp.concatenate([tab_ref[tile_s.at[k][t]] for k in range(PEER_SLOTS)], axis=0)


def _tile_fetch(tile_hbm, tile_s, sem):
    return pltpu.make_async_copy(tile_hbm.at[pl.program_id(0)], tile_s, sem)


def _slot_major(tile, tu):
    T, n = tile.shape
    return tile.reshape(T // tu, tu, n).transpose(0, 2, 1)


def _parity_onehot(idx_v):
    par = (idx_v & 1).astype(F32)
    return jnp.concatenate([1.0 - par, par], axis=1)


def _hi_lo_rows(a):
    hi = a.astype(BF16)
    lo = (a - hi.astype(F32)).astype(BF16)
    return jnp.concatenate([hi, lo], axis=0)


def _peer_u_kernel(tile_hbm, idx_v, x_ref, gate_ref, tab_ref, e_ref, g_ref, cm_ref, w_ref,
                   sel_ref, r_ref, tile_s, sem):
    tu = idx_v.shape[0]
    fetch = _tile_fetch(tile_hbm, tile_s, sem)
    fetch.start()
    sel_ref[...] = jnp.dot(_parity_onehot(idx_v[...]).astype(BF16), e_ref[...],
                           preferred_element_type=F32)
    fetch.wait()

    def token(t):
        m = _gather_tiles(tile_s, t, tab_ref)
        r = lax.dot_general(x_ref[t], m, NT_DIMS, preferred_element_type=F32)
        r8 = (r[0:8, :] + r[8:16, :]) * cm_ref[...] * sel_ref[pl.ds(t, 1), :]
        r_ref[pl.ds(pl.multiple_of(t * 8, 8), 8), :] = r8

    def body(i, carry):
        for j in range(TOKENS_PER_ITER):
            token(i * TOKENS_PER_ITER + j)
        return carry

    lax.fori_loop(0, tu // TOKENS_PER_ITER, body, 0)
    r_all = r_ref[...]
    hi = r_all.astype(BF16)
    lo = (r_all - hi.astype(F32)).astype(BF16)
    a = (jnp.dot(hi, g_ref[...], preferred_element_type=F32)
         + jnp.dot(lo, g_ref[...], preferred_element_type=F32))
    a = jnp.sum(a.reshape(tu, 8, PEER_SLOTS), axis=1)
    w_ref[...] = gate_ref[...] * (0.5 * a * (1.0 + lax.erf(a * (2.0 ** -0.5))))


def _peer_v_kernel(tile_hbm, idx_v, w_ref, tab_ref, e_ref, cm_ref, o_ref, wexp_ref, tile_s, sem):
    tu = idx_v.shape[0]
    fetch = _tile_fetch(tile_hbm, tile_s, sem)
    fetch.start()
    w = w_ref[...]
    w2 = jnp.concatenate([w, w], axis=1) * _parity_onehot(idx_v[...])
    hi = w2.astype(BF16)
    lo = (w2 - hi.astype(F32)).astype(BF16)
    wexp_ref[...] = (jnp.dot(hi, e_ref[...], preferred_element_type=F32)
                     + jnp.dot(lo, e_ref[...], preferred_element_type=F32))
    fetch.wait()

    def token(t):
        m = _gather_tiles(tile_s, t, tab_ref)
        w8 = wexp_ref[pl.ds(t, 1), :] * cm_ref[...]
        o = jnp.dot(_hi_lo_rows(w8), m, preferred_element_type=F32)
        o_ref[t] = o[0:8, :] + o[8:16, :]

    def body(i, carry):
        for j in range(TOKENS_PER_ITER):
            token(i * TOKENS_PER_ITER + j)
        return carry

    lax.fori_loop(0, tu // TOKENS_PER_ITER, body, 0)


def _peer_consts():
    ncol = PEER_SLOTS * PAIR_ROWS
    col = jnp.arange(ncol)
    slot = col // PAIR_ROWS
    half = (col // 8) % 2
    row = jnp.arange(2 * PEER_SLOTS)
    expand = ((row[:, None] % PEER_SLOTS == slot[None, :])
              & (row[:, None] // PEER_SLOTS == half[None, :])).astype(BF16)
    group = (slot[:, None] == jnp.arange(PEER_SLOTS)[None, :]).astype(BF16)
    chunk8 = (jnp.arange(8)[:, None] == (col % 8)[None, :]).astype(F32)
    return expand, group, chunk8


def _peer_table(w):
    n, d = w.shape
    return w.astype(BF16).reshape(n // 2, 2 * (d // LANES), LANES)


def _peer_specs(tu):
    smem = pl.BlockSpec(memory_space=pl.ANY)
    tok = pl.BlockSpec((tu, PEER_SLOTS), lambda i: (i, 0))
    full = lambda a: pl.BlockSpec(a.shape, lambda i: (0,) * a.ndim)
    resident = lambda a: pl.BlockSpec(a.shape, lambda i: (0,) * a.ndim,
                                      pipeline_mode=pl.Buffered(1))
    return smem, tok, full, resident


def _peer_u(tile, idx, x16, gate, tab, tu):
    T = idx.shape[0]
    tu = min(tu, T)
    expand, group, chunk8 = _peer_consts()
    smem, tok, full, resident = _peer_specs(tu)
    ncol = PEER_SLOTS * PAIR_ROWS
    return pl.pallas_call(
        _peer_u_kernel,
        grid=(T // tu,),
        in_specs=[smem, tok, pl.BlockSpec((tu, 16, LANES), lambda i: (i, 0, 0)), tok,
                  resident(tab), full(expand), full(group), full(chunk8)],
        out_specs=tok,
        out_shape=jax.ShapeDtypeStruct((T, PEER_SLOTS), F32),
        scratch_shapes=[pltpu.VMEM((tu, ncol), F32), pltpu.VMEM((tu * 8, ncol), F32),
                        pltpu.SMEM((PEER_SLOTS, tu), jnp.int32), pltpu.SemaphoreType.DMA(())],
        compiler_params=pltpu.CompilerParams(
            dimension_semantics=("parallel",), vmem_limit_bytes=VMEM_LIMIT),
    )(_slot_major(tile, tu), idx, x16, gate, tab, expand, group, chunk8)


def _peer_v(tile, idx, w, tab, tu):
    T = idx.shape[0]
    tu = min(tu, T)
    expand, _, chunk8 = _peer_consts()
    smem, tok, full, resident = _peer_specs(tu)
    ncol = PEER_SLOTS * PAIR_ROWS
    return pl.pallas_call(
        _peer_v_kernel,
        grid=(T // tu,),
        in_specs=[smem, tok, tok, resident(tab), full(expand), full(chunk8)],
        out_specs=pl.BlockSpec((tu, 8, LANES), lambda i: (i, 0, 0)),
        out_shape=jax.ShapeDtypeStruct((T, 8, LANES), F32),
        scratch_shapes=[pltpu.VMEM((tu, ncol), F32),
                        pltpu.SMEM((PEER_SLOTS, tu), jnp.int32), pltpu.SemaphoreType.DMA(())],
        compiler_params=pltpu.CompilerParams(
            dimension_semantics=("parallel",), vmem_limit_bytes=VMEM_LIMIT),
    )(_slot_major(tile, tu), idx, w, tab, expand, chunk8)


def _final_kernel(x_ref, p_ref, g_ref, o_ref):
    o_ref[...] = _rms(x_ref[...] + p_ref[...], g_ref[...])


def _final(x1, peer_out, g, tm):
    T, D = x1.shape
    tm = min(tm, T)
    tok = pl.BlockSpec((tm, D), lambda i: (i, 0))
    return pl.pallas_call(
        _final_kernel,
        grid=(T // tm,),
        in_specs=[tok, tok, pl.BlockSpec((1, D), lambda i: (0, 0))],
        out_specs=tok,
        out_shape=jax.ShapeDtypeStruct((T, D), F32),
        compiler_params=pltpu.CompilerParams(dimension_semantics=("parallel",)),
    )(x1, peer_out, g.reshape(1, D))


def kernel(x, mem, norm_mix_g, w_in, hg_lb, hg_norm_g, sc_conv_w, mem_norm_g, w_mem_kv,
           w_branch, w_out, norm_ffn_g, peer_w_q, peer_sub_keys, peer_u, peer_v, final_norm_g):
    B, S, D = x.shape
    T = B * S
    assert w_in.shape[0] == 1, "single-layer block"
    n_main = 9 * BRANCH_WIDTH
    lb_table = jnp.cumsum(jax.nn.softmax(hg_lb.astype(F32), axis=0), axis=0)
    xf = x.reshape(T, D)
    memf = mem.reshape(B * N_MEM, D)
    w_in_b = w_in[0].astype(BF16)
    proj = _norm_matmul(xf, norm_mix_g[0], w_in_b[:, :n_main], 1024, 768)
    gates = _norm_matmul(xf, norm_mix_g[0], w_in_b[:, n_main:], 1024, 768)
    kv = _norm_matmul(memf, mem_norm_g[0], w_mem_kv[0].astype(BF16), 1024, 512)
    y_hg = _hgrn(proj, lb_table[0], hg_norm_g[0], B, S)
    x1 = _mix(xf, proj, gates, y_hg, kv, sc_conv_w[0], w_branch[0].astype(BF16),
              w_out[0].astype(BF16), B, S, 256)
    h_hi, h_lo, idx, tile, gate = _route(x1, norm_ffn_g[0], peer_w_q[0].astype(BF16),
                                         peer_sub_keys[0], 256)
    x16 = jnp.concatenate([h_hi.reshape(T, 8, LANES), h_lo.reshape(T, 8, LANES)], axis=1)
    w = _peer_u(tile, idx, x16, gate, _peer_table(peer_u[0]), 64)
    peer_out = _peer_v(tile, idx, w, _peer_table(peer_v[0]), 64)
    return _final(x1, peer_out.reshape(T, D), final_norm_g, 512).reshape(B, S, D)
```

```python
import functools

import jax
import jax.numpy as jnp
from jax import lax
from jax.experimental import pallas as pl
from jax.experimental.pallas import tpu as pltpu

F32 = jnp.float32
BF16 = jnp.bfloat16
HIGHEST = lax.Precision.HIGHEST

EPS = 1e-6
LANES = 128
HG_HEADS = 4
HG_CHUNK = 64
HG_CHUNKS_PER_ITER = 2
HG_STATE_UNROLL = 4
MX_HEADS = 4
BRANCH_WIDTH = 512
N_MEM = 256
PEER_HEADS = 8
PEER_NKEYS = 128
PEER_TOPK = 16
PEER_SLOTS = PEER_HEADS * PEER_TOPK
PAIR_ROWS = 16
TILE_WORD_ROWS = PAIR_ROWS // 2
TOKENS_PER_ITER = 16
VMEM_LIMIT = 48 * 1024 * 1024

NT_DIMS = (((1,), (1,)), ((), ()))
TN_DIMS = (((0,), (0,)), ((), ()))


def _rms(x, g):
    return x * lax.rsqrt(jnp.mean(x * x, axis=-1, keepdims=True) + EPS) * g


def _norm_matmul_kernel(x_ref, g_ref, w_ref, o_ref, h_ref):
    @pl.when(pl.program_id(1) == 0)
    def _():
        h_ref[...] = _rms(x_ref[...], g_ref[...]).astype(BF16)

    o_ref[...] = jnp.dot(h_ref[...], w_ref[...], preferred_element_type=F32)


def _norm_matmul(x, g, w_bf16, tm, tn):
    T, D = x.shape
    N = w_bf16.shape[1]
    tm = min(tm, T)
    return pl.pallas_call(
        _norm_matmul_kernel,
        grid=(T // tm, N // tn),
        in_specs=[
            pl.BlockSpec((tm, D), lambda i, j: (i, 0)),
            pl.BlockSpec((1, D), lambda i, j: (0, 0)),
            pl.BlockSpec((D, tn), lambda i, j: (0, j)),
        ],
        out_specs=pl.BlockSpec((tm, tn), lambda i, j: (i, j)),
        out_shape=jax.ShapeDtypeStruct((T, N), F32),
        scratch_shapes=[pltpu.VMEM((tm, D), BF16)],
        compiler_params=pltpu.CompilerParams(
            dimension_semantics=("parallel", "arbitrary"), vmem_limit_bytes=VMEM_LIMIT),
    )(x, g.reshape(1, D), w_bf16)


def _hgrn_kernel(q_ref, v_ref, zf_ref, zb_ref, g_ref, lb_ref, ng_ref, o_ref,
                 st_ref, out_ref, qs_ref, kk_ref, cum_ref, sc_ref, qe_ref, dst_ref, dec_ref):
    S = q_ref.shape[0]
    C = HG_CHUNK
    n = S // C
    row = lax.broadcasted_iota(jnp.int32, (C, C), 0)
    col = lax.broadcasted_iota(jnp.int32, (C, C), 1)
    keeps = (col <= row, col >= row)
    z_refs = (zf_ref, zb_ref)
    unroll = HG_CHUNKS_PER_ITER

    def chunks(fn):
        def body(i, carry):
            for j in range(unroll):
                sl = pl.ds(pl.multiple_of((i * unroll + j) * C, C), C)
                for d in range(2):
                    fn(i * unroll + j, sl, d)
            return carry
        lax.fori_loop(0, n // unroll, body, 0)

    qs_ref[...] = jax.nn.silu(q_ref[...])

    def decay_pass(c, sl, d):
        lb = lb_ref[d:d + 1, :]
        z = z_refs[d][sl, :]
        lf = jnp.log(lb + (1.0 - lb) * jax.nn.sigmoid(z))
        kk_ref[d, sl, :] = (1.0 - lb) * jax.nn.sigmoid(-z)
        tri = keeps[d].astype(BF16)
        hi = lf.astype(BF16)
        r1 = lf - hi.astype(F32)
        mid = r1.astype(BF16)
        lo = (r1 - mid.astype(F32)).astype(BF16)
        cum_ref[d, sl, :] = (jnp.dot(tri, hi, preferred_element_type=F32)
                             + jnp.dot(tri, mid, preferred_element_type=F32)
                             + jnp.dot(tri, lo, preferred_element_type=F32))

    def score_pass(c, sl, d):
        G = cum_ref[d, sl, :]
        kk = kk_ref[d, sl, :]
        q = qs_ref[sl, :]
        mid = C // 2 if d else C // 2 - 1
        last = 0 if d else C - 1
        g_mid = G[mid:mid + 1, :]
        g_last = G[last:last + 1, :]
        q_r = q * jnp.exp(G - g_mid)
        k_r = kk * jnp.exp(g_mid - G)
        scores = lax.dot_general(q_r, k_r, NT_DIMS, preferred_element_type=F32)
        sc_ref[d, sl, :] = jnp.where(keeps[d], scores, 0.0)
        qe_ref[d, sl, :] = q * jnp.exp(G)
        dst_ref[d, c] = lax.dot_general(v_ref[sl, :], kk * jnp.exp(g_last - G), TN_DIMS,
                                        preferred_element_type=F32)
        dec_ref[d, c] = jnp.broadcast_to(jnp.exp(g_last), (8, LANES))

    def intra_pass(c, sl, d):
        out_ref[d, sl, :] = jnp.dot(sc_ref[d, sl, :], v_ref[sl, :], preferred_element_type=F32)

    chunks(decay_pass)
    chunks(score_pass)
    chunks(intra_pass)
    st_ref[...] = jnp.zeros_like(st_ref)

    def state_body(i, carry):
        for j in range(HG_STATE_UNROLL):
            step = i * HG_STATE_UNROLL + j
            for d, c in ((0, step), (1, n - 1 - step)):
                sl = pl.ds(pl.multiple_of(c * C, C), C)
                st = st_ref[d]
                out_ref[d, sl, :] += lax.dot_general(qe_ref[d, sl, :], st, NT_DIMS,
                                                     preferred_element_type=F32)
                st_ref[d] = st * dec_ref[d, c][0:1, :] + dst_ref[d, c]
        return carry

    lax.fori_loop(0, n // HG_STATE_UNROLL, state_body, 0)
    o = out_ref[0] + out_ref[1]
    o = o * lax.rsqrt(jnp.mean(o * o, axis=-1, keepdims=True) + EPS)
    o_ref[...] = o * ng_ref[...] * jax.nn.silu(g_ref[...])


def _hgrn(proj, lb, ng, B, S):
    T = B * S
    H = HG_HEADS
    sec = lambda s: pl.BlockSpec((S, LANES), lambda b, h, s=s: (b, s * H + h))
    return pl.pallas_call(
        _hgrn_kernel,
        grid=(B, H),
        in_specs=[sec(0), sec(1), sec(2), sec(3), sec(4),
                  pl.BlockSpec((2, LANES), lambda b, h: (0, h)),
                  pl.BlockSpec((1, LANES), lambda b, h: (0, h))],
        out_specs=pl.BlockSpec((S, LANES), lambda b, h: (b, h)),
        out_shape=jax.ShapeDtypeStruct((T, H * LANES), F32),
        scratch_shapes=[pltpu.VMEM((2, LANES, LANES), F32),
                        pltpu.VMEM((2, S, LANES), F32),
                        pltpu.VMEM((S, LANES), F32),
                        pltpu.VMEM((2, S, LANES), F32),
                        pltpu.VMEM((2, S, LANES), F32),
                        pltpu.VMEM((2, S, HG_CHUNK), F32),
                        pltpu.VMEM((2, S, LANES), F32),
                        pltpu.VMEM((2, S // HG_CHUNK, LANES, LANES), F32),
                        pltpu.VMEM((2, S // HG_CHUNK, 8, LANES), F32)],
        compiler_params=pltpu.CompilerParams(
            dimension_semantics=("parallel", "parallel"), vmem_limit_bytes=VMEM_LIMIT),
    )(proj, proj, proj, proj, proj, lb, ng.reshape(1, -1))


def _mix_kernel(x_ref, sb_ref, sc_ref, sh_ref, scp_ref, shp_ref, scn_ref, shn_ref, mq_ref,
                gates_ref, yhg_ref, kv_ref, cw_ref, wb_ref, wo_ref, o_ref):
    i = pl.program_id(1)
    ni = pl.num_programs(1)
    ts = x_ref.shape[0]
    W = BRANCH_WIDTH
    u = sc_ref[...] * sh_ref[...]
    u_before = jnp.where(i > 0, scp_ref[7:8, :] * shp_ref[7:8, :], 0.0)
    u_after = jnp.where(i < ni - 1, scn_ref[0:1, :] * shn_ref[0:1, :], 0.0)
    rows = lax.broadcasted_iota(jnp.int32, (ts, W), 0)
    u_prev = jnp.where(rows == 0, u_before, pltpu.roll(u, 1, 0))
    u_next = jnp.where(rows == ts - 1, u_after, pltpu.roll(u, ts - 1, 0))
    cw = cw_ref[...]
    y_sc = sb_ref[...] * (cw[0:1, :] * u_prev + cw[1:2, :] * u + cw[2:3, :] * u_next)

    heads = []
    for h in range(MX_HEADS):
        qh = mq_ref[:, h * LANES:(h + 1) * LANES]
        kh = kv_ref[:, h * LANES:(h + 1) * LANES]
        vh = kv_ref[:, W + h * LANES:W + (h + 1) * LANES]
        logits = lax.dot_general(qh, kh, NT_DIMS, preferred_element_type=F32) * (LANES ** -0.5)
        e = jnp.exp(logits - jnp.max(logits, axis=-1, keepdims=True))
        p = e / jnp.sum(e, axis=-1, keepdims=True)
        heads.append(jnp.dot(p, vh, preferred_element_type=F32))
    y_mx = jnp.concatenate(heads, axis=1)

    D = x_ref.shape[1]
    merged = jnp.zeros((ts, D), F32)
    for n, y in enumerate((yhg_ref[...], y_sc, y_mx)):
        pr = jnp.dot(y.astype(BF16), wb_ref[n], preferred_element_type=F32)
        merged = merged + jax.nn.sigmoid(gates_ref[:, n * D:(n + 1) * D]) * pr
    out = jnp.dot(merged.astype(BF16), wo_ref[...], preferred_element_type=F32)
    o_ref[...] = x_ref[...] + out


def _mix(x, proj, gates, y_hg, kv, conv_w, wb_bf16, wo_bf16, B, S, ts):
    T, D = x.shape
    W = BRANCH_WIDTH
    ts = min(ts, S)
    nt = S // ts
    r8 = ts // 8
    tile = lambda b, i: b * nt + i
    sec = lambda s: pl.BlockSpec((ts, W), lambda b, i, s=s: (tile(b, i), s))
    before = lambda s: pl.BlockSpec(
        (8, W), lambda b, i, s=s: (jnp.maximum(tile(b, i) * r8 - 1, 0), s))
    after = lambda s: pl.BlockSpec(
        (8, W), lambda b, i, s=s: (jnp.minimum((tile(b, i) + 1) * r8, T // 8 - 1), s))
    return pl.pallas_call(
        _mix_kernel,
        grid=(B, nt),
        in_specs=[
            pl.BlockSpec((ts, D), lambda b, i: (tile(b, i), 0)),
            sec(5), sec(6), sec(7), before(6), before(7), after(6), after(7), sec(8),
            pl.BlockSpec((ts, 3 * D), lambda b, i: (tile(b, i), 0)),
            pl.BlockSpec((ts, W), lambda b, i: (tile(b, i), 0)),
            pl.BlockSpec((N_MEM, 2 * W), lambda b, i: (b, 0)),
            pl.BlockSpec((3, W), lambda b, i: (0, 0)),
            pl.BlockSpec((3, W, D), lambda b, i: (0, 0, 0)),
            pl.BlockSpec((D, D), lambda b, i: (0, 0)),
        ],
        out_specs=pl.BlockSpec((ts, D), lambda b, i: (tile(b, i), 0)),
        out_shape=jax.ShapeDtypeStruct((T, D), F32),
        compiler_params=pltpu.CompilerParams(
            dimension_semantics=("parallel", "parallel"), vmem_limit_bytes=VMEM_LIMIT),
    )(x, proj, proj, proj, proj, proj, proj, proj, proj, gates, y_hg, kv, conv_w, wb_bf16, wo_bf16)


def _topk_rows(s, k, payload=None):
    n = s.shape[0]
    iota = lax.broadcasted_iota(jnp.int32, s.shape, 0)
    vals, ids = [], []
    for _ in range(k):
        m = jnp.max(s, axis=0, keepdims=True)
        am = jnp.min(jnp.where(s == m, iota, n), axis=0, keepdims=True)
        hit = iota == am
        vals.append(m)
        if payload is None:
            ids.append(am)
        else:
            ids.append(jnp.sum(jnp.where(hit, payload, 0), axis=0, keepdims=True))
        s = jnp.where(hit, -jnp.inf, s)
    return jnp.concatenate(vals, axis=0), jnp.concatenate(ids, axis=0)


def _pair_candidates(top_s, top_i):
    K = PEER_TOPK
    assert K == 16
    (s0, s1), (i0, i1) = top_s, top_i
    blocks_s = [s0[0:1, :] + s1]
    blocks_i = [i0[0:1, :] * PEER_NKEYS + i1]
    b_row = lax.broadcasted_iota(jnp.int32, (8, s0.shape[1]), 0)
    for a in range(1, 8):
        blocks_s.append(jnp.where(b_row < K // (a + 1), s0[a:a + 1, :] + s1[0:8, :], -jnp.inf))
        blocks_i.append(i0[a:a + 1, :] * PEER_NKEYS + i1[0:8, :])
    blocks_s.append(s0[8:16, :] + s1[0:1, :])
    blocks_i.append(i0[8:16, :] * PEER_NKEYS + i1[0:1, :])
    return jnp.concatenate(blocks_s, axis=0), jnp.concatenate(blocks_i, axis=0)


def _route_kernel(x_ref, g_ref, wq_ref, keys_ref, hhi_ref, hlo_ref, idx_ref, tile_ref, gate_ref):
    K = PEER_TOPK
    h = _rms(x_ref[...], g_ref[...])
    h_hi = h.astype(BF16)
    hhi_ref[...] = h_hi
    hlo_ref[...] = (h - h_hi.astype(F32)).astype(BF16)
    q = jnp.dot(h_hi, wq_ref[...], preferred_element_type=F32)
    idx_rows, gate_rows = [], []
    for hd in range(PEER_HEADS):
        top_s, top_i = [], []
        for p in range(2):
            c0 = (hd * 2 + p) * LANES
            s_t = lax.dot_general(keys_ref[hd, p], q[:, c0:c0 + LANES], NT_DIMS,
                                  preferred_element_type=F32)
            vs, is_ = _topk_rows(s_t, K)
            top_s.append(vs)
            top_i.append(is_)
        cand_s, cand_i = _pair_candidates(top_s, top_i)
        best_s, best_i = _topk_rows(cand_s, K, payload=cand_i)
        e = jnp.exp(best_s - best_s[0:1, :])
        gate_rows.append(e / jnp.sum(e, axis=0, keepdims=True))
        idx_rows.append(best_i)
    idx = jnp.concatenate(idx_rows, axis=0).T
    idx_ref[...] = idx
    tile_ref[...] = lax.shift_right_logical(idx, 1) * TILE_WORD_ROWS
    gate_ref[...] = jnp.concatenate(gate_rows, axis=0).T


def _route(x1, g, wq_bf16, keys, tt):
    T, D = x1.shape
    tt = min(tt, T)
    NQ = wq_bf16.shape[1]
    tok = lambda w: pl.BlockSpec((tt, w), lambda i: (i, 0))
    return pl.pallas_call(
        _route_kernel,
        grid=(T // tt,),
        in_specs=[tok(D), pl.BlockSpec((1, D), lambda i: (0, 0)),
                  pl.BlockSpec((D, NQ), lambda i: (0, 0)),
                  pl.BlockSpec(keys.shape, lambda i: (0, 0, 0, 0))],
        out_specs=[tok(D), tok(D), tok(PEER_SLOTS), tok(PEER_SLOTS), tok(PEER_SLOTS)],
        out_shape=[jax.ShapeDtypeStruct((T, D), BF16), jax.ShapeDtypeStruct((T, D), BF16),
                   jax.ShapeDtypeStruct((T, PEER_SLOTS), jnp.int32),
                   jax.ShapeDtypeStruct((T, PEER_SLOTS), jnp.int32),
                   jax.ShapeDtypeStruct((T, PEER_SLOTS), F32)],
        compiler_params=pltpu.CompilerParams(
            dimension_semantics=("parallel",), vmem_limit_bytes=VMEM_LIMIT),
    )(x1, g.reshape(1, D), wq_bf16, keys)


def _gather_tiles(tile_s, t, tab_ref):
    def tile(k):
        r = pl.multiple_of(tile_s.at[k][t], TILE_WORD_ROWS)
        return pltpu.bitcast(tab_ref[pl.ds(r, TILE_WORD_ROWS), :], BF16)

    return jnp.concatenate([tile(k) for k in range(PEER_SLOTS)], axis=0)


def _tile_fetch(tile_hbm, tile_s, sem):
    return pltpu.make_async_copy(tile_hbm.at[pl.program_id(0)], tile_s, sem)


def _slot_major(tile, tu):
    T, n = tile.shape
    return tile.reshape(T // tu, tu, n).transpose(0, 2, 1)


def _parity_onehot(idx_v):
    par = (idx_v & 1).astype(F32)
    return jnp.concatenate([1.0 - par, par], axis=1)


def _hi_lo_rows(a):
    hi = a.astype(BF16)
    lo = (a - hi.astype(F32)).astype(BF16)
    return jnp.concatenate([hi, lo], axis=0)


def _peer_u_kernel(tile_hbm, idx_v, x_ref, gate_ref, tab_ref, e_ref, g_ref, cm_ref, w_ref,
                   sel_ref, r_ref, tile_s, sem):
    tu = idx_v.shape[0]
    fetch = _tile_fetch(tile_hbm, tile_s, sem)
    fetch.start()
    sel_ref[...] = jnp.dot(_parity_onehot(idx_v[...]).astype(BF16), e_ref[...],
                           preferred_element_type=F32)
    fetch.wait()

    def token(t):
        m = _gather_tiles(tile_s, t, tab_ref)
        r = lax.dot_general(x_ref[t], m, NT_DIMS, preferred_element_type=F32)
        r8 = (r[0:8, :] + r[8:16, :]) * cm_ref[...]
        r_ref[pl.ds(t, 1), :] = jnp.sum(r8, axis=0, keepdims=True) * sel_ref[pl.ds(t, 1), :]

    def body(i, carry):
        for j in range(TOKENS_PER_ITER):
            token(i * TOKENS_PER_ITER + j)
        return carry

    lax.fori_loop(0, tu // TOKENS_PER_ITER, body, 0)
    r_all = r_ref[...]
    hi = r_all.astype(BF16)
    lo = (r_all - hi.astype(F32)).astype(BF16)
    a = (jnp.dot(hi, g_ref[...], preferred_element_type=F32)
         + jnp.dot(lo, g_ref[...], preferred_element_type=F32))
    w_ref[...] = gate_ref[...] * (0.5 * a * (1.0 + lax.erf(a * (2.0 ** -0.5))))


def _peer_v_kernel(tile_hbm, idx_v, w_ref, tab_ref, e_ref, cm_ref, o_ref, wexp_ref, tile_s, sem):
    tu = idx_v.shape[0]
    fetch = _tile_fetch(tile_hbm, tile_s, sem)
    fetch.start()
    w = w_ref[...]
    w2 = jnp.concatenate([w, w], axis=1) * _parity_onehot(idx_v[...])
    hi = w2.astype(BF16)
    lo = (w2 - hi.astype(F32)).astype(BF16)
    wexp_ref[...] = (jnp.dot(hi, e_ref[...], preferred_element_type=F32)
                     + jnp.dot(lo, e_ref[...], preferred_element_type=F32))
    fetch.wait()

    def token(t):
        m = _gather_tiles(tile_s, t, tab_ref)
        w8 = wexp_ref[pl.ds(t, 1), :] * cm_ref[...]
        o = jnp.dot(_hi_lo_rows(w8), m, preferred_element_type=F32)
        o_ref[t] = o[0:8, :] + o[8:16, :]

    def body(i, carry):
        for j in range(TOKENS_PER_ITER):
            token(i * TOKENS_PER_ITER + j)
        return carry

    lax.fori_loop(0, tu // TOKENS_PER_ITER, body, 0)


def _peer_consts():
    ncol = PEER_SLOTS * PAIR_ROWS
    col = jnp.arange(ncol)
    slot = col // PAIR_ROWS
    half = (col // 8) % 2
    row = jnp.arange(2 * PEER_SLOTS)
    expand = ((row[:, None] % PEER_SLOTS == slot[None, :])
              & (row[:, None] // PEER_SLOTS == half[None, :])).astype(BF16)
    group = (slot[:, None] == jnp.arange(PEER_SLOTS)[None, :]).astype(BF16)
    chunk8 = (jnp.arange(8)[:, None] == (col % 8)[None, :]).astype(F32)
    return expand, group, chunk8


def _peer_table(w):
    n, d = w.shape
    t = w.astype(BF16).reshape(n // 2, TILE_WORD_ROWS, 2, LANES).transpose(0, 1, 3, 2)
    return lax.bitcast_convert_type(t, jnp.int32).reshape(n // 2 * TILE_WORD_ROWS, LANES)


def _peer_specs(tu):
    smem = pl.BlockSpec(memory_space=pl.ANY)
    tok = pl.BlockSpec((tu, PEER_SLOTS), lambda i: (i, 0))
    full = lambda a: pl.BlockSpec(a.shape, lambda i: (0,) * a.ndim)
    resident = lambda a: pl.BlockSpec(a.shape, lambda i: (0,) * a.ndim,
                                      pipeline_mode=pl.Buffered(1))
    return smem, tok, full, resident


def _peer_u(tile, idx, x16, gate, tab, tu):
    T = idx.shape[0]
    tu = min(tu, T)
    expand, group, chunk8 = _peer_consts()
    smem, tok, full, resident = _peer_specs(tu)
    ncol = PEER_SLOTS * PAIR_ROWS
    return pl.pallas_call(
        _peer_u_kernel,
        grid=(T // tu,),
        in_specs=[smem, tok, pl.BlockSpec((tu, 16, LANES), lambda i: (i, 0, 0)), tok,
                  resident(tab), full(expand), full(group), full(chunk8)],
        out_specs=tok,
        out_shape=jax.ShapeDtypeStruct((T, PEER_SLOTS), F32),
        scratch_shapes=[pltpu.VMEM((tu, ncol), F32), pltpu.VMEM((tu, ncol), F32),
                        pltpu.SMEM((PEER_SLOTS, tu), jnp.int32), pltpu.SemaphoreType.DMA(())],
        compiler_params=pltpu.CompilerParams(
            dimension_semantics=("parallel",), vmem_limit_bytes=VMEM_LIMIT),
    )(_slot_major(tile, tu), idx, x16, gate, tab, expand, group, chunk8)


def _peer_v(tile, idx, w, tab, tu):
    T = idx.shape[0]
    tu = min(tu, T)
    expand, _, chunk8 = _peer_consts()
    smem, tok, full, resident = _peer_specs(tu)
    ncol = PEER_SLOTS * PAIR_ROWS
    return pl.pallas_call(
        _peer_v_kernel,
        grid=(T // tu,),
        in_specs=[smem, tok, tok, resident(tab), full(expand), full(chunk8)],
        out_specs=pl.BlockSpec((tu, 8, LANES), lambda i: (i, 0, 0)),
        out_shape=jax.ShapeDtypeStruct((T, 8, LANES), F32),
        scratch_shapes=[pltpu.VMEM((tu, ncol), F32),
                        pltpu.SMEM((PEER_SLOTS, tu), jnp.int32), pltpu.SemaphoreType.DMA(())],
        compiler_params=pltpu.CompilerParams(
            dimension_semantics=("parallel",), vmem_limit_bytes=VMEM_LIMIT),
    )(_slot_major(tile, tu), idx, w, tab, expand, chunk8)


def _final_kernel(x_ref, p_ref, g_ref, o_ref):
    o_ref[...] = _rms(x_ref[...] + p_ref[...], g_ref[...])


def _final(x1, peer_out, g, tm):
    T, D = x1.shape
    tm = min(tm, T)
    tok = pl.BlockSpec((tm, D), lambda i: (i, 0))
    return pl.pallas_call(
        _final_kernel,
        grid=(T // tm,),
        in_specs=[tok, tok, pl.BlockSpec((1, D), lambda i: (0, 0))],
        out_specs=tok,
        out_shape=jax.ShapeDtypeStruct((T, D), F32),
        compiler_params=pltpu.CompilerParams(dimension_semantics=("parallel",)),
    )(x1, peer_out, g.reshape(1, D))


def kernel(x, mem, norm_mix_g, w_in, hg_lb, hg_norm_g, sc_conv_w, mem_norm_g, w_mem_kv,
           w_branch, w_out, norm_ffn_g, peer_w_q, peer_sub_keys, peer_u, peer_v, final_norm_g):
    B, S, D = x.shape
    T = B * S
    assert w_in.shape[0] == 1, "single-layer block"
    n_main = 9 * BRANCH_WIDTH
    lb_table = jnp.cumsum(jax.nn.softmax(hg_lb.astype(F32), axis=0), axis=0)
    xf = x.reshape(T, D)
    memf = mem.reshape(B * N_MEM, D)
    w_in_b = w_in[0].astype(BF16)
    proj = _norm_matmul(xf, norm_mix_g[0], w_in_b[:, :n_main], 1024, 768)
    gates = _norm_matmul(xf, norm_mix_g[0], w_in_b[:, n_main:], 1024, 768)
    kv = _norm_matmul(memf, mem_norm_g[0], w_mem_kv[0].astype(BF16), 1024, 512)
    y_hg = _hgrn(proj, lb_table[0], hg_norm_g[0], B, S)
    x1 = _mix(xf, proj, gates, y_hg, kv, sc_conv_w[0], w_branch[0].astype(BF16),
              w_out[0].astype(BF16), B, S, 256)
    h_hi, h_lo, idx, tile, gate = _route(x1, norm_ffn_g[0], peer_w_q[0].astype(BF16),
                                         peer_sub_keys[0], 256)
    x16 = jnp.concatenate([h_hi.reshape(T, 8, LANES), h_lo.reshape(T, 8, LANES)], axis=1)
    w = _peer_u(tile, idx, x16, gate, _peer_table(peer_u[0]), 64)
    peer_out = _peer_v(tile, idx, w, _peer_table(peer_v[0]), 64)
    return _final(x1, peer_out.reshape(T, D), final_norm_g, 512).reshape(B, S, D)
```

```python
import functools

import jax
import jax.numpy as jnp
from jax import lax
from jax.experimental import pallas as pl
from jax.experimental.pallas import tpu as pltpu

F32 = jnp.float32
BF16 = jnp.bfloat16
HIGHEST = lax.Precision.HIGHEST

EPS = 1e-6
LANES = 128
HG_HEADS = 4
HG_CHUNK = 64
HG_CHUNKS_PER_ITER = 2
HG_STATE_UNROLL = 4
MX_HEADS = 4
BRANCH_WIDTH = 512
N_MEM = 256
PEER_HEADS = 8
PEER_NKEYS = 128
PEER_TOPK = 16
PEER_SLOTS = PEER_HEADS * PEER_TOPK
PAIR_ROWS = 16
TILE_WORD_ROWS = PAIR_ROWS // 2
PEER_TOKEN_TILE = 128
TOKENS_PER_ITER = 16
VMEM_LIMIT = 48 * 1024 * 1024

NT_DIMS = (((1,), (1,)), ((), ()))
TN_DIMS = (((0,), (0,)), ((), ()))


def _rms(x, g):
    return x * lax.rsqrt(jnp.mean(x * x, axis=-1, keepdims=True) + EPS) * g


def _norm_matmul_kernel(x_ref, g_ref, w_ref, o_ref, h_ref):
    @pl.when(pl.program_id(1) == 0)
    def _():
        h_ref[...] = _rms(x_ref[...], g_ref[...]).astype(BF16)

    o_ref[...] = jnp.dot(h_ref[...], w_ref[...], preferred_element_type=F32)


def _norm_matmul(x, g, w_bf16, tm, tn):
    T, D = x.shape
    N = w_bf16.shape[1]
    tm = min(tm, T)
    return pl.pallas_call(
        _norm_matmul_kernel,
        grid=(T // tm, N // tn),
        in_specs=[
            pl.BlockSpec((tm, D), lambda i, j: (i, 0)),
            pl.BlockSpec((1, D), lambda i, j: (0, 0)),
            pl.BlockSpec((D, tn), lambda i, j: (0, j)),
        ],
        out_specs=pl.BlockSpec((tm, tn), lambda i, j: (i, j)),
        out_shape=jax.ShapeDtypeStruct((T, N), F32),
        scratch_shapes=[pltpu.VMEM((tm, D), BF16)],
        compiler_params=pltpu.CompilerParams(
            dimension_semantics=("parallel", "arbitrary"), vmem_limit_bytes=VMEM_LIMIT),
    )(x, g.reshape(1, D), w_bf16)


def _hgrn_kernel(q_ref, v_ref, zf_ref, zb_ref, g_ref, lb_ref, ng_ref, o_ref,
                 st_ref, out_ref, qs_ref, kk_ref, cum_ref, sc_ref, qe_ref, dst_ref, dec_ref):
    S = q_ref.shape[0]
    C = HG_CHUNK
    n = S // C
    row = lax.broadcasted_iota(jnp.int32, (C, C), 0)
    col = lax.broadcasted_iota(jnp.int32, (C, C), 1)
    keeps = (col <= row, col >= row)
    z_refs = (zf_ref, zb_ref)
    unroll = HG_CHUNKS_PER_ITER

    def chunks(fn):
        def body(i, carry):
            for j in range(unroll):
                sl = pl.ds(pl.multiple_of((i * unroll + j) * C, C), C)
                for d in range(2):
                    fn(i * unroll + j, sl, d)
            return carry
        lax.fori_loop(0, n // unroll, body, 0)

    qs_ref[...] = jax.nn.silu(q_ref[...])

    def decay_pass(c, sl, d):
        lb = lb_ref[d:d + 1, :]
        z = z_refs[d][sl, :]
        lf = jnp.log(lb + (1.0 - lb) * jax.nn.sigmoid(z))
        kk_ref[d, sl, :] = (1.0 - lb) * jax.nn.sigmoid(-z)
        tri = keeps[d].astype(BF16)
        hi = lf.astype(BF16)
        r1 = lf - hi.astype(F32)
        mid = r1.astype(BF16)
        lo = (r1 - mid.astype(F32)).astype(BF16)
        cum_ref[d, sl, :] = (jnp.dot(tri, hi, preferred_element_type=F32)
                             + jnp.dot(tri, mid, preferred_element_type=F32)
                             + jnp.dot(tri, lo, preferred_element_type=F32))

    def score_pass(c, sl, d):
        G = cum_ref[d, sl, :]
        kk = kk_ref[d, sl, :]
        q = qs_ref[sl, :]
        mid = C // 2 if d else C // 2 - 1
        last = 0 if d else C - 1
        g_mid = G[mid:mid + 1, :]
        g_last = G[last:last + 1, :]
        q_r = q * jnp.exp(G - g_mid)
        k_r = kk * jnp.exp(g_mid - G)
        scores = lax.dot_general(q_r, k_r, NT_DIMS, preferred_element_type=F32)
        sc_ref[d, sl, :] = jnp.where(keeps[d], scores, 0.0)
        qe_ref[d, sl, :] = q * jnp.exp(G)
        dst_ref[d, c] = lax.dot_general(v_ref[sl, :], kk * jnp.exp(g_last - G), TN_DIMS,
                                        preferred_element_type=F32)
        dec_ref[d, c] = jnp.broadcast_to(jnp.exp(g_last), (8, LANES))

    def intra_pass(c, sl, d):
        out_ref[d, sl, :] = jnp.dot(sc_ref[d, sl, :], v_ref[sl, :], preferred_element_type=F32)

    chunks(decay_pass)
    chunks(score_pass)
    chunks(intra_pass)
    st_ref[...] = jnp.zeros_like(st_ref)

    def state_body(i, carry):
        for j in range(HG_STATE_UNROLL):
            step = i * HG_STATE_UNROLL + j
            for d, c in ((0, step), (1, n - 1 - step)):
                sl = pl.ds(pl.multiple_of(c * C, C), C)
                st = st_ref[d]
                out_ref[d, sl, :] += lax.dot_general(qe_ref[d, sl, :], st, NT_DIMS,
                                                     preferred_element_type=F32)
                st_ref[d] = st * dec_ref[d, c][0:1, :] + dst_ref[d, c]
        return carry

    lax.fori_loop(0, n // HG_STATE_UNROLL, state_body, 0)
    o = out_ref[0] + out_ref[1]
    o = o * lax.rsqrt(jnp.mean(o * o, axis=-1, keepdims=True) + EPS)
    o_ref[...] = o * ng_ref[...] * jax.nn.silu(g_ref[...])


def _hgrn(proj, lb, ng, B, S):
    T = B * S
    H = HG_HEADS
    sec = lambda s: pl.BlockSpec((S, LANES), lambda b, h, s=s: (b, s * H + h))
    return pl.pallas_call(
        _hgrn_kernel,
        grid=(B, H),
        in_specs=[sec(0), sec(1), sec(2), sec(3), sec(4),
                  pl.BlockSpec((2, LANES), lambda b, h: (0, h)),
                  pl.BlockSpec((1, LANES), lambda b, h: (0, h))],
        out_specs=pl.BlockSpec((S, LANES), lambda b, h: (b, h)),
        out_shape=jax.ShapeDtypeStruct((T, H * LANES), F32),
        scratch_shapes=[pltpu.VMEM((2, LANES, LANES), F32),
                        pltpu.VMEM((2, S, LANES), F32),
                        pltpu.VMEM((S, LANES), F32),
                        pltpu.VMEM((2, S, LANES), F32),
                        pltpu.VMEM((2, S, LANES), F32),
                        pltpu.VMEM((2, S, HG_CHUNK), F32),
                        pltpu.VMEM((2, S, LANES), F32),
                        pltpu.VMEM((2, S // HG_CHUNK, LANES, LANES), F32),
                        pltpu.VMEM((2, S // HG_CHUNK, 8, LANES), F32)],
        compiler_params=pltpu.CompilerParams(
            dimension_semantics=("parallel", "parallel"), vmem_limit_bytes=VMEM_LIMIT),
    )(proj, proj, proj, proj, proj, lb, ng.reshape(1, -1))


def _mix_kernel(x_ref, sb_ref, sc_ref, sh_ref, scp_ref, shp_ref, scn_ref, shn_ref, mq_ref,
                gates_ref, yhg_ref, kv_ref, cw_ref, wb_ref, wo_ref, o_ref):
    i = pl.program_id(1)
    ni = pl.num_programs(1)
    ts = x_ref.shape[0]
    W = BRANCH_WIDTH
    u = sc_ref[...] * sh_ref[...]
    u_before = jnp.where(i > 0, scp_ref[7:8, :] * shp_ref[7:8, :], 0.0)
    u_after = jnp.where(i < ni - 1, scn_ref[0:1, :] * shn_ref[0:1, :], 0.0)
    rows = lax.broadcasted_iota(jnp.int32, (ts, W), 0)
    u_prev = jnp.where(rows == 0, u_before, pltpu.roll(u, 1, 0))
    u_next = jnp.where(rows == ts - 1, u_after, pltpu.roll(u, ts - 1, 0))
    cw = cw_ref[...]
    y_sc = sb_ref[...] * (cw[0:1, :] * u_prev + cw[1:2, :] * u + cw[2:3, :] * u_next)

    heads = []
    for h in range(MX_HEADS):
        qh = mq_ref[:, h * LANES:(h + 1) * LANES]
        kh = kv_ref[:, h * LANES:(h + 1) * LANES]
        vh = kv_ref[:, W + h * LANES:W + (h + 1) * LANES]
        logits = lax.dot_general(qh, kh, NT_DIMS, preferred_element_type=F32) * (LANES ** -0.5)
        e = jnp.exp(logits - jnp.max(logits, axis=-1, keepdims=True))
        p = e / jnp.sum(e, axis=-1, keepdims=True)
        heads.append(jnp.dot(p, vh, preferred_element_type=F32))
    y_mx = jnp.concatenate(heads, axis=1)

    D = x_ref.shape[1]
    merged = jnp.zeros((ts, D), F32)
    for n, y in enumerate((yhg_ref[...], y_sc, y_mx)):
        pr = jnp.dot(y.astype(BF16), wb_ref[n], preferred_element_type=F32)
        merged = merged + jax.nn.sigmoid(gates_ref[:, n * D:(n + 1) * D]) * pr
    out = jnp.dot(merged.astype(BF16), wo_ref[...], preferred_element_type=F32)
    o_ref[...] = x_ref[...] + out


def _mix(x, proj, gates, y_hg, kv, conv_w, wb_bf16, wo_bf16, B, S, ts):
    T, D = x.shape
    W = BRANCH_WIDTH
    ts = min(ts, S)
    nt = S // ts
    r8 = ts // 8
    tile = lambda b, i: b * nt + i
    sec = lambda s: pl.BlockSpec((ts, W), lambda b, i, s=s: (tile(b, i), s))
    before = lambda s: pl.BlockSpec(
        (8, W), lambda b, i, s=s: (jnp.maximum(tile(b, i) * r8 - 1, 0), s))
    after = lambda s: pl.BlockSpec(
        (8, W), lambda b, i, s=s: (jnp.minimum((tile(b, i) + 1) * r8, T // 8 - 1), s))
    return pl.pallas_call(
        _mix_kernel,
        grid=(B, nt),
        in_specs=[
            pl.BlockSpec((ts, D), lambda b, i: (tile(b, i), 0)),
            sec(5), sec(6), sec(7), before(6), before(7), after(6), after(7), sec(8),
            pl.BlockSpec((ts, 3 * D), lambda b, i: (tile(b, i), 0)),
            pl.BlockSpec((ts, W), lambda b, i: (tile(b, i), 0)),
            pl.BlockSpec((N_MEM, 2 * W), lambda b, i: (b, 0)),
            pl.BlockSpec((3, W), lambda b, i: (0, 0)),
            pl.BlockSpec((3, W, D), lambda b, i: (0, 0, 0)),
            pl.BlockSpec((D, D), lambda b, i: (0, 0)),
        ],
        out_specs=pl.BlockSpec((ts, D), lambda b, i: (tile(b, i), 0)),
        out_shape=jax.ShapeDtypeStruct((T, D), F32),
        compiler_params=pltpu.CompilerParams(
            dimension_semantics=("parallel", "parallel"), vmem_limit_bytes=VMEM_LIMIT),
    )(x, proj, proj, proj, proj, proj, proj, proj, proj, gates, y_hg, kv, conv_w, wb_bf16, wo_bf16)


def _topk_rows(s, k, payload=None):
    n = s.shape[0]
    iota = lax.broadcasted_iota(jnp.int32, s.shape, 0)
    vals, ids = [], []
    for _ in range(k):
        m = jnp.max(s, axis=0, keepdims=True)
        am = jnp.min(jnp.where(s == m, iota, n), axis=0, keepdims=True)
        hit = iota == am
        vals.append(m)
        if payload is None:
            ids.append(am)
        else:
            ids.append(jnp.sum(jnp.where(hit, payload, 0), axis=0, keepdims=True))
        s = jnp.where(hit, -jnp.inf, s)
    return jnp.concatenate(vals, axis=0), jnp.concatenate(ids, axis=0)


def _pair_candidates(top_s, top_i):
    K = PEER_TOPK
    assert K == 16
    (s0, s1), (i0, i1) = top_s, top_i
    blocks_s = [s0[0:1, :] + s1]
    blocks_i = [i0[0:1, :] * PEER_NKEYS + i1]
    b_row = lax.broadcasted_iota(jnp.int32, (8, s0.shape[1]), 0)
    for a in range(1, 8):
        blocks_s.append(jnp.where(b_row < K // (a + 1), s0[a:a + 1, :] + s1[0:8, :], -jnp.inf))
        blocks_i.append(i0[a:a + 1, :] * PEER_NKEYS + i1[0:8, :])
    blocks_s.append(s0[8:16, :] + s1[0:1, :])
    blocks_i.append(i0[8:16, :] * PEER_NKEYS + i1[0:1, :])
    return jnp.concatenate(blocks_s, axis=0), jnp.concatenate(blocks_i, axis=0)


def _route_kernel(x_ref, g_ref, wq_ref, keys_ref, hhi_ref, hlo_ref, idx_ref, tile_ref, gate_ref):
    K = PEER_TOPK
    h = _rms(x_ref[...], g_ref[...])
    h_hi = h.astype(BF16)
    hhi_ref[...] = h_hi
    hlo_ref[...] = (h - h_hi.astype(F32)).astype(BF16)
    q = jnp.dot(h_hi, wq_ref[...], preferred_element_type=F32)
    idx_rows, gate_rows = [], []
    for hd in range(PEER_HEADS):
        top_s, top_i = [], []
        for p in range(2):
            c0 = (hd * 2 + p) * LANES
            s_t = lax.dot_general(keys_ref[hd, p], q[:, c0:c0 + LANES], NT_DIMS,
                                  preferred_element_type=F32)
            vs, is_ = _topk_rows(s_t, K)
            top_s.append(vs)
            top_i.append(is_)
        cand_s, cand_i = _pair_candidates(top_s, top_i)
        best_s, best_i = _topk_rows(cand_s, K, payload=cand_i)
        e = jnp.exp(best_s - best_s[0:1, :])
        gate_rows.append(e / jnp.sum(e, axis=0, keepdims=True))
        idx_rows.append(best_i)
    idx_t = jnp.concatenate(idx_rows, axis=0)
    idx_ref[...] = idx_t.T
    tile_t = lax.shift_right_logical(idx_t, 1) * TILE_WORD_ROWS
    for j in range(tile_ref.shape[0]):
        tile_ref[j] = tile_t[:, j * PEER_TOKEN_TILE:(j + 1) * PEER_TOKEN_TILE]
    gate_ref[...] = jnp.concatenate(gate_rows, axis=0).T


def _route(x1, g, wq_bf16, keys, tt):
    T, D = x1.shape
    tt = min(tt, T)
    TU = PEER_TOKEN_TILE
    NQ = wq_bf16.shape[1]
    tok = lambda w: pl.BlockSpec((tt, w), lambda i: (i, 0))
    return pl.pallas_call(
        _route_kernel,
        grid=(T // tt,),
        in_specs=[tok(D), pl.BlockSpec((1, D), lambda i: (0, 0)),
                  pl.BlockSpec((D, NQ), lambda i: (0, 0)),
                  pl.BlockSpec(keys.shape, lambda i: (0, 0, 0, 0))],
        out_specs=[tok(D), tok(D), tok(PEER_SLOTS),
                   pl.BlockSpec((tt // TU, PEER_SLOTS, TU), lambda i: (i, 0, 0)),
                   tok(PEER_SLOTS)],
        out_shape=[jax.ShapeDtypeStruct((T, D), BF16), jax.ShapeDtypeStruct((T, D), BF16),
                   jax.ShapeDtypeStruct((T, PEER_SLOTS), jnp.int32),
                   jax.ShapeDtypeStruct((T // TU, PEER_SLOTS, TU), jnp.int32),
                   jax.ShapeDtypeStruct((T, PEER_SLOTS), F32)],
        compiler_params=pltpu.CompilerParams(
            dimension_semantics=("parallel",), vmem_limit_bytes=VMEM_LIMIT),
    )(x1, g.reshape(1, D), wq_bf16, keys)


def _gather_tiles(tile_s, t, tab_ref):
    def tile(k):
        r = pl.multiple_of(tile_s.at[k][t], TILE_WORD_ROWS)
        return pltpu.bitcast(tab_ref[pl.ds(r, TILE_WORD_ROWS), :], BF16)

    return jnp.concatenate([tile(k) for k in range(PEER_SLOTS)], axis=0)


def _tile_fetch(tile_hbm, tile_s, sem):
    return pltpu.make_async_copy(tile_hbm.at[pl.program_id(0)], tile_s, sem)


def _parity_onehot(idx_v):
    par = (idx_v & 1).astype(F32)
    return jnp.concatenate([1.0 - par, par], axis=1)


def _hi_lo_rows(a):
    hi = a.astype(BF16)
    lo = (a - hi.astype(F32)).astype(BF16)
    return jnp.concatenate([hi, lo], axis=0)


def _peer_u_kernel(tile_hbm, idx_v, x_ref, gate_ref, tab_ref, e_ref, g_ref, cm_ref, w_ref,
                   sel_ref, r_ref, tile_s, sem):
    tu = idx_v.shape[0]
    fetch = _tile_fetch(tile_hbm, tile_s, sem)
    fetch.start()
    sel_ref[...] = jnp.dot(_parity_onehot(idx_v[...]).astype(BF16), e_ref[...],
                           preferred_element_type=F32)
    fetch.wait()

    def token(t):
        m = _gather_tiles(tile_s, t, tab_ref)
        r = lax.dot_general(x_ref[t], m, NT_DIMS, preferred_element_type=F32)
        r8 = (r[0:8, :] + r[8:16, :]) * cm_ref[...]
        r_ref[pl.ds(t, 1), :] = jnp.sum(r8, axis=0, keepdims=True) * sel_ref[pl.ds(t, 1), :]

    def body(i, carry):
        for j in range(TOKENS_PER_ITER):
            token(i * TOKENS_PER_ITER + j)
        return carry

    lax.fori_loop(0, tu // TOKENS_PER_ITER, body, 0)
    r_all = r_ref[...]
    hi = r_all.astype(BF16)
    lo = (r_all - hi.astype(F32)).astype(BF16)
    a = (jnp.dot(hi, g_ref[...], preferred_element_type=F32)
         + jnp.dot(lo, g_ref[...], preferred_element_type=F32))
    w_ref[...] = gate_ref[...] * (0.5 * a * (1.0 + lax.erf(a * (2.0 ** -0.5))))


def _peer_v_kernel(tile_hbm, idx_v, w_ref, tab_ref, e_ref, cm_ref, o_ref, wexp_ref, tile_s, sem):
    tu = idx_v.shape[0]
    fetch = _tile_fetch(tile_hbm, tile_s, sem)
    fetch.start()
    w = w_ref[...]
    w2 = jnp.concatenate([w, w], axis=1) * _parity_onehot(idx_v[...])
    hi = w2.astype(BF16)
    lo = (w2 - hi.astype(F32)).astype(BF16)
    wexp_ref[...] = (jnp.dot(hi, e_ref[...], preferred_element_type=F32)
                     + jnp.dot(lo, e_ref[...], preferred_element_type=F32))
    fetch.wait()

    def token(t):
        m = _gather_tiles(tile_s, t, tab_ref)
        w8 = wexp_ref[pl.ds(t, 1), :] * cm_ref[...]
        o = jnp.dot(_hi_lo_rows(w8), m, preferred_element_type=F32)
        o_ref[t] = o[0:8, :] + o[8:16, :]

    def body(i, carry):
        for j in range(TOKENS_PER_ITER):
            token(i * TOKENS_PER_ITER + j)
        return carry

    lax.fori_loop(0, tu // TOKENS_PER_ITER, body, 0)


def _peer_consts():
    ncol = PEER_SLOTS * PAIR_ROWS
    col = jnp.arange(ncol)
    slot = col // PAIR_ROWS
    half = col % 2
    row = jnp.arange(2 * PEER_SLOTS)
    expand = ((row[:, None] % PEER_SLOTS == slot[None, :])
              & (row[:, None] // PEER_SLOTS == half[None, :])).astype(BF16)
    group = (slot[:, None] == jnp.arange(PEER_SLOTS)[None, :]).astype(BF16)
    chunk8 = (jnp.arange(8)[:, None] == (col % PAIR_ROWS // 2)[None, :]).astype(F32)
    return expand, group, chunk8


def _peer_table(w):
    n, d = w.shape
    bits = lax.bitcast_convert_type(w.astype(BF16), jnp.uint16).astype(jnp.uint32)
    words = bits[0::2] | (bits[1::2] << 16)
    return lax.bitcast_convert_type(words, jnp.int32).reshape(n // 2 * TILE_WORD_ROWS, LANES)


def _peer_specs(tu):
    smem = pl.BlockSpec(memory_space=pl.ANY)
    tok = pl.BlockSpec((tu, PEER_SLOTS), lambda i: (i, 0))
    full = lambda a: pl.BlockSpec(a.shape, lambda i: (0,) * a.ndim)
    resident = lambda a: pl.BlockSpec(a.shape, lambda i: (0,) * a.ndim,
                                      pipeline_mode=pl.Buffered(1))
    return smem, tok, full, resident


def _peer_u(tile, idx, x16, gate, tab):
    T = idx.shape[0]
    tu = PEER_TOKEN_TILE
    expand, group, chunk8 = _peer_consts()
    smem, tok, full, resident = _peer_specs(tu)
    ncol = PEER_SLOTS * PAIR_ROWS
    return pl.pallas_call(
        _peer_u_kernel,
        grid=(T // tu,),
        in_specs=[smem, tok, pl.BlockSpec((tu, 16, LANES), lambda i: (i, 0, 0)), tok,
                  resident(tab), full(expand), full(group), full(chunk8)],
        out_specs=tok,
        out_shape=jax.ShapeDtypeStruct((T, PEER_SLOTS), F32),
        scratch_shapes=[pltpu.VMEM((tu, ncol), F32), pltpu.VMEM((tu, ncol), F32),
                        pltpu.SMEM((PEER_SLOTS, tu), jnp.int32), pltpu.SemaphoreType.DMA(())],
        compiler_params=pltpu.CompilerParams(
            dimension_semantics=("parallel",), vmem_limit_bytes=VMEM_LIMIT),
    )(tile, idx, x16, gate, tab, expand, group, chunk8)


def _peer_v(tile, idx, w, tab):
    T = idx.shape[0]
    tu = PEER_TOKEN_TILE
    expand, _, chunk8 = _peer_consts()
    smem, tok, full, resident = _peer_specs(tu)
    ncol = PEER_SLOTS * PAIR_ROWS
    return pl.pallas_call(
        _peer_v_kernel,
        grid=(T // tu,),
        in_specs=[smem, tok, tok, resident(tab), full(expand), full(chunk8)],
        out_specs=pl.BlockSpec((tu, 8, LANES), lambda i: (i, 0, 0)),
        out_shape=jax.ShapeDtypeStruct((T, 8, LANES), F32),
        scratch_shapes=[pltpu.VMEM((tu, ncol), F32),
                        pltpu.SMEM((PEER_SLOTS, tu), jnp.int32), pltpu.SemaphoreType.DMA(())],
        compiler_params=pltpu.CompilerParams(
            dimension_semantics=("parallel",), vmem_limit_bytes=VMEM_LIMIT),
    )(tile, idx, w, tab, expand, chunk8)


def _final_kernel(x_ref, p_ref, g_ref, o_ref):
    o_ref[...] = _rms(x_ref[...] + p_ref[...], g_ref[...])


def _final(x1, peer_out, g, tm):
    T, D = x1.shape
    tm = min(tm, T)
    tok = pl.BlockSpec((tm, D), lambda i: (i, 0))
    return pl.pallas_call(
        _final_kernel,
        grid=(T // tm,),
        in_specs=[tok, tok, pl.BlockSpec((1, D), lambda i: (0, 0))],
        out_specs=tok,
        out_shape=jax.ShapeDtypeStruct((T, D), F32),
        compiler_params=pltpu.CompilerParams(dimension_semantics=("parallel",)),
    )(x1, peer_out, g.reshape(1, D))


def kernel(x, mem, norm_mix_g, w_in, hg_lb, hg_norm_g, sc_conv_w, mem_norm_g, w_mem_kv,
           w_branch, w_out, norm_ffn_g, peer_w_q, peer_sub_keys, peer_u, peer_v, final_norm_g):
    B, S, D = x.shape
    T = B * S
    assert w_in.shape[0] == 1, "single-layer block"
    n_main = 9 * BRANCH_WIDTH
    lb_table = jnp.cumsum(jax.nn.softmax(hg_lb.astype(F32), axis=0), axis=0)
    xf = x.reshape(T, D)
    memf = mem.reshape(B * N_MEM, D)
    w_in_b = w_in[0].astype(BF16)
    proj = _norm_matmul(xf, norm_mix_g[0], w_in_b[:, :n_main], 1024, 768)
    gates = _norm_matmul(xf, norm_mix_g[0], w_in_b[:, n_main:], 1024, 768)
    kv = _norm_matmul(memf, mem_norm_g[0], w_mem_kv[0].astype(BF16), 1024, 512)
    y_hg = _hgrn(proj, lb_table[0], hg_norm_g[0], B, S)
    x1 = _mix(xf, proj, gates, y_hg, kv, sc_conv_w[0], w_branch[0].astype(BF16),
              w_out[0].astype(BF16), B, S, 256)
    h_hi, h_lo, idx, tile, gate = _route(x1, norm_ffn_g[0], peer_w_q[0].astype(BF16),
                                         peer_sub_keys[0], 256)
    x16 = jnp.concatenate([h_hi.reshape(T, 8, LANES), h_lo.reshape(T, 8, LANES)], axis=1)
    w = _peer_u(tile, idx, x16, gate, _peer_table(peer_u[0]))
    peer_out = _peer_v(tile, idx, w, _peer_table(peer_v[0]))
    return _final(x1, peer_out.reshape(T, D), final_norm_g, 512).reshape(B, S, D)
```

```python
import functools

import jax
import jax.numpy as jnp
from jax import lax
from jax.experimental import pallas as pl
from jax.experimental.pallas import tpu as pltpu

F32 = jnp.float32
BF16 = jnp.bfloat16
HIGHEST = lax.Precision.HIGHEST

EPS = 1e-6
LANES = 128
HG_HEADS = 4
HG_CHUNK = 64
HG_CHUNKS_PER_ITER = 2
HG_STATE_UNROLL = 4
MX_HEADS = 4
BRANCH_WIDTH = 512
N_MEM = 256
PEER_HEADS = 8
PEER_NKEYS = 128
PEER_TOPK = 16
PEER_SLOTS = PEER_HEADS * PEER_TOPK
EXPERT_WORD_ROWS = 4
PEER_TOKEN_TILE = 128
TOKENS_PER_ITER = 16
VMEM_LIMIT = 48 * 1024 * 1024

NT_DIMS = (((1,), (1,)), ((), ()))
TN_DIMS = (((0,), (0,)), ((), ()))


def _rms(x, g):
    return x * lax.rsqrt(jnp.mean(x * x, axis=-1, keepdims=True) + EPS) * g


def _norm_matmul_kernel(x_ref, g_ref, w_ref, o_ref, h_ref):
    @pl.when(pl.program_id(1) == 0)
    def _():
        h_ref[...] = _rms(x_ref[...], g_ref[...]).astype(BF16)

    o_ref[...] = jnp.dot(h_ref[...], w_ref[...], preferred_element_type=F32)


def _norm_matmul(x, g, w_bf16, tm, tn):
    T, D = x.shape
    N = w_bf16.shape[1]
    tm = min(tm, T)
    return pl.pallas_call(
        _norm_matmul_kernel,
        grid=(T // tm, N // tn),
        in_specs=[
            pl.BlockSpec((tm, D), lambda i, j: (i, 0)),
            pl.BlockSpec((1, D), lambda i, j: (0, 0)),
            pl.BlockSpec((D, tn), lambda i, j: (0, j)),
        ],
        out_specs=pl.BlockSpec((tm, tn), lambda i, j: (i, j)),
        out_shape=jax.ShapeDtypeStruct((T, N), F32),
        scratch_shapes=[pltpu.VMEM((tm, D), BF16)],
        compiler_params=pltpu.CompilerParams(
            dimension_semantics=("parallel", "arbitrary"), vmem_limit_bytes=VMEM_LIMIT),
    )(x, g.reshape(1, D), w_bf16)


def _hgrn_kernel(q_ref, v_ref, zf_ref, zb_ref, g_ref, lb_ref, ng_ref, o_ref,
                 st_ref, out_ref, qs_ref, kk_ref, cum_ref, sc_ref, qe_ref, dst_ref, dec_ref):
    S = q_ref.shape[0]
    C = HG_CHUNK
    n = S // C
    row = lax.broadcasted_iota(jnp.int32, (C, C), 0)
    col = lax.broadcasted_iota(jnp.int32, (C, C), 1)
    keeps = (col <= row, col >= row)
    z_refs = (zf_ref, zb_ref)
    unroll = HG_CHUNKS_PER_ITER

    def chunks(fn):
        def body(i, carry):
            for j in range(unroll):
                sl = pl.ds(pl.multiple_of((i * unroll + j) * C, C), C)
                for d in range(2):
                    fn(i * unroll + j, sl, d)
            return carry
        lax.fori_loop(0, n // unroll, body, 0)

    qs_ref[...] = jax.nn.silu(q_ref[...])

    def decay_pass(c, sl, d):
        lb = lb_ref[d:d + 1, :]
        z = z_refs[d][sl, :]
        lf = jnp.log(lb + (1.0 - lb) * jax.nn.sigmoid(z))
        kk_ref[d, sl, :] = (1.0 - lb) * jax.nn.sigmoid(-z)
        tri = keeps[d].astype(BF16)
        hi = lf.astype(BF16)
        r1 = lf - hi.astype(F32)
        mid = r1.astype(BF16)
        lo = (r1 - mid.astype(F32)).astype(BF16)
        cum_ref[d, sl, :] = (jnp.dot(tri, hi, preferred_element_type=F32)
                             + jnp.dot(tri, mid, preferred_element_type=F32)
                             + jnp.dot(tri, lo, preferred_element_type=F32))

    def score_pass(c, sl, d):
        G = cum_ref[d, sl, :]
        kk = kk_ref[d, sl, :]
        q = qs_ref[sl, :]
        mid = C // 2 if d else C // 2 - 1
        last = 0 if d else C - 1
        g_mid = G[mid:mid + 1, :]
        g_last = G[last:last + 1, :]
        q_r = q * jnp.exp(G - g_mid)
        k_r = kk * jnp.exp(g_mid - G)
        scores = lax.dot_general(q_r, k_r, NT_DIMS, preferred_element_type=F32)
        sc_ref[d, sl, :] = jnp.where(keeps[d], scores, 0.0)
        qe_ref[d, sl, :] = q * jnp.exp(G)
        dst_ref[d, c] = lax.dot_general(v_ref[sl, :], kk * jnp.exp(g_last - G), TN_DIMS,
                                        preferred_element_type=F32)
        dec_ref[d, c] = jnp.broadcast_to(jnp.exp(g_last), (8, LANES))

    def intra_pass(c, sl, d):
        out_ref[d, sl, :] = jnp.dot(sc_ref[d, sl, :], v_ref[sl, :], preferred_element_type=F32)

    chunks(decay_pass)
    chunks(score_pass)
    chunks(intra_pass)
    st_ref[...] = jnp.zeros_like(st_ref)

    def state_body(i, carry):
        for j in range(HG_STATE_UNROLL):
            step = i * HG_STATE_UNROLL + j
            for d, c in ((0, step), (1, n - 1 - step)):
                sl = pl.ds(pl.multiple_of(c * C, C), C)
                st = st_ref[d]
                out_ref[d, sl, :] += lax.dot_general(qe_ref[d, sl, :], st, NT_DIMS,
                                                     preferred_element_type=F32)
                st_ref[d] = st * dec_ref[d, c][0:1, :] + dst_ref[d, c]
        return carry

    lax.fori_loop(0, n // HG_STATE_UNROLL, state_body, 0)
    o = out_ref[0] + out_ref[1]
    o = o * lax.rsqrt(jnp.mean(o * o, axis=-1, keepdims=True) + EPS)
    o_ref[...] = o * ng_ref[...] * jax.nn.silu(g_ref[...])


def _hgrn(proj, lb, ng, B, S):
    T = B * S
    H = HG_HEADS
    sec = lambda s: pl.BlockSpec((S, LANES), lambda b, h, s=s: (b, s * H + h))
    return pl.pallas_call(
        _hgrn_kernel,
        grid=(B, H),
        in_specs=[sec(0), sec(1), sec(2), sec(3), sec(4),
                  pl.BlockSpec((2, LANES), lambda b, h: (0, h)),
                  pl.BlockSpec((1, LANES), lambda b, h: (0, h))],
        out_specs=pl.BlockSpec((S, LANES), lambda b, h: (b, h)),
        out_shape=jax.ShapeDtypeStruct((T, H * LANES), F32),
        scratch_shapes=[pltpu.VMEM((2, LANES, LANES), F32),
                        pltpu.VMEM((2, S, LANES), F32),
                        pltpu.VMEM((S, LANES), F32),
                        pltpu.VMEM((2, S, LANES), F32),
                        pltpu.VMEM((2, S, LANES), F32),
                        pltpu.VMEM((2, S, HG_CHUNK), F32),
                        pltpu.VMEM((2, S, LANES), F32),
                        pltpu.VMEM((2, S // HG_CHUNK, LANES, LANES), F32),
                        pltpu.VMEM((2, S // HG_CHUNK, 8, LANES), F32)],
        compiler_params=pltpu.CompilerParams(
            dimension_semantics=("parallel", "parallel"), vmem_limit_bytes=VMEM_LIMIT),
    )(proj, proj, proj, proj, proj, lb, ng.reshape(1, -1))


def _mix_kernel(x_ref, sb_ref, sc_ref, sh_ref, scp_ref, shp_ref, scn_ref, shn_ref, mq_ref,
                gates_ref, yhg_ref, kv_ref, cw_ref, wb_ref, wo_ref, o_ref):
    i = pl.program_id(1)
    ni = pl.num_programs(1)
    ts = x_ref.shape[0]
    W = BRANCH_WIDTH
    u = sc_ref[...] * sh_ref[...]
    u_before = jnp.where(i > 0, scp_ref[7:8, :] * shp_ref[7:8, :], 0.0)
    u_after = jnp.where(i < ni - 1, scn_ref[0:1, :] * shn_ref[0:1, :], 0.0)
    rows = lax.broadcasted_iota(jnp.int32, (ts, W), 0)
    u_prev = jnp.where(rows == 0, u_before, pltpu.roll(u, 1, 0))
    u_next = jnp.where(rows == ts - 1, u_after, pltpu.roll(u, ts - 1, 0))
    cw = cw_ref[...]
    y_sc = sb_ref[...] * (cw[0:1, :] * u_prev + cw[1:2, :] * u + cw[2:3, :] * u_next)

    heads = []
    for h in range(MX_HEADS):
        qh = mq_ref[:, h * LANES:(h + 1) * LANES]
        kh = kv_ref[:, h * LANES:(h + 1) * LANES]
        vh = kv_ref[:, W + h * LANES:W + (h + 1) * LANES]
        logits = lax.dot_general(qh, kh, NT_DIMS, preferred_element_type=F32) * (LANES ** -0.5)
        e = jnp.exp(logits - jnp.max(logits, axis=-1, keepdims=True))
        p = e / jnp.sum(e, axis=-1, keepdims=True)
        heads.append(jnp.dot(p, vh, preferred_element_type=F32))
    y_mx = jnp.concatenate(heads, axis=1)

    D = x_ref.shape[1]
    merged = jnp.zeros((ts, D), F32)
    for n, y in enumerate((yhg_ref[...], y_sc, y_mx)):
        pr = jnp.dot(y.astype(BF16), wb_ref[n], preferred_element_type=F32)
        merged = merged + jax.nn.sigmoid(gates_ref[:, n * D:(n + 1) * D]) * pr
    out = jnp.dot(merged.astype(BF16), wo_ref[...], preferred_element_type=F32)
    o_ref[...] = x_ref[...] + out


def _mix(x, proj, gates, y_hg, kv, conv_w, wb_bf16, wo_bf16, B, S, ts):
    T, D = x.shape
    W = BRANCH_WIDTH
    ts = min(ts, S)
    nt = S // ts
    r8 = ts // 8
    tile = lambda b, i: b * nt + i
    sec = lambda s: pl.BlockSpec((ts, W), lambda b, i, s=s: (tile(b, i), s))
    before = lambda s: pl.BlockSpec(
        (8, W), lambda b, i, s=s: (jnp.maximum(tile(b, i) * r8 - 1, 0), s))
    after = lambda s: pl.BlockSpec(
        (8, W), lambda b, i, s=s: (jnp.minimum((tile(b, i) + 1) * r8, T // 8 - 1), s))
    return pl.pallas_call(
        _mix_kernel,
        grid=(B, nt),
        in_specs=[
            pl.BlockSpec((ts, D), lambda b, i: (tile(b, i), 0)),
            sec(5), sec(6), sec(7), before(6), before(7), after(6), after(7), sec(8),
            pl.BlockSpec((ts, 3 * D), lambda b, i: (tile(b, i), 0)),
            pl.BlockSpec((ts, W), lambda b, i: (tile(b, i), 0)),
            pl.BlockSpec((N_MEM, 2 * W), lambda b, i: (b, 0)),
            pl.BlockSpec((3, W), lambda b, i: (0, 0)),
            pl.BlockSpec((3, W, D), lambda b, i: (0, 0, 0)),
            pl.BlockSpec((D, D), lambda b, i: (0, 0)),
        ],
        out_specs=pl.BlockSpec((ts, D), lambda b, i: (tile(b, i), 0)),
        out_shape=jax.ShapeDtypeStruct((T, D), F32),
        compiler_params=pltpu.CompilerParams(
            dimension_semantics=("parallel", "parallel"), vmem_limit_bytes=VMEM_LIMIT),
    )(x, proj, proj, proj, proj, proj, proj, proj, proj, gates, y_hg, kv, conv_w, wb_bf16, wo_bf16)


def _topk_rows(s, k, payload=None):
    n = s.shape[0]
    iota = lax.broadcasted_iota(jnp.int32, s.shape, 0)
    vals, ids = [], []
    for _ in range(k):
        m = jnp.max(s, axis=0, keepdims=True)
        am = jnp.min(jnp.where(s == m, iota, n), axis=0, keepdims=True)
        hit = iota == am
        vals.append(m)
        if payload is None:
            ids.append(am)
        else:
            ids.append(jnp.sum(jnp.where(hit, payload, 0), axis=0, keepdims=True))
        s = jnp.where(hit, -jnp.inf, s)
    return jnp.concatenate(vals, axis=0), jnp.concatenate(ids, axis=0)


def _pair_candidates(top_s, top_i):
    K = PEER_TOPK
    assert K == 16
    (s0, s1), (i0, i1) = top_s, top_i
    blocks_s = [s0[0:1, :] + s1]
    blocks_i = [i0[0:1, :] * PEER_NKEYS + i1]
    b_row = lax.broadcasted_iota(jnp.int32, (8, s0.shape[1]), 0)
    for a in range(1, 8):
        blocks_s.append(jnp.where(b_row < K // (a + 1), s0[a:a + 1, :] + s1[0:8, :], -jnp.inf))
        blocks_i.append(i0[a:a + 1, :] * PEER_NKEYS + i1[0:8, :])
    blocks_s.append(s0[8:16, :] + s1[0:1, :])
    blocks_i.append(i0[8:16, :] * PEER_NKEYS + i1[0:1, :])
    return jnp.concatenate(blocks_s, axis=0), jnp.concatenate(blocks_i, axis=0)


def _route_kernel(x_ref, g_ref, wq_ref, keys_ref, hhi_ref, hlo_ref, tile_ref, gate_ref):
    K = PEER_TOPK
    h = _rms(x_ref[...], g_ref[...])
    h_hi = h.astype(BF16)
    hhi_ref[...] = h_hi
    hlo_ref[...] = (h - h_hi.astype(F32)).astype(BF16)
    q = jnp.dot(h_hi, wq_ref[...], preferred_element_type=F32)
    idx_rows, gate_rows = [], []
    for hd in range(PEER_HEADS):
        top_s, top_i = [], []
        for p in range(2):
            c0 = (hd * 2 + p) * LANES
            s_t = lax.dot_general(keys_ref[hd, p], q[:, c0:c0 + LANES], NT_DIMS,
                                  preferred_element_type=F32)
            vs, is_ = _topk_rows(s_t, K)
            top_s.append(vs)
            top_i.append(is_)
        cand_s, cand_i = _pair_candidates(top_s, top_i)
        best_s, best_i = _topk_rows(cand_s, K, payload=cand_i)
        e = jnp.exp(best_s - best_s[0:1, :])
        gate_rows.append(e / jnp.sum(e, axis=0, keepdims=True))
        idx_rows.append(best_i)
    tile_t = jnp.concatenate(idx_rows, axis=0) * EXPERT_WORD_ROWS
    for j in range(tile_ref.shape[0]):
        tile_ref[j] = tile_t[:, j * PEER_TOKEN_TILE:(j + 1) * PEER_TOKEN_TILE]
    gate_ref[...] = jnp.concatenate(gate_rows, axis=0).T


def _route(x1, g, wq_bf16, keys, tt):
    T, D = x1.shape
    tt = min(tt, T)
    TU = PEER_TOKEN_TILE
    NQ = wq_bf16.shape[1]
    tok = lambda w: pl.BlockSpec((tt, w), lambda i: (i, 0))
    return pl.pallas_call(
        _route_kernel,
        grid=(T // tt,),
        in_specs=[tok(D), pl.BlockSpec((1, D), lambda i: (0, 0)),
                  pl.BlockSpec((D, NQ), lambda i: (0, 0)),
                  pl.BlockSpec(keys.shape, lambda i: (0, 0, 0, 0))],
        out_specs=[tok(D), tok(D),
                   pl.BlockSpec((tt // TU, PEER_SLOTS, TU), lambda i: (i, 0, 0)),
                   tok(PEER_SLOTS)],
        out_shape=[jax.ShapeDtypeStruct((T, D), BF16), jax.ShapeDtypeStruct((T, D), BF16),
                   jax.ShapeDtypeStruct((T // TU, PEER_SLOTS, TU), jnp.int32),
                   jax.ShapeDtypeStruct((T, PEER_SLOTS), F32)],
        compiler_params=pltpu.CompilerParams(
            dimension_semantics=("parallel",), vmem_limit_bytes=VMEM_LIMIT),
    )(x1, g.reshape(1, D), wq_bf16, keys)


def _gather_tiles(tile_s, t, tab_ref):
    def words(k):
        r = pl.multiple_of(tile_s.at[k][t], EXPERT_WORD_ROWS)
        return tab_ref[pl.ds(r, EXPERT_WORD_ROWS), :]

    return jnp.concatenate(
        [pltpu.bitcast(jnp.concatenate([words(k), words(k + 1)], axis=0), BF16)
         for k in range(0, PEER_SLOTS, 2)], axis=0)


def _tile_fetch(tile_hbm, tile_s, sem):
    return pltpu.make_async_copy(tile_hbm.at[pl.program_id(0)], tile_s, sem)


def _hi_lo(a):
    hi = a.astype(BF16)
    return hi, (a - hi.astype(F32)).astype(BF16)


def _token_loop(tu, token):
    def body(i, carry):
        for j in range(TOKENS_PER_ITER):
            token(i * TOKENS_PER_ITER + j)
        return carry

    lax.fori_loop(0, tu // TOKENS_PER_ITER, body, 0)


def _peer_u_kernel(tile_hbm, x_ref, gate_ref, tab_ref, g_ref, cm_ref, w_ref, r_ref, tile_s, sem):
    tu = gate_ref.shape[0]
    fetch = _tile_fetch(tile_hbm, tile_s, sem)
    fetch.start()
    fetch.wait()

    def token(t):
        m = _gather_tiles(tile_s, t, tab_ref)
        r = lax.dot_general(x_ref[t], m, NT_DIMS, preferred_element_type=F32)
        r8 = (r[0:8, :] + r[8:16, :]) * cm_ref[...]
        r_ref[pl.ds(t, 1), :] = jnp.sum(r8, axis=0, keepdims=True)

    _token_loop(tu, token)
    hi, lo = _hi_lo(r_ref[...])
    a = (jnp.dot(hi, g_ref[...], preferred_element_type=F32)
         + jnp.dot(lo, g_ref[...], preferred_element_type=F32))
    w_ref[...] = gate_ref[...] * (0.5 * a * (1.0 + lax.erf(a * (2.0 ** -0.5))))


def _peer_v_kernel(tile_hbm, w_ref, tab_ref, e_ref, cm_ref, o_ref, wexp_ref, tile_s, sem):
    tu = w_ref.shape[0]
    fetch = _tile_fetch(tile_hbm, tile_s, sem)
    fetch.start()
    hi, lo = _hi_lo(w_ref[...])
    wexp_ref[...] = (jnp.dot(hi, e_ref[...], preferred_element_type=F32)
                     + jnp.dot(lo, e_ref[...], preferred_element_type=F32))
    fetch.wait()

    def token(t):
        m = _gather_tiles(tile_s, t, tab_ref)
        hi8, lo8 = _hi_lo(wexp_ref[pl.ds(t, 1), :] * cm_ref[...])
        o = jnp.dot(jnp.concatenate([hi8, lo8], axis=0), m, preferred_element_type=F32)
        o_ref[t] = o[0:8, :] + o[8:16, :]

    _token_loop(tu, token)


def _peer_consts():
    row = jnp.arange(PEER_SLOTS * 8)
    slot = row // 8
    chunk = row % 8 // 2 + 4 * (row % 2)
    expand = (jnp.arange(PEER_SLOTS)[:, None] == slot[None, :]).astype(BF16)
    chunk8 = (jnp.arange(8)[:, None] == chunk[None, :]).astype(F32)
    return expand, expand.T, chunk8


def _peer_table(w):
    n, d = w.shape
    bits = lax.bitcast_convert_type(w.astype(BF16), jnp.uint16).astype(jnp.uint32)
    half_weight = jnp.array([1, 1 << 16], jnp.uint32)[None, :, None]
    words = jnp.sum(bits.reshape(n, 2, d // 2) * half_weight, axis=1, dtype=jnp.uint32)
    return lax.bitcast_convert_type(words, jnp.int32).reshape(n * EXPERT_WORD_ROWS, LANES)


def _peer_specs(tu):
    smem = pl.BlockSpec(memory_space=pl.ANY)
    tok = pl.BlockSpec((tu, PEER_SLOTS), lambda i: (i, 0))
    full = lambda a: pl.BlockSpec(a.shape, lambda i: (0,) * a.ndim)
    resident = lambda a: pl.BlockSpec(a.shape, lambda i: (0,) * a.ndim,
                                      pipeline_mode=pl.Buffered(1))
    return smem, tok, full, resident


def _peer_u(tile, x16, gate, tab):
    T = gate.shape[0]
    tu = PEER_TOKEN_TILE
    _, group, chunk8 = _peer_consts()
    smem, tok, full, resident = _peer_specs(tu)
    return pl.pallas_call(
        _peer_u_kernel,
        grid=(T // tu,),
        in_specs=[smem, pl.BlockSpec((tu, 16, LANES), lambda i: (i, 0, 0)), tok,
                  resident(tab), full(group), full(chunk8)],
        out_specs=tok,
        out_shape=jax.ShapeDtypeStruct((T, PEER_SLOTS), F32),
        scratch_shapes=[pltpu.VMEM((tu, PEER_SLOTS * 8), F32),
                        pltpu.SMEM((PEER_SLOTS, tu), jnp.int32), pltpu.SemaphoreType.DMA(())],
        compiler_params=pltpu.CompilerParams(
            dimension_semantics=("parallel",), vmem_limit_bytes=VMEM_LIMIT),
    )(tile, x16, gate, tab, group, chunk8)


def _peer_v(tile, w, tab):
    T = w.shape[0]
    tu = PEER_TOKEN_TILE
    expand, _, chunk8 = _peer_consts()
    smem, tok, full, resident = _peer_specs(tu)
    return pl.pallas_call(
        _peer_v_kernel,
        grid=(T // tu,),
        in_specs=[smem, tok, resident(tab), full(expand), full(chunk8)],
        out_specs=pl.BlockSpec((tu, 8, LANES), lambda i: (i, 0, 0)),
        out_shape=jax.ShapeDtypeStruct((T, 8, LANES), F32),
        scratch_shapes=[pltpu.VMEM((tu, PEER_SLOTS * 8), F32),
                        pltpu.SMEM((PEER_SLOTS, tu), jnp.int32), pltpu.SemaphoreType.DMA(())],
        compiler_params=pltpu.CompilerParams(
            dimension_semantics=("parallel",), vmem_limit_bytes=VMEM_LIMIT),
    )(tile, w, tab, expand, chunk8)


def _final_kernel(x_ref, p_ref, g_ref, o_ref):
    o_ref[...] = _rms(x_ref[...] + p_ref[...], g_ref[...])


def _final(x1, peer_out, g, tm):
    T, D = x1.shape
    tm = min(tm, T)
    tok = pl.BlockSpec((tm, D), lambda i: (i, 0))
    return pl.pallas_call(
        _final_kernel,
        grid=(T // tm,),
        in_specs=[tok, tok, pl.BlockSpec((1, D), lambda i: (0, 0))],
        out_specs=tok,
        out_shape=jax.ShapeDtypeStruct((T, D), F32),
        compiler_params=pltpu.CompilerParams(dimension_semantics=("parallel",)),
    )(x1, peer_out, g.reshape(1, D))


def kernel(x, mem, norm_mix_g, w_in, hg_lb, hg_norm_g, sc_conv_w, mem_norm_g, w_mem_kv,
           w_branch, w_out, norm_ffn_g, peer_w_q, peer_sub_keys, peer_u, peer_v, final_norm_g):
    B, S, D = x.shape
    T = B * S
    assert w_in.shape[0] == 1, "single-layer block"
    n_main = 9 * BRANCH_WIDTH
    lb_table = jnp.cumsum(jax.nn.softmax(hg_lb.astype(F32), axis=0), axis=0)
    xf = x.reshape(T, D)
    memf = mem.reshape(B * N_MEM, D)
    w_in_b = w_in[0].astype(BF16)
    proj = _norm_matmul(xf, norm_mix_g[0], w_in_b[:, :n_main], 1024, 768)
    gates = _norm_matmul(xf, norm_mix_g[0], w_in_b[:, n_main:], 1024, 768)
    kv = _norm_matmul(memf, mem_norm_g[0], w_mem_kv[0].astype(BF16), 1024, 512)
    y_hg = _hgrn(proj, lb_table[0], hg_norm_g[0], B, S)
    x1 = _mix(xf, proj, gates, y_hg, kv, sc_conv_w[0], w_branch[0].astype(BF16),
              w_out[0].astype(BF16), B, S, 256)
    h_hi, h_lo, tile, gate = _route(x1, norm_ffn_g[0], peer_w_q[0].astype(BF16),
                                         peer_sub_keys[0], 256)
    x16 = jnp.concatenate([h_hi.reshape(T, 8, LANES), h_lo.reshape(T, 8, LANES)], axis=1)
    w = _peer_u(tile, x16, gate, _peer_table(peer_u[0]))
    peer_out = _peer_v(tile, w, _peer_table(peer_v[0]))
    return _final(x1, peer_out.reshape(T, D), final_norm_g, 512).reshape(B, S, D)
```

```python
import jax
import jax.numpy as jnp
from jax import lax
from jax.experimental import pallas as pl
from jax.experimental.pallas import tpu as pltpu

F32 = jnp.float32
BF16 = jnp.bfloat16

EPS = 1e-6
LANES = 128
HG_HEADS = 4
HG_CHUNK = 64
HG_CHUNKS_PER_ITER = 2
HG_STATE_UNROLL = 4
MX_HEADS = 4
BRANCH_WIDTH = 512
N_MEM = 256
PEER_HEADS = 8
PEER_NKEYS = 128
PEER_TOPK = 16
PEER_SLOTS = PEER_HEADS * PEER_TOPK
EXPERT_WORD_ROWS = 4
PACK_ROWS = 512
PEER_TOKEN_TILE = 128
TOKENS_PER_ITER = 16
VMEM_LIMIT = 48 * 1024 * 1024

NT_DIMS = (((1,), (1,)), ((), ()))
TN_DIMS = (((0,), (0,)), ((), ()))


def _rms(x, g):
    return x * lax.rsqrt(jnp.mean(x * x, axis=-1, keepdims=True) + EPS) * g


def _norm_matmul_kernel(x_ref, g_ref, w_ref, o_ref, h_ref):
    @pl.when(pl.program_id(1) == 0)
    def _():
        h_ref[...] = _rms(x_ref[...], g_ref[...]).astype(BF16)

    o_ref[...] = jnp.dot(h_ref[...], w_ref[...], preferred_element_type=F32)


def _norm_matmul(x, g, w_bf16, tm, tn):
    T, D = x.shape
    N = w_bf16.shape[1]
    tm = min(tm, T)
    return pl.pallas_call(
        _norm_matmul_kernel,
        grid=(T // tm, N // tn),
        in_specs=[
            pl.BlockSpec((tm, D), lambda i, j: (i, 0)),
            pl.BlockSpec((1, D), lambda i, j: (0, 0)),
            pl.BlockSpec((D, tn), lambda i, j: (0, j)),
        ],
        out_specs=pl.BlockSpec((tm, tn), lambda i, j: (i, j)),
        out_shape=jax.ShapeDtypeStruct((T, N), F32),
        scratch_shapes=[pltpu.VMEM((tm, D), BF16)],
        compiler_params=pltpu.CompilerParams(
            dimension_semantics=("parallel", "arbitrary"), vmem_limit_bytes=VMEM_LIMIT),
    )(x, g.reshape(1, D), w_bf16)


def _hgrn_kernel(q_ref, v_ref, zf_ref, zb_ref, g_ref, lb_ref, ng_ref, o_ref,
                 st_ref, out_ref, qs_ref, kk_ref, cum_ref, sc_ref, qe_ref, dst_ref, dec_ref):
    S = q_ref.shape[0]
    C = HG_CHUNK
    n = S // C
    row = lax.broadcasted_iota(jnp.int32, (C, C), 0)
    col = lax.broadcasted_iota(jnp.int32, (C, C), 1)
    keeps = (col <= row, col >= row)
    z_refs = (zf_ref, zb_ref)
    unroll = HG_CHUNKS_PER_ITER

    def chunks(fn):
        def body(i, carry):
            for j in range(unroll):
                sl = pl.ds(pl.multiple_of((i * unroll + j) * C, C), C)
                for d in range(2):
                    fn(i * unroll + j, sl, d)
            return carry
        lax.fori_loop(0, n // unroll, body, 0)

    qs_ref[...] = jax.nn.silu(q_ref[...])

    def decay_pass(c, sl, d):
        lb = lb_ref[d:d + 1, :]
        z = z_refs[d][sl, :]
        lf = jnp.log(lb + (1.0 - lb) * jax.nn.sigmoid(z))
        kk_ref[d, sl, :] = (1.0 - lb) * jax.nn.sigmoid(-z)
        tri = keeps[d].astype(BF16)
        hi = lf.astype(BF16)
        r1 = lf - hi.astype(F32)
        mid = r1.astype(BF16)
        lo = (r1 - mid.astype(F32)).astype(BF16)
        cum_ref[d, sl, :] = (jnp.dot(tri, hi, preferred_element_type=F32)
                             + jnp.dot(tri, mid, preferred_element_type=F32)
                             + jnp.dot(tri, lo, preferred_element_type=F32))

    def score_pass(c, sl, d):
        G = cum_ref[d, sl, :]
        kk = kk_ref[d, sl, :]
        q = qs_ref[sl, :]
        mid = C // 2 if d else C // 2 - 1
        last = 0 if d else C - 1
        g_mid = G[mid:mid + 1, :]
        g_last = G[last:last + 1, :]
        q_r = q * jnp.exp(G - g_mid)
        k_r = kk * jnp.exp(g_mid - G)
        scores = lax.dot_general(q_r, k_r, NT_DIMS, preferred_element_type=F32)
        sc_ref[d, sl, :] = jnp.where(keeps[d], scores, 0.0)
        qe_ref[d, sl, :] = q * jnp.exp(G)
        dst_ref[d, c] = lax.dot_general(v_ref[sl, :], kk * jnp.exp(g_last - G), TN_DIMS,
                                        preferred_element_type=F32)
        dec_ref[d, c] = jnp.broadcast_to(jnp.exp(g_last), (8, LANES))

    def intra_pass(c, sl, d):
        out_ref[d, sl, :] = jnp.dot(sc_ref[d, sl, :], v_ref[sl, :], preferred_element_type=F32)

    chunks(decay_pass)
    chunks(score_pass)
    chunks(intra_pass)
    st_ref[...] = jnp.zeros_like(st_ref)

    def state_body(i, carry):
        for j in range(HG_STATE_UNROLL):
            step = i * HG_STATE_UNROLL + j
            for d, c in ((0, step), (1, n - 1 - step)):
                sl = pl.ds(pl.multiple_of(c * C, C), C)
                st = st_ref[d]
                out_ref[d, sl, :] += lax.dot_general(qe_ref[d, sl, :], st, NT_DIMS,
                                                     preferred_element_type=F32)
                st_ref[d] = st * dec_ref[d, c][0:1, :] + dst_ref[d, c]
        return carry

    lax.fori_loop(0, n // HG_STATE_UNROLL, state_body, 0)
    o = out_ref[0] + out_ref[1]
    o = o * lax.rsqrt(jnp.mean(o * o, axis=-1, keepdims=True) + EPS)
    o_ref[...] = o * ng_ref[...] * jax.nn.silu(g_ref[...])


def _hgrn(proj, lb, ng, B, S):
    T = B * S
    H = HG_HEADS
    sec = lambda s: pl.BlockSpec((S, LANES), lambda b, h, s=s: (b, s * H + h))
    return pl.pallas_call(
        _hgrn_kernel,
        grid=(B, H),
        in_specs=[sec(0), sec(1), sec(2), sec(3), sec(4),
                  pl.BlockSpec((2, LANES), lambda b, h: (0, h)),
                  pl.BlockSpec((1, LANES), lambda b, h: (0, h))],
        out_specs=pl.BlockSpec((S, LANES), lambda b, h: (b, h)),
        out_shape=jax.ShapeDtypeStruct((T, H * LANES), F32),
        scratch_shapes=[pltpu.VMEM((2, LANES, LANES), F32),
                        pltpu.VMEM((2, S, LANES), F32),
                        pltpu.VMEM((S, LANES), F32),
                        pltpu.VMEM((2, S, LANES), F32),
                        pltpu.VMEM((2, S, LANES), F32),
                        pltpu.VMEM((2, S, HG_CHUNK), F32),
                        pltpu.VMEM((2, S, LANES), F32),
                        pltpu.VMEM((2, S // HG_CHUNK, LANES, LANES), F32),
                        pltpu.VMEM((2, S // HG_CHUNK, 8, LANES), F32)],
        compiler_params=pltpu.CompilerParams(
            dimension_semantics=("parallel", "parallel"), vmem_limit_bytes=VMEM_LIMIT),
    )(proj, proj, proj, proj, proj, lb, ng.reshape(1, -1))


def _mix_kernel(x_ref, sb_ref, sc_ref, sh_ref, scp_ref, shp_ref, scn_ref, shn_ref, mq_ref,
                gates_ref, yhg_ref, kv_ref, cw_ref, wb_ref, wo_ref, o_ref):
    i = pl.program_id(1)
    ni = pl.num_programs(1)
    ts = x_ref.shape[0]
    W = BRANCH_WIDTH
    u = sc_ref[...] * sh_ref[...]
    u_before = jnp.where(i > 0, scp_ref[7:8, :] * shp_ref[7:8, :], 0.0)
    u_after = jnp.where(i < ni - 1, scn_ref[0:1, :] * shn_ref[0:1, :], 0.0)
    rows = lax.broadcasted_iota(jnp.int32, (ts, W), 0)
    u_prev = jnp.where(rows == 0, u_before, pltpu.roll(u, 1, 0))
    u_next = jnp.where(rows == ts - 1, u_after, pltpu.roll(u, ts - 1, 0))
    cw = cw_ref[...]
    y_sc = sb_ref[...] * (cw[0:1, :] * u_prev + cw[1:2, :] * u + cw[2:3, :] * u_next)

    heads = []
    for h in range(MX_HEADS):
        qh = mq_ref[:, h * LANES:(h + 1) * LANES]
        kh = kv_ref[:, h * LANES:(h + 1) * LANES]
        vh = kv_ref[:, W + h * LANES:W + (h + 1) * LANES]
        logits = lax.dot_general(qh, kh, NT_DIMS, preferred_element_type=F32) * (LANES ** -0.5)
        e = jnp.exp(logits - jnp.max(logits, axis=-1, keepdims=True))
        p = e / jnp.sum(e, axis=-1, keepdims=True)
        heads.append(jnp.dot(p, vh, preferred_element_type=F32))
    y_mx = jnp.concatenate(heads, axis=1)

    D = x_ref.shape[1]
    merged = jnp.zeros((ts, D), F32)
    for n, y in enumerate((yhg_ref[...], y_sc, y_mx)):
        pr = jnp.dot(y.astype(BF16), wb_ref[n], preferred_element_type=F32)
        merged = merged + jax.nn.sigmoid(gates_ref[:, n * D:(n + 1) * D]) * pr
    out = jnp.dot(merged.astype(BF16), wo_ref[...], preferred_element_type=F32)
    o_ref[...] = x_ref[...] + out


def _mix(x, proj, gates, y_hg, kv, conv_w, wb_bf16, wo_bf16, B, S, ts):
    T, D = x.shape
    W = BRANCH_WIDTH
    ts = min(ts, S)
    nt = S // ts
    r8 = ts // 8
    tile = lambda b, i: b * nt + i
    sec = lambda s: pl.BlockSpec((ts, W), lambda b, i, s=s: (tile(b, i), s))
    before = lambda s: pl.BlockSpec(
        (8, W), lambda b, i, s=s: (jnp.maximum(tile(b, i) * r8 - 1, 0), s))
    after = lambda s: pl.BlockSpec(
        (8, W), lambda b, i, s=s: (jnp.minimum((tile(b, i) + 1) * r8, T // 8 - 1), s))
    return pl.pallas_call(
        _mix_kernel,
        grid=(B, nt),
        in_specs=[
            pl.BlockSpec((ts, D), lambda b, i: (tile(b, i), 0)),
            sec(5), sec(6), sec(7), before(6), before(7), after(6), after(7), sec(8),
            pl.BlockSpec((ts, 3 * D), lambda b, i: (tile(b, i), 0)),
            pl.BlockSpec((ts, W), lambda b, i: (tile(b, i), 0)),
            pl.BlockSpec((N_MEM, 2 * W), lambda b, i: (b, 0)),
            pl.BlockSpec((3, W), lambda b, i: (0, 0)),
            pl.BlockSpec((3, W, D), lambda b, i: (0, 0, 0)),
            pl.BlockSpec((D, D), lambda b, i: (0, 0)),
        ],
        out_specs=pl.BlockSpec((ts, D), lambda b, i: (tile(b, i), 0)),
        out_shape=jax.ShapeDtypeStruct((T, D), F32),
        compiler_params=pltpu.CompilerParams(
            dimension_semantics=("parallel", "parallel"), vmem_limit_bytes=VMEM_LIMIT),
    )(x, proj, proj, proj, proj, proj, proj, proj, proj, gates, y_hg, kv, conv_w, wb_bf16, wo_bf16)


def _topk_rows(s, k, payload=None):
    n = s.shape[0]
    iota = lax.broadcasted_iota(jnp.int32, s.shape, 0)
    vals, ids = [], []
    for _ in range(k):
        m = jnp.max(s, axis=0, keepdims=True)
        am = jnp.min(jnp.where(s == m, iota, n), axis=0, keepdims=True)
        hit = iota == am
        vals.append(m)
        if payload is None:
            ids.append(am)
        else:
            ids.append(jnp.sum(jnp.where(hit, payload, 0), axis=0, keepdims=True))
        s = jnp.where(hit, -jnp.inf, s)
    return jnp.concatenate(vals, axis=0), jnp.concatenate(ids, axis=0)


def _pair_candidates(top_s, top_i):
    K = PEER_TOPK
    assert K == 16
    (s0, s1), (i0, i1) = top_s, top_i
    blocks_s = [s0[0:1, :] + s1]
    blocks_i = [i0[0:1, :] * PEER_NKEYS + i1]
    b_row = lax.broadcasted_iota(jnp.int32, (8, s0.shape[1]), 0)
    for a in range(1, 8):
        blocks_s.append(jnp.where(b_row < K // (a + 1), s0[a:a + 1, :] + s1[0:8, :], -jnp.inf))
        blocks_i.append(i0[a:a + 1, :] * PEER_NKEYS + i1[0:8, :])
    blocks_s.append(s0[8:16, :] + s1[0:1, :])
    blocks_i.append(i0[8:16, :] * PEER_NKEYS + i1[0:1, :])
    return jnp.concatenate(blocks_s, axis=0), jnp.concatenate(blocks_i, axis=0)


def _route_kernel(x_ref, g_ref, wq_ref, keys_ref, tile_ref, gate_ref):
    K = PEER_TOPK
    h = _rms(x_ref[...], g_ref[...])
    q = jnp.dot(h.astype(BF16), wq_ref[...], preferred_element_type=F32)
    idx_rows, gate_rows = [], []
    for hd in range(PEER_HEADS):
        top_s, top_i = [], []
        for p in range(2):
            c0 = (hd * 2 + p) * LANES
            s_t = lax.dot_general(keys_ref[hd, p], q[:, c0:c0 + LANES], NT_DIMS,
                                  preferred_element_type=F32)
            vs, is_ = _topk_rows(s_t, K)
            top_s.append(vs)
            top_i.append(is_)
        cand_s, cand_i = _pair_candidates(top_s, top_i)
        best_s, best_i = _topk_rows(cand_s, K, payload=cand_i)
        e = jnp.exp(best_s - best_s[0:1, :])
        gate_rows.append(e / jnp.sum(e, axis=0, keepdims=True))
        idx_rows.append(best_i)
    tile_t = jnp.concatenate(idx_rows, axis=0) * EXPERT_WORD_ROWS
    for j in range(tile_ref.shape[0]):
        tile_ref[j] = tile_t[:, j * PEER_TOKEN_TILE:(j + 1) * PEER_TOKEN_TILE]
    gate_ref[...] = jnp.concatenate(gate_rows, axis=0).T


def _route(x1, g, wq_bf16, keys, tt):
    T, D = x1.shape
    tt = min(tt, T)
    TU = PEER_TOKEN_TILE
    NQ = wq_bf16.shape[1]
    tok = lambda w: pl.BlockSpec((tt, w), lambda i: (i, 0))
    return pl.pallas_call(
        _route_kernel,
        grid=(T // tt,),
        in_specs=[tok(D), pl.BlockSpec((1, D), lambda i: (0, 0)),
                  pl.BlockSpec((D, NQ), lambda i: (0, 0)),
                  pl.BlockSpec(keys.shape, lambda i: (0, 0, 0, 0))],
        out_specs=[pl.BlockSpec((tt // TU, PEER_SLOTS, TU), lambda i: (i, 0, 0)),
                   tok(PEER_SLOTS)],
        out_shape=[jax.ShapeDtypeStruct((T // TU, PEER_SLOTS, TU), jnp.int32),
                   jax.ShapeDtypeStruct((T, PEER_SLOTS), F32)],
        compiler_params=pltpu.CompilerParams(
            dimension_semantics=("parallel",), vmem_limit_bytes=VMEM_LIMIT),
    )(x1, g.reshape(1, D), wq_bf16, keys)


def _gather_tiles(tile_s, t, tab_ref):
    def words(k):
        r = pl.multiple_of(tile_s.at[k][t], EXPERT_WORD_ROWS)
        return tab_ref[pl.ds(r, EXPERT_WORD_ROWS), :]

    return jnp.concatenate(
        [pltpu.bitcast(jnp.concatenate([words(k), words(k + 1)], axis=0), BF16)
         for k in range(0, PEER_SLOTS, 2)], axis=0)


def _tile_fetch(tile_hbm, tile_s, sem):
    return pltpu.make_async_copy(tile_hbm.at[pl.program_id(0)], tile_s, sem)


def _hi_lo(a):
    hi = a.astype(BF16)
    return hi, (a - hi.astype(F32)).astype(BF16)


def _row_to_chunks(row):
    return jnp.concatenate([row[:, c * LANES:(c + 1) * LANES]
                            for c in range(row.shape[1] // LANES)], axis=0)


def _chunks_to_row(x8):
    return jnp.concatenate([x8[c:c + 1, :] for c in range(x8.shape[0])], axis=1)


def _token_loop(tu, token):
    def body(i, carry):
        for j in range(TOKENS_PER_ITER):
            token(i * TOKENS_PER_ITER + j)
        return carry

    lax.fori_loop(0, tu // TOKENS_PER_ITER, body, 0)


def _peer_u_kernel(tile_hbm, x_ref, ng_ref, gate_ref, tab_ref, g_ref, cm_ref, w_ref,
                   h_ref, r_ref, tile_s, sem):
    tu = gate_ref.shape[0]
    fetch = _tile_fetch(tile_hbm, tile_s, sem)
    fetch.start()
    h_ref[...] = _rms(x_ref[...], ng_ref[...])
    fetch.wait()

    def token(t):
        m = _gather_tiles(tile_s, t, tab_ref)
        hi, lo = _hi_lo(_row_to_chunks(h_ref[pl.ds(t, 1), :]))
        r = lax.dot_general(jnp.concatenate([hi, lo], axis=0), m, NT_DIMS,
                            preferred_element_type=F32)
        r8 = (r[0:8, :] + r[8:16, :]) * cm_ref[...]
        r_ref[pl.ds(t, 1), :] = jnp.sum(r8, axis=0, keepdims=True)

    _token_loop(tu, token)
    hi, lo = _hi_lo(r_ref[...])
    a = (jnp.dot(hi, g_ref[...], preferred_element_type=F32)
         + jnp.dot(lo, g_ref[...], preferred_element_type=F32))
    w_ref[...] = gate_ref[...] * (0.5 * a * (1.0 + lax.erf(a * (2.0 ** -0.5))))


def _peer_v_kernel(tile_hbm, w_ref, x_ref, fg_ref, tab_ref, e_ref, cm_ref, o_ref, wexp_ref,
                   tile_s, sem):
    tu = w_ref.shape[0]
    fetch = _tile_fetch(tile_hbm, tile_s, sem)
    fetch.start()
    hi, lo = _hi_lo(w_ref[...])
    wexp_ref[...] = (jnp.dot(hi, e_ref[...], preferred_element_type=F32)
                     + jnp.dot(lo, e_ref[...], preferred_element_type=F32))
    fetch.wait()

    def token(t):
        m = _gather_tiles(tile_s, t, tab_ref)
        hi8, lo8 = _hi_lo(wexp_ref[pl.ds(t, 1), :] * cm_ref[...])
        o = jnp.dot(jnp.concatenate([hi8, lo8], axis=0), m, preferred_element_type=F32)
        x2 = _row_to_chunks(x_ref[pl.ds(t, 1), :]) + o[0:8, :] + o[8:16, :]
        ss = jnp.sum(jnp.sum(x2 * x2, axis=1, keepdims=True), axis=0, keepdims=True)
        y8 = x2 * lax.rsqrt(ss * (1.0 / x_ref.shape[1]) + EPS) * fg_ref[...]
        o_ref[pl.ds(t, 1), :] = _chunks_to_row(y8)

    _token_loop(tu, token)


def _peer_consts():
    row = jnp.arange(PEER_SLOTS * 8)
    slot = row // 8
    chunk = row % 8 // 2 + 4 * (row % 2)
    expand = (jnp.arange(PEER_SLOTS)[:, None] == slot[None, :]).astype(BF16)
    chunk8 = (jnp.arange(8)[:, None] == chunk[None, :]).astype(F32)
    return expand, expand.T, chunk8


def _pack_table_kernel(w_ref, o_ref):
    tm, d = w_ref.shape
    w = w_ref[...]
    lo = pltpu.bitcast(w[:, :d // 2].astype(BF16).astype(F32), jnp.uint32) >> 16
    hi = pltpu.bitcast(w[:, d // 2:].astype(BF16).astype(F32), jnp.uint32) & jnp.uint32(0xFFFF0000)
    words = pltpu.bitcast(hi | lo, jnp.int32)
    for j in range(EXPERT_WORD_ROWS):
        o_ref[pl.ds(j, tm, stride=EXPERT_WORD_ROWS), :] = words[:, j * LANES:(j + 1) * LANES]


def _peer_table(w):
    n, d = w.shape
    tm = min(PACK_ROWS, n)
    return pl.pallas_call(
        _pack_table_kernel,
        grid=(n // tm,),
        in_specs=[pl.BlockSpec((tm, d), lambda i: (i, 0))],
        out_specs=pl.BlockSpec((tm * EXPERT_WORD_ROWS, LANES), lambda i: (i, 0)),
        out_shape=jax.ShapeDtypeStruct((n * EXPERT_WORD_ROWS, LANES), jnp.int32),
        compiler_params=pltpu.CompilerParams(dimension_semantics=("parallel",)),
    )(w)


def _peer_specs(tu):
    smem = pl.BlockSpec(memory_space=pl.ANY)
    tok = pl.BlockSpec((tu, PEER_SLOTS), lambda i: (i, 0))
    full = lambda a: pl.BlockSpec(a.shape, lambda i: (0,) * a.ndim)
    resident = lambda a: pl.BlockSpec(a.shape, lambda i: (0,) * a.ndim,
                                      pipeline_mode=pl.Buffered(1))
    return smem, tok, full, resident


def _peer_u(tile, x1, ng, gate, tab):
    T, D = x1.shape
    tu = PEER_TOKEN_TILE
    _, group, chunk8 = _peer_consts()
    smem, tok, full, resident = _peer_specs(tu)
    return pl.pallas_call(
        _peer_u_kernel,
        grid=(T // tu,),
        in_specs=[smem, pl.BlockSpec((tu, D), lambda i: (i, 0)),
                  pl.BlockSpec((1, D), lambda i: (0, 0)), tok,
                  resident(tab), full(group), full(chunk8)],
        out_specs=tok,
        out_shape=jax.ShapeDtypeStruct((T, PEER_SLOTS), F32),
        scratch_shapes=[pltpu.VMEM((tu, D), F32), pltpu.VMEM((tu, PEER_SLOTS * 8), F32),
                        pltpu.SMEM((PEER_SLOTS, tu), jnp.int32), pltpu.SemaphoreType.DMA(())],
        compiler_params=pltpu.CompilerParams(
            dimension_semantics=("parallel",), vmem_limit_bytes=VMEM_LIMIT),
    )(tile, x1, ng.reshape(1, D), gate, tab, group, chunk8)


def _peer_v(tile, w, x1, fg, tab):
    T, D = x1.shape
    tu = PEER_TOKEN_TILE
    expand, _, chunk8 = _peer_consts()
    smem, tok, full, resident = _peer_specs(tu)
    return pl.pallas_call(
        _peer_v_kernel,
        grid=(T // tu,),
        in_specs=[smem, tok, pl.BlockSpec((tu, D), lambda i: (i, 0)),
                  pl.BlockSpec((D // LANES, LANES), lambda i: (0, 0)),
                  resident(tab), full(expand), full(chunk8)],
        out_specs=pl.BlockSpec((tu, D), lambda i: (i, 0)),
        out_shape=jax.ShapeDtypeStruct((T, D), F32),
        scratch_shapes=[pltpu.VMEM((tu, PEER_SLOTS * 8), F32),
                        pltpu.SMEM((PEER_SLOTS, tu), jnp.int32), pltpu.SemaphoreType.DMA(())],
        compiler_params=pltpu.CompilerParams(
            dimension_semantics=("parallel",), vmem_limit_bytes=VMEM_LIMIT),
    )(tile, w, x1, fg.reshape(D // LANES, LANES), tab, expand, chunk8)


def kernel(x, mem, norm_mix_g, w_in, hg_lb, hg_norm_g, sc_conv_w, mem_norm_g, w_mem_kv,
           w_branch, w_out, norm_ffn_g, peer_w_q, peer_sub_keys, peer_u, peer_v, final_norm_g):
    B, S, D = x.shape
    T = B * S
    assert w_in.shape[0] == 1, "single-layer block"
    n_main = 9 * BRANCH_WIDTH
    lb_table = jnp.cumsum(jax.nn.softmax(hg_lb.astype(F32), axis=0), axis=0)
    xf = x.reshape(T, D)
    memf = mem.reshape(B * N_MEM, D)
    w_in_b = w_in[0].astype(BF16)
    proj = _norm_matmul(xf, norm_mix_g[0], w_in_b[:, :n_main], 1024, 768)
    gates = _norm_matmul(xf, norm_mix_g[0], w_in_b[:, n_main:], 1024, 768)
    kv = _norm_matmul(memf, mem_norm_g[0], w_mem_kv[0].astype(BF16), 1024, 512)
    y_hg = _hgrn(proj, lb_table[0], hg_norm_g[0], B, S)
    x1 = _mix(xf, proj, gates, y_hg, kv, sc_conv_w[0], w_branch[0].astype(BF16),
              w_out[0].astype(BF16), B, S, 256)
    tile, gate = _route(x1, norm_ffn_g[0], peer_w_q[0].astype(BF16), peer_sub_keys[0], 256)
    w = _peer_u(tile, x1, norm_ffn_g[0], gate, _peer_table(peer_u[0]))
    return _peer_v(tile, w, x1, final_norm_g, _peer_table(peer_v[0])).reshape(B, S, D)
```

```python
import jax
import jax.numpy as jnp
from jax import lax
from jax.experimental import pallas as pl
from jax.experimental.pallas import tpu as pltpu

F32 = jnp.float32
BF16 = jnp.bfloat16

EPS = 1e-6
LANES = 128
HG_HEADS = 4
HG_CHUNK = 64
HG_CHUNKS_PER_ITER = 8
HG_STATE_UNROLL = 16
MX_HEADS = 4
BRANCH_WIDTH = 512
N_MEM = 256
PEER_HEADS = 8
PEER_NKEYS = 128
PEER_TOPK = 16
PEER_SLOTS = PEER_HEADS * PEER_TOPK
EXPERT_WORD_ROWS = 4
PACK_ROWS = 512
PEER_TOKEN_TILE = 128
TOKENS_PER_ITER = 32
VMEM_LIMIT = 48 * 1024 * 1024

NT_DIMS = (((1,), (1,)), ((), ()))
TN_DIMS = (((0,), (0,)), ((), ()))


def _rms(x, g):
    return x * lax.rsqrt(jnp.mean(x * x, axis=-1, keepdims=True) + EPS) * g


def _norm_matmul_kernel(x_ref, g_ref, w_ref, o_ref, h_ref):
    @pl.when(pl.program_id(1) == 0)
    def _():
        h_ref[...] = _rms(x_ref[...], g_ref[...]).astype(BF16)

    o_ref[...] = jnp.dot(h_ref[...], w_ref[...], preferred_element_type=F32)


def _norm_matmul(x, g, w_bf16, tm, tn):
    T, D = x.shape
    N = w_bf16.shape[1]
    tm = min(tm, T)
    return pl.pallas_call(
        _norm_matmul_kernel,
        grid=(T // tm, N // tn),
        in_specs=[
            pl.BlockSpec((tm, D), lambda i, j: (i, 0)),
            pl.BlockSpec((1, D), lambda i, j: (0, 0)),
            pl.BlockSpec((D, tn), lambda i, j: (0, j)),
        ],
        out_specs=pl.BlockSpec((tm, tn), lambda i, j: (i, j)),
        out_shape=jax.ShapeDtypeStruct((T, N), F32),
        scratch_shapes=[pltpu.VMEM((tm, D), BF16)],
        compiler_params=pltpu.CompilerParams(
            dimension_semantics=("parallel", "arbitrary"), vmem_limit_bytes=VMEM_LIMIT),
    )(x, g.reshape(1, D), w_bf16)


def _hgrn_kernel(q_ref, v_ref, zf_ref, zb_ref, g_ref, lb_ref, ng_ref, o_ref,
                 st_ref, out_ref, qs_ref, kk_ref, cum_ref, sc_ref, qe_ref, dst_ref, dec_ref):
    S = q_ref.shape[0]
    C = HG_CHUNK
    n = S // C
    row = lax.broadcasted_iota(jnp.int32, (C, C), 0)
    col = lax.broadcasted_iota(jnp.int32, (C, C), 1)
    keeps = (col <= row, col >= row)
    z_refs = (zf_ref, zb_ref)
    unroll = min(HG_CHUNKS_PER_ITER, n)
    state_unroll = min(HG_STATE_UNROLL, n)

    def chunks(fn):
        def body(i, carry):
            for j in range(unroll):
                sl = pl.ds(pl.multiple_of((i * unroll + j) * C, C), C)
                for d in range(2):
                    fn(i * unroll + j, sl, d)
            return carry
        lax.fori_loop(0, n // unroll, body, 0)

    qs_ref[...] = jax.nn.silu(q_ref[...])

    def decay_pass(c, sl, d):
        lb = lb_ref[d:d + 1, :]
        z = z_refs[d][sl, :]
        lf = jnp.log(lb + (1.0 - lb) * jax.nn.sigmoid(z))
        kk_ref[d, sl, :] = (1.0 - lb) * jax.nn.sigmoid(-z)
        tri = keeps[d].astype(BF16)
        hi = lf.astype(BF16)
        r1 = lf - hi.astype(F32)
        mid = r1.astype(BF16)
        lo = (r1 - mid.astype(F32)).astype(BF16)
        cum_ref[d, sl, :] = (jnp.dot(tri, hi, preferred_element_type=F32)
                             + jnp.dot(tri, mid, preferred_element_type=F32)
                             + jnp.dot(tri, lo, preferred_element_type=F32))

    def score_pass(c, sl, d):
        G = cum_ref[d, sl, :]
        kk = kk_ref[d, sl, :]
        q = qs_ref[sl, :]
        mid = C // 2 if d else C // 2 - 1
        last = 0 if d else C - 1
        g_mid = G[mid:mid + 1, :]
        g_last = G[last:last + 1, :]
        q_r = q * jnp.exp(G - g_mid)
        k_r = kk * jnp.exp(g_mid - G)
        scores = lax.dot_general(q_r, k_r, NT_DIMS, preferred_element_type=F32)
        sc_ref[d, sl, :] = jnp.where(keeps[d], scores, 0.0)
        qe_ref[d, sl, :] = q * jnp.exp(G)
        dst_ref[d, c] = lax.dot_general(v_ref[sl, :], kk * jnp.exp(g_last - G), TN_DIMS,
                                        preferred_element_type=F32)
        dec_ref[d, c] = jnp.broadcast_to(jnp.exp(g_last), (8, LANES))

    def intra_pass(c, sl, d):
        out_ref[d, sl, :] = jnp.dot(sc_ref[d, sl, :], v_ref[sl, :], preferred_element_type=F32)

    chunks(decay_pass)
    chunks(score_pass)
    chunks(intra_pass)
    st_ref[...] = jnp.zeros_like(st_ref)

    def state_body(i, carry):
        for j in range(state_unroll):
            step = i * state_unroll + j
            for d, c in ((0, step), (1, n - 1 - step)):
                sl = pl.ds(pl.multiple_of(c * C, C), C)
                st = st_ref[d]
                out_ref[d, sl, :] += lax.dot_general(qe_ref[d, sl, :], st, NT_DIMS,
                                                     preferred_element_type=F32)
                st_ref[d] = st * dec_ref[d, c][0:1, :] + dst_ref[d, c]
        return carry

    lax.fori_loop(0, n // state_unroll, state_body, 0)
    o = out_ref[0] + out_ref[1]
    o = o * lax.rsqrt(jnp.mean(o * o, axis=-1, keepdims=True) + EPS)
    o_ref[...] = o * ng_ref[...] * jax.nn.silu(g_ref[...])


def _hgrn(proj, lb, ng, B, S):
    T = B * S
    H = HG_HEADS
    sec = lambda s: pl.BlockSpec((S, LANES), lambda b, h, s=s: (b, s * H + h))
    return pl.pallas_call(
        _hgrn_kernel,
        grid=(B, H),
        in_specs=[sec(0), sec(1), sec(2), sec(3), sec(4),
                  pl.BlockSpec((2, LANES), lambda b, h: (0, h)),
                  pl.BlockSpec((1, LANES), lambda b, h: (0, h))],
        out_specs=pl.BlockSpec((S, LANES), lambda b, h: (b, h)),
        out_shape=jax.ShapeDtypeStruct((T, H * LANES), F32),
        scratch_shapes=[pltpu.VMEM((2, LANES, LANES), F32),
                        pltpu.VMEM((2, S, LANES), F32),
                        pltpu.VMEM((S, LANES), F32),
                        pltpu.VMEM((2, S, LANES), F32),
                        pltpu.VMEM((2, S, LANES), F32),
                        pltpu.VMEM((2, S, HG_CHUNK), F32),
                        pltpu.VMEM((2, S, LANES), F32),
                        pltpu.VMEM((2, S // HG_CHUNK, LANES, LANES), F32),
                        pltpu.VMEM((2, S // HG_CHUNK, 8, LANES), F32)],
        compiler_params=pltpu.CompilerParams(
            dimension_semantics=("parallel", "parallel"), vmem_limit_bytes=VMEM_LIMIT),
    )(proj, proj, proj, proj, proj, lb, ng.reshape(1, -1))


def _mix_kernel(x_ref, sb_ref, sc_ref, sh_ref, scp_ref, shp_ref, scn_ref, shn_ref, mq_ref,
                gates_ref, yhg_ref, kv_ref, cw_ref, wb_ref, wo_ref, o_ref):
    i = pl.program_id(1)
    ni = pl.num_programs(1)
    ts = x_ref.shape[0]
    W = BRANCH_WIDTH
    u = sc_ref[...] * sh_ref[...]
    u_before = jnp.where(i > 0, scp_ref[7:8, :] * shp_ref[7:8, :], 0.0)
    u_after = jnp.where(i < ni - 1, scn_ref[0:1, :] * shn_ref[0:1, :], 0.0)
    rows = lax.broadcasted_iota(jnp.int32, (ts, W), 0)
    u_prev = jnp.where(rows == 0, u_before, pltpu.roll(u, 1, 0))
    u_next = jnp.where(rows == ts - 1, u_after, pltpu.roll(u, ts - 1, 0))
    cw = cw_ref[...]
    y_sc = sb_ref[...] * (cw[0:1, :] * u_prev + cw[1:2, :] * u + cw[2:3, :] * u_next)

    heads = []
    for h in range(MX_HEADS):
        qh = mq_ref[:, h * LANES:(h + 1) * LANES]
        kh = kv_ref[:, h * LANES:(h + 1) * LANES]
        vh = kv_ref[:, W + h * LANES:W + (h + 1) * LANES]
        logits = lax.dot_general(qh, kh, NT_DIMS, preferred_element_type=F32) * (LANES ** -0.5)
        e = jnp.exp(logits - jnp.max(logits, axis=-1, keepdims=True))
        p = e / jnp.sum(e, axis=-1, keepdims=True)
        heads.append(jnp.dot(p, vh, preferred_element_type=F32))
    y_mx = jnp.concatenate(heads, axis=1)

    D = x_ref.shape[1]
    merged = jnp.zeros((ts, D), F32)
    for n, y in enumerate((yhg_ref[...], y_sc, y_mx)):
        pr = jnp.dot(y.astype(BF16), wb_ref[n], preferred_element_type=F32)
        merged = merged + jax.nn.sigmoid(gates_ref[:, n * D:(n + 1) * D]) * pr
    out = jnp.dot(merged.astype(BF16), wo_ref[...], preferred_element_type=F32)
    o_ref[...] = x_ref[...] + out


def _mix(x, proj, gates, y_hg, kv, conv_w, wb_bf16, wo_bf16, B, S, ts):
    T, D = x.shape
    W = BRANCH_WIDTH
    ts = min(ts, S)
    nt = S // ts
    r8 = ts // 8
    tile = lambda b, i: b * nt + i
    sec = lambda s: pl.BlockSpec((ts, W), lambda b, i, s=s: (tile(b, i), s))
    before = lambda s: pl.BlockSpec(
        (8, W), lambda b, i, s=s: (jnp.maximum(tile(b, i) * r8 - 1, 0), s))
    after = lambda s: pl.BlockSpec(
        (8, W), lambda b, i, s=s: (jnp.minimum((tile(b, i) + 1) * r8, T // 8 - 1), s))
    return pl.pallas_call(
        _mix_kernel,
        grid=(B, nt),
        in_specs=[
            pl.BlockSpec((ts, D), lambda b, i: (tile(b, i), 0)),
            sec(5), sec(6), sec(7), before(6), before(7), after(6), after(7), sec(8),
            pl.BlockSpec((ts, 3 * D), lambda b, i: (tile(b, i), 0)),
            pl.BlockSpec((ts, W), lambda b, i: (tile(b, i), 0)),
            pl.BlockSpec((N_MEM, 2 * W), lambda b, i: (b, 0)),
            pl.BlockSpec((3, W), lambda b, i: (0, 0)),
            pl.BlockSpec((3, W, D), lambda b, i: (0, 0, 0)),
            pl.BlockSpec((D, D), lambda b, i: (0, 0)),
        ],
        out_specs=pl.BlockSpec((ts, D), lambda b, i: (tile(b, i), 0)),
        out_shape=jax.ShapeDtypeStruct((T, D), F32),
        compiler_params=pltpu.CompilerParams(
            dimension_semantics=("parallel", "parallel"), vmem_limit_bytes=VMEM_LIMIT),
    )(x, proj, proj, proj, proj, proj, proj, proj, proj, gates, y_hg, kv, conv_w, wb_bf16, wo_bf16)


def _topk_rows(s, k, payload=None):
    n = s.shape[0]
    iota = lax.broadcasted_iota(jnp.int32, s.shape, 0)
    vals, ids = [], []
    for _ in range(k):
        m = jnp.max(s, axis=0, keepdims=True)
        am = jnp.min(jnp.where(s == m, iota, n), axis=0, keepdims=True)
        hit = iota == am
        vals.append(m)
        if payload is None:
            ids.append(am)
        else:
            ids.append(jnp.sum(jnp.where(hit, payload, 0), axis=0, keepdims=True))
        s = jnp.where(hit, -jnp.inf, s)
    return jnp.concatenate(vals, axis=0), jnp.concatenate(ids, axis=0)


def _pair_candidates(top_s, top_i):
    K = PEER_TOPK
    assert K == 16
    (s0, s1), (i0, i1) = top_s, top_i
    blocks_s = [s0[0:1, :] + s1]
    blocks_i = [i0[0:1, :] * PEER_NKEYS + i1]
    b_row = lax.broadcasted_iota(jnp.int32, (8, s0.shape[1]), 0)
    for a in range(1, 8):
        blocks_s.append(jnp.where(b_row < K // (a + 1), s0[a:a + 1, :] + s1[0:8, :], -jnp.inf))
        blocks_i.append(i0[a:a + 1, :] * PEER_NKEYS + i1[0:8, :])
    blocks_s.append(s0[8:16, :] + s1[0:1, :])
    blocks_i.append(i0[8:16, :] * PEER_NKEYS + i1[0:1, :])
    return jnp.concatenate(blocks_s, axis=0), jnp.concatenate(blocks_i, axis=0)


def _route_kernel(x_ref, g_ref, wq_ref, keys_ref, tile_ref, gate_ref):
    K = PEER_TOPK
    h = _rms(x_ref[...], g_ref[...])
    q = jnp.dot(h.astype(BF16), wq_ref[...], preferred_element_type=F32)
    idx_rows, gate_rows = [], []
    for hd in range(PEER_HEADS):
        top_s, top_i = [], []
        for p in range(2):
            c0 = (hd * 2 + p) * LANES
            s_t = lax.dot_general(keys_ref[hd, p], q[:, c0:c0 + LANES], NT_DIMS,
                                  preferred_element_type=F32)
            vs, is_ = _topk_rows(s_t, K)
            top_s.append(vs)
            top_i.append(is_)
        cand_s, cand_i = _pair_candidates(top_s, top_i)
        best_s, best_i = _topk_rows(cand_s, K, payload=cand_i)
        e = jnp.exp(best_s - best_s[0:1, :])
        gate_rows.append(e / jnp.sum(e, axis=0, keepdims=True))
        idx_rows.append(best_i)
    tile_t = jnp.concatenate(idx_rows, axis=0) * EXPERT_WORD_ROWS
    for j in range(tile_ref.shape[0]):
        tile_ref[j] = tile_t[:, j * PEER_TOKEN_TILE:(j + 1) * PEER_TOKEN_TILE]
    gate_ref[...] = jnp.concatenate(gate_rows, axis=0).T


def _route(x1, g, wq_bf16, keys, tt):
    T, D = x1.shape
    tt = min(tt, T)
    TU = PEER_TOKEN_TILE
    NQ = wq_bf16.shape[1]
    tok = lambda w: pl.BlockSpec((tt, w), lambda i: (i, 0))
    return pl.pallas_call(
        _route_kernel,
        grid=(T // tt,),
        in_specs=[tok(D), pl.BlockSpec((1, D), lambda i: (0, 0)),
                  pl.BlockSpec((D, NQ), lambda i: (0, 0)),
                  pl.BlockSpec(keys.shape, lambda i: (0, 0, 0, 0))],
        out_specs=[pl.BlockSpec((tt // TU, PEER_SLOTS, TU), lambda i: (i, 0, 0)),
                   tok(PEER_SLOTS)],
        out_shape=[jax.ShapeDtypeStruct((T // TU, PEER_SLOTS, TU), jnp.int32),
                   jax.ShapeDtypeStruct((T, PEER_SLOTS), F32)],
        compiler_params=pltpu.CompilerParams(
            dimension_semantics=("parallel",), vmem_limit_bytes=VMEM_LIMIT),
    )(x1, g.reshape(1, D), wq_bf16, keys)


def _gather_tiles(tile_s, t, tab_ref):
    def words(k):
        r = pl.multiple_of(tile_s.at[k][t], EXPERT_WORD_ROWS)
        return tab_ref[pl.ds(r, EXPERT_WORD_ROWS), :]

    return jnp.concatenate(
        [pltpu.bitcast(jnp.concatenate([words(k), words(k + 1)], axis=0), BF16)
         for k in range(0, PEER_SLOTS, 2)], axis=0)


def _tile_fetch(tile_hbm, tile_s, sem):
    return pltpu.make_async_copy(tile_hbm.at[pl.program_id(0)], tile_s, sem)


def _hi_lo(a):
    hi = a.astype(BF16)
    return hi, (a - hi.astype(F32)).astype(BF16)


def _row_to_chunks(row):
    return jnp.concatenate([row[:, c * LANES:(c + 1) * LANES]
                            for c in range(row.shape[1] // LANES)], axis=0)


def _chunks_to_row(x8):
    return jnp.concatenate([x8[c:c + 1, :] for c in range(x8.shape[0])], axis=1)


def _token_loop(tu, token):
    def body(i, carry):
        for j in range(TOKENS_PER_ITER):
            token(i * TOKENS_PER_ITER + j)
        return carry

    lax.fori_loop(0, tu // TOKENS_PER_ITER, body, 0)


def _peer_u_kernel(tile_hbm, x_ref, ng_ref, gate_ref, tab_ref, g_ref, cm_ref, w_ref,
                   h_ref, r_ref, tile_s, sem):
    tu = gate_ref.shape[0]
    fetch = _tile_fetch(tile_hbm, tile_s, sem)
    fetch.start()
    h_ref[...] = _rms(x_ref[...], ng_ref[...])
    fetch.wait()

    def token(t):
        m = _gather_tiles(tile_s, t, tab_ref)
        hi, lo = _hi_lo(_row_to_chunks(h_ref[pl.ds(t, 1), :]))
        r = lax.dot_general(jnp.concatenate([hi, lo], axis=0), m, NT_DIMS,
                            preferred_element_type=F32)
        r8 = (r[0:8, :] + r[8:16, :]) * cm_ref[...]
        r_ref[pl.ds(t, 1), :] = jnp.sum(r8, axis=0, keepdims=True)

    _token_loop(tu, token)
    hi, lo = _hi_lo(r_ref[...])
    a = (jnp.dot(hi, g_ref[...], preferred_element_type=F32)
         + jnp.dot(lo, g_ref[...], preferred_element_type=F32))
    w_ref[...] = gate_ref[...] * (0.5 * a * (1.0 + lax.erf(a * (2.0 ** -0.5))))


def _peer_v_kernel(tile_hbm, w_ref, x_ref, fg_ref, tab_ref, e_ref, cm_ref, o_ref, wexp_ref,
                   tile_s, sem):
    tu = w_ref.shape[0]
    fetch = _tile_fetch(tile_hbm, tile_s, sem)
    fetch.start()
    hi, lo = _hi_lo(w_ref[...])
    wexp_ref[...] = (jnp.dot(hi, e_ref[...], preferred_element_type=F32)
                     + jnp.dot(lo, e_ref[...], preferred_element_type=F32))
    fetch.wait()

    def token(t):
        m = _gather_tiles(tile_s, t, tab_ref)
        hi8, lo8 = _hi_lo(wexp_ref[pl.ds(t, 1), :] * cm_ref[...])
        o = jnp.dot(jnp.concatenate([hi8, lo8], axis=0), m, preferred_element_type=F32)
        x2 = _row_to_chunks(x_ref[pl.ds(t, 1), :]) + o[0:8, :] + o[8:16, :]
        ss = jnp.sum(jnp.sum(x2 * x2, axis=1, keepdims=True), axis=0, keepdims=True)
        y8 = x2 * lax.rsqrt(ss * (1.0 / x_ref.shape[1]) + EPS) * fg_ref[...]
        o_ref[pl.ds(t, 1), :] = _chunks_to_row(y8)

    _token_loop(tu, token)


def _peer_consts():
    row = jnp.arange(PEER_SLOTS * 8)
    slot = row // 8
    chunk = row % 8 // 2 + 4 * (row % 2)
    expand = (jnp.arange(PEER_SLOTS)[:, None] == slot[None, :]).astype(BF16)
    chunk8 = (jnp.arange(8)[:, None] == chunk[None, :]).astype(F32)
    return expand, expand.T, chunk8


def _pack_table_kernel(w_ref, o_ref):
    tm, d = w_ref.shape
    w = w_ref[...]
    lo = pltpu.bitcast(w[:, :d // 2].astype(BF16).astype(F32), jnp.uint32) >> 16
    hi = pltpu.bitcast(w[:, d // 2:].astype(BF16).astype(F32), jnp.uint32) & jnp.uint32(0xFFFF0000)
    words = pltpu.bitcast(hi | lo, jnp.int32)
    for j in range(EXPERT_WORD_ROWS):
        o_ref[pl.ds(j, tm, stride=EXPERT_WORD_ROWS), :] = words[:, j * LANES:(j + 1) * LANES]


def _peer_table(w):
    n, d = w.shape
    tm = min(PACK_ROWS, n)
    return pl.pallas_call(
        _pack_table_kernel,
        grid=(n // tm,),
        in_specs=[pl.BlockSpec((tm, d), lambda i: (i, 0))],
        out_specs=pl.BlockSpec((tm * EXPERT_WORD_ROWS, LANES), lambda i: (i, 0)),
        out_shape=jax.ShapeDtypeStruct((n * EXPERT_WORD_ROWS, LANES), jnp.int32),
        compiler_params=pltpu.CompilerParams(dimension_semantics=("parallel",)),
    )(w)


def _peer_specs(tu):
    smem = pl.BlockSpec(memory_space=pl.ANY)
    tok = pl.BlockSpec((tu, PEER_SLOTS), lambda i: (i, 0))
    full = lambda a: pl.BlockSpec(a.shape, lambda i: (0,) * a.ndim)
    resident = lambda a: pl.BlockSpec(a.shape, lambda i: (0,) * a.ndim,
                                      pipeline_mode=pl.Buffered(1))
    return smem, tok, full, resident


def _peer_u(tile, x1, ng, gate, tab):
    T, D = x1.shape
    tu = PEER_TOKEN_TILE
    _, group, chunk8 = _peer_consts()
    smem, tok, full, resident = _peer_specs(tu)
    return pl.pallas_call(
        _peer_u_kernel,
        grid=(T // tu,),
        in_specs=[smem, pl.BlockSpec((tu, D), lambda i: (i, 0)),
                  pl.BlockSpec((1, D), lambda i: (0, 0)), tok,
                  resident(tab), full(group), full(chunk8)],
        out_specs=tok,
        out_shape=jax.ShapeDtypeStruct((T, PEER_SLOTS), F32),
        scratch_shapes=[pltpu.VMEM((tu, D), F32), pltpu.VMEM((tu, PEER_SLOTS * 8), F32),
                        pltpu.SMEM((PEER_SLOTS, tu), jnp.int32), pltpu.SemaphoreType.DMA(())],
        compiler_params=pltpu.CompilerParams(
            dimension_semantics=("parallel",), vmem_limit_bytes=VMEM_LIMIT),
    )(tile, x1, ng.reshape(1, D), gate, tab, group, chunk8)


def _peer_v(tile, w, x1, fg, tab):
    T, D = x1.shape
    tu = PEER_TOKEN_TILE
    expand, _, chunk8 = _peer_consts()
    smem, tok, full, resident = _peer_specs(tu)
    return pl.pallas_call(
        _peer_v_kernel,
        grid=(T // tu,),
        in_specs=[smem, tok, pl.BlockSpec((tu, D), lambda i: (i, 0)),
                  pl.BlockSpec((D // LANES, LANES), lambda i: (0, 0)),
                  resident(tab), full(expand), full(chunk8)],
        out_specs=pl.BlockSpec((tu, D), lambda i: (i, 0)),
        out_shape=jax.ShapeDtypeStruct((T, D), F32),
        scratch_shapes=[pltpu.VMEM((tu, PEER_SLOTS * 8), F32),
                        pltpu.SMEM((PEER_SLOTS, tu), jnp.int32), pltpu.SemaphoreType.DMA(())],
        compiler_params=pltpu.CompilerParams(
            dimension_semantics=("parallel",), vmem_limit_bytes=VMEM_LIMIT),
    )(tile, w, x1, fg.reshape(D // LANES, LANES), tab, expand, chunk8)


def kernel(x, mem, norm_mix_g, w_in, hg_lb, hg_norm_g, sc_conv_w, mem_norm_g, w_mem_kv,
           w_branch, w_out, norm_ffn_g, peer_w_q, peer_sub_keys, peer_u, peer_v, final_norm_g):
    B, S, D = x.shape
    T = B * S
    assert w_in.shape[0] == 1, "single-layer block"
    n_main = 9 * BRANCH_WIDTH
    lb_table = jnp.cumsum(jax.nn.softmax(hg_lb.astype(F32), axis=0), axis=0)
    xf = x.reshape(T, D)
    memf = mem.reshape(B * N_MEM, D)
    w_in_b = w_in[0].astype(BF16)
    proj = _norm_matmul(xf, norm_mix_g[0], w_in_b[:, :n_main], 2048, 768)
    gates = _norm_matmul(xf, norm_mix_g[0], w_in_b[:, n_main:], 2048, 768)
    kv = _norm_matmul(memf, mem_norm_g[0], w_mem_kv[0].astype(BF16), 1024, 512)
    y_hg = _hgrn(proj, lb_table[0], hg_norm_g[0], B, S)
    x1 = _mix(xf, proj, gates, y_hg, kv, sc_conv_w[0], w_branch[0].astype(BF16),
              w_out[0].astype(BF16), B, S, 256)
    tile, gate = _route(x1, norm_ffn_g[0], peer_w_q[0].astype(BF16), peer_sub_keys[0], 256)
    w = _peer_u(tile, x1, norm_ffn_g[0], gate, _peer_table(peer_u[0]))
    return _peer_v(tile, w, x1, final_norm_g, _peer_table(peer_v[0])).reshape(B, S, D)
```

```python
import jax
import jax.numpy as jnp
from jax import lax
from jax.experimental import pallas as pl
from jax.experimental.pallas import tpu as pltpu

F32 = jnp.float32
BF16 = jnp.bfloat16

EPS = 1e-6
LANES = 128
HG_HEADS = 4
HG_CHUNK = 64
HG_CHUNKS_PER_ITER = 8
HG_STATE_UNROLL = 16
MX_HEADS = 4
BRANCH_WIDTH = 512
N_MEM = 256
PEER_HEADS = 8
PEER_NKEYS = 128
PEER_TOPK = 16
PEER_SLOTS = PEER_HEADS * PEER_TOPK
EXPERT_WORD_ROWS = 4
PACK_ROWS = 512
PEER_TOKEN_TILE = 128
TOKENS_PER_ITER = 32
VMEM_LIMIT = 48 * 1024 * 1024

NT_DIMS = (((1,), (1,)), ((), ()))
TN_DIMS = (((0,), (0,)), ((), ()))


def _rms(x, g):
    return x * lax.rsqrt(jnp.mean(x * x, axis=-1, keepdims=True) + EPS) * g


def _norm_matmul_kernel(x_ref, g_ref, w_ref, o_ref, h_ref):
    @pl.when(pl.program_id(1) == 0)
    def _():
        h_ref[...] = _rms(x_ref[...], g_ref[...]).astype(BF16)

    o_ref[...] = jnp.dot(h_ref[...], w_ref[...], preferred_element_type=F32)


def _norm_matmul(x, g, w_bf16, tm, tn):
    T, D = x.shape
    N = w_bf16.shape[1]
    tm = min(tm, T)
    return pl.pallas_call(
        _norm_matmul_kernel,
        grid=(T // tm, N // tn),
        in_specs=[
            pl.BlockSpec((tm, D), lambda i, j: (i, 0)),
            pl.BlockSpec((1, D), lambda i, j: (0, 0)),
            pl.BlockSpec((D, tn), lambda i, j: (0, j)),
        ],
        out_specs=pl.BlockSpec((tm, tn), lambda i, j: (i, j)),
        out_shape=jax.ShapeDtypeStruct((T, N), F32),
        scratch_shapes=[pltpu.VMEM((tm, D), BF16)],
        compiler_params=pltpu.CompilerParams(
            dimension_semantics=("parallel", "arbitrary"), vmem_limit_bytes=VMEM_LIMIT),
    )(x, g.reshape(1, D), w_bf16)


def _hgrn_kernel(q_ref, v_ref, zf_ref, zb_ref, g_ref, lb_ref, ng_ref, o_ref,
                 st_ref, out_ref, qs_ref, kk_ref, cum_ref, sc_ref, qe_ref, dst_ref, dec_ref):
    S = q_ref.shape[0]
    C = HG_CHUNK
    n = S // C
    row = lax.broadcasted_iota(jnp.int32, (C, C), 0)
    col = lax.broadcasted_iota(jnp.int32, (C, C), 1)
    keeps = (col <= row, col >= row)
    z_refs = (zf_ref, zb_ref)
    unroll = min(HG_CHUNKS_PER_ITER, n)
    state_unroll = min(HG_STATE_UNROLL, n)

    def chunks(fn):
        def body(i, carry):
            for j in range(unroll):
                sl = pl.ds(pl.multiple_of((i * unroll + j) * C, C), C)
                for d in range(2):
                    fn(i * unroll + j, sl, d)
            return carry
        lax.fori_loop(0, n // unroll, body, 0)

    qs_ref[...] = jax.nn.silu(q_ref[...])

    def decay_pass(c, sl, d):
        lb = lb_ref[d:d + 1, :]
        z = z_refs[d][sl, :]
        lf = jnp.log(lb + (1.0 - lb) * jax.nn.sigmoid(z))
        kk_ref[d, sl, :] = (1.0 - lb) * jax.nn.sigmoid(-z)
        tri = keeps[d].astype(BF16)
        hi = lf.astype(BF16)
        r1 = lf - hi.astype(F32)
        mid = r1.astype(BF16)
        lo = (r1 - mid.astype(F32)).astype(BF16)
        cum_ref[d, sl, :] = (jnp.dot(tri, hi, preferred_element_type=F32)
                             + jnp.dot(tri, mid, preferred_element_type=F32)
                             + jnp.dot(tri, lo, preferred_element_type=F32))

    def score_pass(c, sl, d):
        G = cum_ref[d, sl, :]
        kk = kk_ref[d, sl, :]
        q = qs_ref[sl, :]
        mid = C // 2 if d else C // 2 - 1
        last = 0 if d else C - 1
        g_mid = G[mid:mid + 1, :]
        g_last = G[last:last + 1, :]
        q_r = q * jnp.exp(G - g_mid)
        k_r = kk * jnp.exp(g_mid - G)
        scores = lax.dot_general(q_r, k_r, NT_DIMS, preferred_element_type=F32)
        sc_ref[d, sl, :] = jnp.where(keeps[d], scores, 0.0)
        qe_ref[d, sl, :] = q * jnp.exp(G)
        dst_ref[d, c] = lax.dot_general(v_ref[sl, :], kk * jnp.exp(g_last - G), TN_DIMS,
                                        preferred_element_type=F32)
        dec_ref[d, c] = jnp.broadcast_to(jnp.exp(g_last), (8, LANES))

    def intra_pass(c, sl, d):
        out_ref[d, sl, :] = jnp.dot(sc_ref[d, sl, :], v_ref[sl, :], preferred_element_type=F32)

    chunks(decay_pass)
    chunks(score_pass)
    chunks(intra_pass)
    st_ref[...] = jnp.zeros_like(st_ref)

    def state_body(i, carry):
        for j in range(state_unroll):
            step = i * state_unroll + j
            for d, c in ((0, step), (1, n - 1 - step)):
                sl = pl.ds(pl.multiple_of(c * C, C), C)
                st = st_ref[d]
                out_ref[d, sl, :] += lax.dot_general(qe_ref[d, sl, :], st, NT_DIMS,
                                                     preferred_element_type=F32)
                st_ref[d] = st * dec_ref[d, c][0:1, :] + dst_ref[d, c]
        return carry

    lax.fori_loop(0, n // state_unroll, state_body, 0)
    o = out_ref[0] + out_ref[1]
    o = o * lax.rsqrt(jnp.mean(o * o, axis=-1, keepdims=True) + EPS)
    o_ref[...] = o * ng_ref[...] * jax.nn.silu(g_ref[...])


def _hgrn(proj, lb, ng, B, S):
    T = B * S
    H = HG_HEADS
    sec = lambda s: pl.BlockSpec((S, LANES), lambda b, h, s=s: (b, s * H + h))
    return pl.pallas_call(
        _hgrn_kernel,
        grid=(B, H),
        in_specs=[sec(0), sec(1), sec(2), sec(3), sec(4),
                  pl.BlockSpec((2, LANES), lambda b, h: (0, h)),
                  pl.BlockSpec((1, LANES), lambda b, h: (0, h))],
        out_specs=pl.BlockSpec((S, LANES), lambda b, h: (b, h)),
        out_shape=jax.ShapeDtypeStruct((T, H * LANES), F32),
        scratch_shapes=[pltpu.VMEM((2, LANES, LANES), F32),
                        pltpu.VMEM((2, S, LANES), F32),
                        pltpu.VMEM((S, LANES), F32),
                        pltpu.VMEM((2, S, LANES), F32),
                        pltpu.VMEM((2, S, LANES), F32),
                        pltpu.VMEM((2, S, HG_CHUNK), F32),
                        pltpu.VMEM((2, S, LANES), F32),
                        pltpu.VMEM((2, S // HG_CHUNK, LANES, LANES), F32),
                        pltpu.VMEM((2, S // HG_CHUNK, 8, LANES), F32)],
        compiler_params=pltpu.CompilerParams(
            dimension_semantics=("parallel", "parallel"), vmem_limit_bytes=VMEM_LIMIT),
    )(proj, proj, proj, proj, proj, lb, ng.reshape(1, -1))


def _mix_kernel(x_ref, sb_ref, sc_ref, sh_ref, scp_ref, shp_ref, scn_ref, shn_ref, mq_ref,
                gates_ref, yhg_ref, kv_ref, cw_ref, wb_ref, wo_ref, o_ref):
    i = pl.program_id(1)
    ni = pl.num_programs(1)
    ts = x_ref.shape[0]
    W = BRANCH_WIDTH
    u = sc_ref[...] * sh_ref[...]
    u_before = jnp.where(i > 0, scp_ref[7:8, :] * shp_ref[7:8, :], 0.0)
    u_after = jnp.where(i < ni - 1, scn_ref[0:1, :] * shn_ref[0:1, :], 0.0)
    rows = lax.broadcasted_iota(jnp.int32, (ts, W), 0)
    u_prev = jnp.where(rows == 0, u_before, pltpu.roll(u, 1, 0))
    u_next = jnp.where(rows == ts - 1, u_after, pltpu.roll(u, ts - 1, 0))
    cw = cw_ref[...]
    y_sc = sb_ref[...] * (cw[0:1, :] * u_prev + cw[1:2, :] * u + cw[2:3, :] * u_next)

    heads = []
    for h in range(MX_HEADS):
        qh = mq_ref[:, h * LANES:(h + 1) * LANES]
        kh = kv_ref[:, h * LANES:(h + 1) * LANES]
        vh = kv_ref[:, W + h * LANES:W + (h + 1) * LANES]
        logits = lax.dot_general(qh, kh, NT_DIMS, preferred_element_type=F32) * (LANES ** -0.5)
        e = jnp.exp(logits - jnp.max(logits, axis=-1, keepdims=True))
        p = e / jnp.sum(e, axis=-1, keepdims=True)
        heads.append(jnp.dot(p, vh, preferred_element_type=F32))
    y_mx = jnp.concatenate(heads, axis=1)

    D = x_ref.shape[1]
    merged = jnp.zeros((ts, D), F32)
    for n, y in enumerate((yhg_ref[...], y_sc, y_mx)):
        pr = jnp.dot(y.astype(BF16), wb_ref[n], preferred_element_type=F32)
        merged = merged + jax.nn.sigmoid(gates_ref[:, n * D:(n + 1) * D]) * pr
    out = jnp.dot(merged.astype(BF16), wo_ref[...], preferred_element_type=F32)
    o_ref[...] = x_ref[...] + out


def _mix(x, proj, gates, y_hg, kv, conv_w, wb_bf16, wo_bf16, B, S, ts):
    T, D = x.shape
    W = BRANCH_WIDTH
    ts = min(ts, S)
    nt = S // ts
    r8 = ts // 8
    tile = lambda b, i: b * nt + i
    sec = lambda s: pl.BlockSpec((ts, W), lambda b, i, s=s: (tile(b, i), s))
    before = lambda s: pl.BlockSpec(
        (8, W), lambda b, i, s=s: (jnp.maximum(tile(b, i) * r8 - 1, 0), s))
    after = lambda s: pl.BlockSpec(
        (8, W), lambda b, i, s=s: (jnp.minimum((tile(b, i) + 1) * r8, T // 8 - 1), s))
    return pl.pallas_call(
        _mix_kernel,
        grid=(B, nt),
        in_specs=[
            pl.BlockSpec((ts, D), lambda b, i: (tile(b, i), 0)),
            sec(5), sec(6), sec(7), before(6), before(7), after(6), after(7), sec(8),
            pl.BlockSpec((ts, 3 * D), lambda b, i: (tile(b, i), 0)),
            pl.BlockSpec((ts, W), lambda b, i: (tile(b, i), 0)),
            pl.BlockSpec((N_MEM, 2 * W), lambda b, i: (b, 0)),
            pl.BlockSpec((3, W), lambda b, i: (0, 0)),
            pl.BlockSpec((3, W, D), lambda b, i: (0, 0, 0)),
            pl.BlockSpec((D, D), lambda b, i: (0, 0)),
        ],
        out_specs=pl.BlockSpec((ts, D), lambda b, i: (tile(b, i), 0)),
        out_shape=jax.ShapeDtypeStruct((T, D), F32),
        compiler_params=pltpu.CompilerParams(
            dimension_semantics=("parallel", "parallel"), vmem_limit_bytes=VMEM_LIMIT),
    )(x, proj, proj, proj, proj, proj, proj, proj, proj, gates, y_hg, kv, conv_w, wb_bf16, wo_bf16)


def _topk_rows(s, k, payload=None):
    n = s.shape[0]
    iota = lax.broadcasted_iota(jnp.int32, s.shape, 0)
    vals, ids = [], []
    for _ in range(k):
        m = jnp.max(s, axis=0, keepdims=True)
        am = jnp.min(jnp.where(s == m, iota, n), axis=0, keepdims=True)
        hit = iota == am
        vals.append(m)
        if payload is None:
            ids.append(am)
        else:
            ids.append(jnp.sum(jnp.where(hit, payload, 0), axis=0, keepdims=True))
        s = jnp.where(hit, -jnp.inf, s)
    return jnp.concatenate(vals, axis=0), jnp.concatenate(ids, axis=0)


def _pair_candidates(top_s, top_i):
    K = PEER_TOPK
    assert K == 16
    (s0, s1), (i0, i1) = top_s, top_i
    blocks_s = [s0[0:1, :] + s1]
    blocks_i = [i0[0:1, :] * PEER_NKEYS + i1]
    b_row = lax.broadcasted_iota(jnp.int32, (8, s0.shape[1]), 0)
    for a in range(1, 8):
        blocks_s.append(jnp.where(b_row < K // (a + 1), s0[a:a + 1, :] + s1[0:8, :], -jnp.inf))
        blocks_i.append(i0[a:a + 1, :] * PEER_NKEYS + i1[0:8, :])
    blocks_s.append(s0[8:16, :] + s1[0:1, :])
    blocks_i.append(i0[8:16, :] * PEER_NKEYS + i1[0:1, :])
    return jnp.concatenate(blocks_s, axis=0), jnp.concatenate(blocks_i, axis=0)


def _route_kernel(x_ref, g_ref, wq_ref, keys_ref, tile_ref, gate_ref):
    K = PEER_TOPK
    h = _rms(x_ref[...], g_ref[...])
    q = jnp.dot(h.astype(BF16), wq_ref[...], preferred_element_type=F32)
    idx_rows, gate_rows = [], []
    for hd in range(PEER_HEADS):
        top_s, top_i = [], []
        for p in range(2):
            c0 = (hd * 2 + p) * LANES
            s_t = lax.dot_general(keys_ref[hd, p], q[:, c0:c0 + LANES], NT_DIMS,
                                  preferred_element_type=F32)
            vs, is_ = _topk_rows(s_t, K)
            top_s.append(vs)
            top_i.append(is_)
        cand_s, cand_i = _pair_candidates(top_s, top_i)
        best_s, best_i = _topk_rows(cand_s, K, payload=cand_i)
        e = jnp.exp(best_s - best_s[0:1, :])
        gate_rows.append(e / jnp.sum(e, axis=0, keepdims=True))
        idx_rows.append(best_i)
    tile_t = jnp.concatenate(idx_rows, axis=0) * EXPERT_WORD_ROWS
    for j in range(tile_ref.shape[0]):
        tile_ref[j] = tile_t[:, j * PEER_TOKEN_TILE:(j + 1) * PEER_TOKEN_TILE]
    gate_ref[...] = jnp.concatenate(gate_rows, axis=0).T


def _route(x1, g, wq_bf16, keys, tt):
    T, D = x1.shape
    tt = min(tt, T)
    TU = PEER_TOKEN_TILE
    NQ = wq_bf16.shape[1]
    tok = lambda w: pl.BlockSpec((tt, w), lambda i: (i, 0))
    return pl.pallas_call(
        _route_kernel,
        grid=(T // tt,),
        in_specs=[tok(D), pl.BlockSpec((1, D), lambda i: (0, 0)),
                  pl.BlockSpec((D, NQ), lambda i: (0, 0)),
                  pl.BlockSpec(keys.shape, lambda i: (0, 0, 0, 0))],
        out_specs=[pl.BlockSpec((tt // TU, PEER_SLOTS, TU), lambda i: (i, 0, 0)),
                   tok(PEER_SLOTS)],
        out_shape=[jax.ShapeDtypeStruct((T // TU, PEER_SLOTS, TU), jnp.int32),
                   jax.ShapeDtypeStruct((T, PEER_SLOTS), F32)],
        compiler_params=pltpu.CompilerParams(
            dimension_semantics=("parallel",), vmem_limit_bytes=VMEM_LIMIT),
    )(x1, g.reshape(1, D), wq_bf16, keys)


def _gather_tiles(tile_s, t, tab_ref):
    def words(k):
        r = pl.multiple_of(tile_s.at[k][t], EXPERT_WORD_ROWS)
        return tab_ref[pl.ds(r, EXPERT_WORD_ROWS), :]

    return jnp.concatenate(
        [pltpu.bitcast(jnp.concatenate([words(k), words(k + 1)], axis=0), BF16)
         for k in range(0, PEER_SLOTS, 2)], axis=0)


def _tile_fetch(tile_hbm, tile_s, sem):
    return pltpu.make_async_copy(tile_hbm.at[pl.program_id(0)], tile_s, sem)


def _hi_lo(a):
    hi = a.astype(BF16)
    return hi, (a - hi.astype(F32)).astype(BF16)


def _row_to_chunks(row):
    return jnp.concatenate([row[:, c * LANES:(c + 1) * LANES]
                            for c in range(row.shape[1] // LANES)], axis=0)


def _chunks_to_row(x8):
    return jnp.concatenate([x8[c:c + 1, :] for c in range(x8.shape[0])], axis=1)


def _token_loop(tu, token):
    def body(i, carry):
        for j in range(TOKENS_PER_ITER):
            token(i * TOKENS_PER_ITER + j)
        return carry

    lax.fori_loop(0, tu // TOKENS_PER_ITER, body, 0)


def _peer_u_kernel(tile_hbm, x_ref, ng_ref, gate_ref, tab_ref, g_ref, cm_ref, w_ref,
                   h_ref, r_ref, tile_s, sem):
    tu = gate_ref.shape[0]
    fetch = _tile_fetch(tile_hbm, tile_s, sem)
    fetch.start()
    h_ref[...] = _rms(x_ref[...], ng_ref[...])
    fetch.wait()

    def token(t):
        m = _gather_tiles(tile_s, t, tab_ref)
        hi, lo = _hi_lo(_row_to_chunks(h_ref[pl.ds(t, 1), :]))
        r = lax.dot_general(jnp.concatenate([hi, lo], axis=0), m, NT_DIMS,
                            preferred_element_type=F32)
        r8 = (r[0:8, :] + r[8:16, :]) * cm_ref[...]
        r_ref[pl.ds(t, 1), :] = jnp.sum(r8, axis=0, keepdims=True)

    _token_loop(tu, token)
    hi, lo = _hi_lo(r_ref[...])
    a = (jnp.dot(hi, g_ref[...], preferred_element_type=F32)
         + jnp.dot(lo, g_ref[...], preferred_element_type=F32))
    w_ref[...] = gate_ref[...] * (0.5 * a * (1.0 + lax.erf(a * (2.0 ** -0.5))))


def _peer_v_kernel(tile_hbm, w_ref, x_ref, fg_ref, tab_ref, e_ref, cm_ref, o_ref, wexp_ref,
                   tile_s, sem):
    tu = w_ref.shape[0]
    fetch = _tile_fetch(tile_hbm, tile_s, sem)
    fetch.start()
    hi, lo = _hi_lo(w_ref[...])
    wexp_ref[...] = (jnp.dot(hi, e_ref[...], preferred_element_type=F32)
                     + jnp.dot(lo, e_ref[...], preferred_element_type=F32))
    fetch.wait()

    def token(t):
        m = _gather_tiles(tile_s, t, tab_ref)
        hi8, lo8 = _hi_lo(wexp_ref[pl.ds(t, 1), :] * cm_ref[...])
        o = jnp.dot(jnp.concatenate([hi8, lo8], axis=0), m, preferred_element_type=F32)
        x2 = _row_to_chunks(x_ref[pl.ds(t, 1), :]) + o[0:8, :] + o[8:16, :]
        ss = jnp.sum(jnp.sum(x2 * x2, axis=1, keepdims=True), axis=0, keepdims=True)
        y8 = x2 * lax.rsqrt(ss * (1.0 / x_ref.shape[1]) + EPS) * fg_ref[...]
        o_ref[pl.ds(t, 1), :] = _chunks_to_row(y8)

    _token_loop(tu, token)


def _peer_consts():
    row = jnp.arange(PEER_SLOTS * 8)
    slot = row // 8
    chunk = row % 8 // 2 + 4 * (row % 2)
    expand = (jnp.arange(PEER_SLOTS)[:, None] == slot[None, :]).astype(BF16)
    chunk8 = (jnp.arange(8)[:, None] == chunk[None, :]).astype(F32)
    return expand, expand.T, chunk8


def _pack_table_kernel(w_ref, o_ref):
    tm, d = w_ref.shape
    w = w_ref[...]
    lo = pltpu.bitcast(w[:, :d // 2].astype(BF16).astype(F32), jnp.uint32) >> 16
    hi = pltpu.bitcast(w[:, d // 2:].astype(BF16).astype(F32), jnp.uint32) & jnp.uint32(0xFFFF0000)
    words = pltpu.bitcast(hi | lo, jnp.int32)
    for j in range(EXPERT_WORD_ROWS):
        o_ref[pl.ds(j, tm, stride=EXPERT_WORD_ROWS), :] = words[:, j * LANES:(j + 1) * LANES]


def _peer_table(w):
    n, d = w.shape
    tm = min(PACK_ROWS, n)
    return pl.pallas_call(
        _pack_table_kernel,
        grid=(n // tm,),
        in_specs=[pl.BlockSpec((tm, d), lambda i: (i, 0))],
        out_specs=pl.BlockSpec((tm * EXPERT_WORD_ROWS, LANES), lambda i: (i, 0)),
        out_shape=jax.ShapeDtypeStruct((n * EXPERT_WORD_ROWS, LANES), jnp.int32),
        compiler_params=pltpu.CompilerParams(dimension_semantics=("parallel",)),
    )(w)


def _peer_specs(tu):
    smem = pl.BlockSpec(memory_space=pl.ANY)
    tok = pl.BlockSpec((tu, PEER_SLOTS), lambda i: (i, 0))
    full = lambda a: pl.BlockSpec(a.shape, lambda i: (0,) * a.ndim)
    resident = lambda a: pl.BlockSpec(a.shape, lambda i: (0,) * a.ndim,
                                      pipeline_mode=pl.Buffered(1))
    return smem, tok, full, resident


def _peer_u(tile, x1, ng, gate, tab):
    T, D = x1.shape
    tu = PEER_TOKEN_TILE
    _, group, chunk8 = _peer_consts()
    smem, tok, full, resident = _peer_specs(tu)
    return pl.pallas_call(
        _peer_u_kernel,
        grid=(T // tu,),
        in_specs=[smem, pl.BlockSpec((tu, D), lambda i: (i, 0)),
                  pl.BlockSpec((1, D), lambda i: (0, 0)), tok,
                  resident(tab), full(group), full(chunk8)],
        out_specs=tok,
        out_shape=jax.ShapeDtypeStruct((T, PEER_SLOTS), F32),
        scratch_shapes=[pltpu.VMEM((tu, D), F32), pltpu.VMEM((tu, PEER_SLOTS * 8), F32),
                        pltpu.SMEM((PEER_SLOTS, tu), jnp.int32), pltpu.SemaphoreType.DMA(())],
        compiler_params=pltpu.CompilerParams(
            dimension_semantics=("parallel",), vmem_limit_bytes=VMEM_LIMIT),
    )(tile, x1, ng.reshape(1, D), gate, tab, group, chunk8)


def _peer_v(tile, w, x1, fg, tab):
    T, D = x1.shape
    tu = PEER_TOKEN_TILE
    expand, _, chunk8 = _peer_consts()
    smem, tok, full, resident = _peer_specs(tu)
    return pl.pallas_call(
        _peer_v_kernel,
        grid=(T // tu,),
        in_specs=[smem, tok, pl.BlockSpec((tu, D), lambda i: (i, 0)),
                  pl.BlockSpec((D // LANES, LANES), lambda i: (0, 0)),
                  resident(tab), full(expand), full(chunk8)],
        out_specs=pl.BlockSpec((tu, D), lambda i: (i, 0)),
        out_shape=jax.ShapeDtypeStruct((T, D), F32),
        scratch_shapes=[pltpu.VMEM((tu, PEER_SLOTS * 8), F32),
                        pltpu.SMEM((PEER_SLOTS, tu), jnp.int32), pltpu.SemaphoreType.DMA(())],
        compiler_params=pltpu.CompilerParams(
            dimension_semantics=("parallel",), vmem_limit_bytes=VMEM_LIMIT),
    )(tile, w, x1, fg.reshape(D // LANES, LANES), tab, expand, chunk8)


def kernel(x, mem, norm_mix_g, w_in, hg_lb, hg_norm_g, sc_conv_w, mem_norm_g, w_mem_kv,
           w_branch, w_out, norm_ffn_g, peer_w_q, peer_sub_keys, peer_u, peer_v, final_norm_g):
    B, S, D = x.shape
    T = B * S
    assert w_in.shape[0] == 1, "single-layer block"
    n_main = 9 * BRANCH_WIDTH
    lb_table = jnp.cumsum(jax.nn.softmax(hg_lb.astype(F32), axis=0), axis=0)
    xf = x.reshape(T, D)
    memf = mem.reshape(B * N_MEM, D)
    w_in_b = w_in[0].astype(BF16)
    proj = _norm_matmul(xf, norm_mix_g[0], w_in_b[:, :n_main], 2048, 768)
    gates = _norm_matmul(xf, norm_mix_g[0], w_in_b[:, n_main:], 2048, 768)
    kv = _norm_matmul(memf, mem_norm_g[0], w_mem_kv[0].astype(BF16), 1024, 512)
    y_hg = _hgrn(proj, lb_table[0], hg_norm_g[0], B, S)
    x1 = _mix(xf, proj, gates, y_hg, kv, sc_conv_w[0], w_branch[0].astype(BF16),
              w_out[0].astype(BF16), B, S, 512)
    tile, gate = _route(x1, norm_ffn_g[0], peer_w_q[0].astype(BF16), peer_sub_keys[0], 256)
    w = _peer_u(tile, x1, norm_ffn_g[0], gate, _peer_table(peer_u[0]))
    return _peer_v(tile, w, x1, final_norm_g, _peer_table(peer_v[0])).reshape(B, S, D)
```

```python
import jax
import jax.numpy as jnp
from jax import lax
from jax.experimental import pallas as pl
from jax.experimental.pallas import tpu as pltpu

F32 = jnp.float32
BF16 = jnp.bfloat16

EPS = 1e-6
LANES = 128
HG_HEADS = 4
HG_CHUNK = 64
HG_CHUNKS_PER_ITER = 8
HG_STATE_UNROLL = 16
MX_HEADS = 4
BRANCH_WIDTH = 512
N_MEM = 256
PEER_HEADS = 8
PEER_NKEYS = 128
PEER_TOPK = 16
PEER_SLOTS = PEER_HEADS * PEER_TOPK
EXPERT_WORD_ROWS = 4
PACK_ROWS = 512
PEER_TOKEN_TILE = 128
TOKENS_PER_ITER = 32
VMEM_LIMIT = 48 * 1024 * 1024

NT_DIMS = (((1,), (1,)), ((), ()))
TN_DIMS = (((0,), (0,)), ((), ()))


def _rms(x, g):
    return x * lax.rsqrt(jnp.mean(x * x, axis=-1, keepdims=True) + EPS) * g


def _norm_matmul_kernel(x_ref, g_ref, w_ref, o_ref, h_ref):
    @pl.when(pl.program_id(1) == 0)
    def _():
        h_ref[...] = _rms(x_ref[...], g_ref[...]).astype(BF16)

    o_ref[...] = jnp.dot(h_ref[...], w_ref[...], preferred_element_type=F32)


def _norm_matmul(x, g, w_bf16, tm, tn):
    T, D = x.shape
    N = w_bf16.shape[1]
    tm = min(tm, T)
    return pl.pallas_call(
        _norm_matmul_kernel,
        grid=(T // tm, N // tn),
        in_specs=[
            pl.BlockSpec((tm, D), lambda i, j: (i, 0)),
            pl.BlockSpec((1, D), lambda i, j: (0, 0)),
            pl.BlockSpec((D, tn), lambda i, j: (0, j)),
        ],
        out_specs=pl.BlockSpec((tm, tn), lambda i, j: (i, j)),
        out_shape=jax.ShapeDtypeStruct((T, N), F32),
        scratch_shapes=[pltpu.VMEM((tm, D), BF16)],
        compiler_params=pltpu.CompilerParams(
            dimension_semantics=("parallel", "arbitrary"), vmem_limit_bytes=VMEM_LIMIT),
    )(x, g.reshape(1, D), w_bf16)


def _hgrn_kernel(q_ref, v_ref, zf_ref, zb_ref, g_ref, lb_ref, ng_ref, o_ref,
                 st_ref, out_ref, qs_ref, kk_ref, cum_ref, sc_ref, qe_ref, dst_ref, dec_ref):
    S = q_ref.shape[0]
    C = HG_CHUNK
    n = S // C
    row = lax.broadcasted_iota(jnp.int32, (C, C), 0)
    col = lax.broadcasted_iota(jnp.int32, (C, C), 1)
    keeps = (col <= row, col >= row)
    z_refs = (zf_ref, zb_ref)
    unroll = min(HG_CHUNKS_PER_ITER, n)
    state_unroll = min(HG_STATE_UNROLL, n)

    def chunks(fn):
        def body(i, carry):
            for j in range(unroll):
                sl = pl.ds(pl.multiple_of((i * unroll + j) * C, C), C)
                for d in range(2):
                    fn(i * unroll + j, sl, d)
            return carry
        lax.fori_loop(0, n // unroll, body, 0)

    qs_ref[...] = jax.nn.silu(q_ref[...])

    def decay_pass(c, sl, d):
        lb = lb_ref[d:d + 1, :]
        z = z_refs[d][sl, :]
        lf = jnp.log(lb + (1.0 - lb) * jax.nn.sigmoid(z))
        kk_ref[d, sl, :] = (1.0 - lb) * jax.nn.sigmoid(-z)
        tri = keeps[d].astype(BF16)
        hi = lf.astype(BF16)
        r1 = lf - hi.astype(F32)
        mid = r1.astype(BF16)
        lo = (r1 - mid.astype(F32)).astype(BF16)
        cum_ref[d, sl, :] = (jnp.dot(tri, hi, preferred_element_type=F32)
                             + jnp.dot(tri, mid, preferred_element_type=F32)
                             + jnp.dot(tri, lo, preferred_element_type=F32))

    def score_pass(c, sl, d):
        G = cum_ref[d, sl, :]
        kk = kk_ref[d, sl, :]
        q = qs_ref[sl, :]
        mid = C // 2 if d else C // 2 - 1
        last = 0 if d else C - 1
        g_mid = G[mid:mid + 1, :]
        g_last = G[last:last + 1, :]
        q_r = q * jnp.exp(G - g_mid)
        k_r = kk * jnp.exp(g_mid - G)
        scores = lax.dot_general(q_r, k_r, NT_DIMS, preferred_element_type=F32)
        sc_ref[d, sl, :] = jnp.where(keeps[d], scores, 0.0)
        qe_ref[d, sl, :] = q * jnp.exp(G)
        dst_ref[d, c] = lax.dot_general(v_ref[sl, :], kk * jnp.exp(g_last - G), TN_DIMS,
                                        preferred_element_type=F32)
        dec_ref[d, c] = jnp.broadcast_to(jnp.exp(g_last), (8, LANES))

    def intra_pass(c, sl, d):
        out_ref[d, sl, :] = jnp.dot(sc_ref[d, sl, :], v_ref[sl, :], preferred_element_type=F32)

    chunks(decay_pass)
    chunks(score_pass)
    chunks(intra_pass)
    st_ref[...] = jnp.zeros_like(st_ref)

    def state_body(i, carry):
        for j in range(state_unroll):
            step = i * state_unroll + j
            for d, c in ((0, step), (1, n - 1 - step)):
                sl = pl.ds(pl.multiple_of(c * C, C), C)
                st = st_ref[d]
                out_ref[d, sl, :] += lax.dot_general(qe_ref[d, sl, :], st, NT_DIMS,
                                                     preferred_element_type=F32)
                st_ref[d] = st * dec_ref[d, c][0:1, :] + dst_ref[d, c]
        return carry

    lax.fori_loop(0, n // state_unroll, state_body, 0)
    o = out_ref[0] + out_ref[1]
    o = o * lax.rsqrt(jnp.mean(o * o, axis=-1, keepdims=True) + EPS)
    o_ref[...] = o * ng_ref[...] * jax.nn.silu(g_ref[...])


def _hgrn(proj, lb, ng, B, S):
    T = B * S
    H = HG_HEADS
    sec = lambda s: pl.BlockSpec((S, LANES), lambda b, h, s=s: (b, s * H + h))
    return pl.pallas_call(
        _hgrn_kernel,
        grid=(B, H),
        in_specs=[sec(0), sec(1), sec(2), sec(3), sec(4),
                  pl.BlockSpec((2, LANES), lambda b, h: (0, h)),
                  pl.BlockSpec((1, LANES), lambda b, h: (0, h))],
        out_specs=pl.BlockSpec((S, LANES), lambda b, h: (b, h)),
        out_shape=jax.ShapeDtypeStruct((T, H * LANES), F32),
        scratch_shapes=[pltpu.VMEM((2, LANES, LANES), F32),
                        pltpu.VMEM((2, S, LANES), F32),
                        pltpu.VMEM((S, LANES), F32),
                        pltpu.VMEM((2, S, LANES), F32),
                        pltpu.VMEM((2, S, LANES), F32),
                        pltpu.VMEM((2, S, HG_CHUNK), F32),
                        pltpu.VMEM((2, S, LANES), F32),
                        pltpu.VMEM((2, S // HG_CHUNK, LANES, LANES), F32),
                        pltpu.VMEM((2, S // HG_CHUNK, 8, LANES), F32)],
        compiler_params=pltpu.CompilerParams(
            dimension_semantics=("parallel", "parallel"), vmem_limit_bytes=VMEM_LIMIT),
    )(proj, proj, proj, proj, proj, lb, ng.reshape(1, -1))


def _mix_kernel(x_ref, sb_ref, sc_ref, sh_ref, scp_ref, shp_ref, scn_ref, shn_ref, mq_ref,
                gates_ref, yhg_ref, kv_ref, cw_ref, wb_ref, wo_ref, o_ref):
    i = pl.program_id(1)
    ni = pl.num_programs(1)
    ts = x_ref.shape[0]
    W = BRANCH_WIDTH
    u = sc_ref[...] * sh_ref[...]
    u_before = jnp.where(i > 0, scp_ref[7:8, :] * shp_ref[7:8, :], 0.0)
    u_after = jnp.where(i < ni - 1, scn_ref[0:1, :] * shn_ref[0:1, :], 0.0)
    rows = lax.broadcasted_iota(jnp.int32, (ts, W), 0)
    u_prev = jnp.where(rows == 0, u_before, pltpu.roll(u, 1, 0))
    u_next = jnp.where(rows == ts - 1, u_after, pltpu.roll(u, ts - 1, 0))
    cw = cw_ref[...]
    y_sc = sb_ref[...] * (cw[0:1, :] * u_prev + cw[1:2, :] * u + cw[2:3, :] * u_next)

    heads = []
    for h in range(MX_HEADS):
        qh = mq_ref[:, h * LANES:(h + 1) * LANES]
        kh = kv_ref[:, h * LANES:(h + 1) * LANES]
        vh = kv_ref[:, W + h * LANES:W + (h + 1) * LANES]
        logits = lax.dot_general(qh, kh, NT_DIMS, preferred_element_type=F32) * (LANES ** -0.5)
        e = jnp.exp(logits - jnp.max(logits, axis=-1, keepdims=True))
        p = e / jnp.sum(e, axis=-1, keepdims=True)
        heads.append(jnp.dot(p, vh, preferred_element_type=F32))
    y_mx = jnp.concatenate(heads, axis=1)

    D = x_ref.shape[1]
    merged = jnp.zeros((ts, D), F32)
    for n, y in enumerate((yhg_ref[...], y_sc, y_mx)):
        pr = jnp.dot(y.astype(BF16), wb_ref[n], preferred_element_type=F32)
        merged = merged + jax.nn.sigmoid(gates_ref[:, n * D:(n + 1) * D]) * pr
    out = jnp.dot(merged.astype(BF16), wo_ref[...], preferred_element_type=F32)
    o_ref[...] = x_ref[...] + out


def _mix(x, proj, gates, y_hg, kv, conv_w, wb_bf16, wo_bf16, B, S, ts):
    T, D = x.shape
    W = BRANCH_WIDTH
    ts = min(ts, S)
    nt = S // ts
    r8 = ts // 8
    tile = lambda b, i: b * nt + i
    sec = lambda s: pl.BlockSpec((ts, W), lambda b, i, s=s: (tile(b, i), s))
    before = lambda s: pl.BlockSpec(
        (8, W), lambda b, i, s=s: (jnp.maximum(tile(b, i) * r8 - 1, 0), s))
    after = lambda s: pl.BlockSpec(
        (8, W), lambda b, i, s=s: (jnp.minimum((tile(b, i) + 1) * r8, T // 8 - 1), s))
    return pl.pallas_call(
        _mix_kernel,
        grid=(B, nt),
        in_specs=[
            pl.BlockSpec((ts, D), lambda b, i: (tile(b, i), 0)),
            sec(5), sec(6), sec(7), before(6), before(7), after(6), after(7), sec(8),
            pl.BlockSpec((ts, 3 * D), lambda b, i: (tile(b, i), 0)),
            pl.BlockSpec((ts, W), lambda b, i: (tile(b, i), 0)),
            pl.BlockSpec((N_MEM, 2 * W), lambda b, i: (b, 0)),
            pl.BlockSpec((3, W), lambda b, i: (0, 0)),
            pl.BlockSpec((3, W, D), lambda b, i: (0, 0, 0)),
            pl.BlockSpec((D, D), lambda b, i: (0, 0)),
        ],
        out_specs=pl.BlockSpec((ts, D), lambda b, i: (tile(b, i), 0)),
        out_shape=jax.ShapeDtypeStruct((T, D), F32),
        compiler_params=pltpu.CompilerParams(
            dimension_semantics=("parallel", "parallel"), vmem_limit_bytes=VMEM_LIMIT),
    )(x, proj, proj, proj, proj, proj, proj, proj, proj, gates, y_hg, kv, conv_w, wb_bf16, wo_bf16)


def _topk_rows(s, k, payload=None):
    n = s.shape[0]
    iota = lax.broadcasted_iota(jnp.int32, s.shape, 0)
    vals, ids = [], []
    for _ in range(k):
        m = jnp.max(s, axis=0, keepdims=True)
        am = jnp.min(jnp.where(s == m, iota, n), axis=0, keepdims=True)
        hit = iota == am
        vals.append(m)
        if payload is None:
            ids.append(am)
        else:
            ids.append(jnp.sum(jnp.where(hit, payload, 0), axis=0, keepdims=True))
        s = jnp.where(hit, -jnp.inf, s)
    return jnp.concatenate(vals, axis=0), jnp.concatenate(ids, axis=0)


def _pair_candidates(top_s, top_i):
    K = PEER_TOPK
    assert K == 16
    (s0, s1), (i0, i1) = top_s, top_i
    blocks_s = [s0[0:1, :] + s1]
    blocks_i = [i0[0:1, :] * PEER_NKEYS + i1]
    b_row = lax.broadcasted_iota(jnp.int32, (8, s0.shape[1]), 0)
    for a in range(1, 8):
        blocks_s.append(jnp.where(b_row < K // (a + 1), s0[a:a + 1, :] + s1[0:8, :], -jnp.inf))
        blocks_i.append(i0[a:a + 1, :] * PEER_NKEYS + i1[0:8, :])
    blocks_s.append(s0[8:16, :] + s1[0:1, :])
    blocks_i.append(i0[8:16, :] * PEER_NKEYS + i1[0:1, :])
    return jnp.concatenate(blocks_s, axis=0), jnp.concatenate(blocks_i, axis=0)


def _route_kernel(x_ref, g_ref, wq_ref, keys_ref, tile_ref, gate_ref):
    K = PEER_TOPK
    h = _rms(x_ref[...], g_ref[...])
    q = jnp.dot(h.astype(BF16), wq_ref[...], preferred_element_type=F32)
    idx_rows, gate_rows = [], []
    for hd in range(PEER_HEADS):
        top_s, top_i = [], []
        for p in range(2):
            c0 = (hd * 2 + p) * LANES
            s_t = lax.dot_general(keys_ref[hd, p], q[:, c0:c0 + LANES], NT_DIMS,
                                  preferred_element_type=F32)
            vs, is_ = _topk_rows(s_t, K)
            top_s.append(vs)
            top_i.append(is_)
        cand_s, cand_i = _pair_candidates(top_s, top_i)
        best_s, best_i = _topk_rows(cand_s, K, payload=cand_i)
        e = jnp.exp(best_s - best_s[0:1, :])
        gate_rows.append(e / jnp.sum(e, axis=0, keepdims=True))
        idx_rows.append(best_i)
    tile_t = jnp.concatenate(idx_rows, axis=0) * EXPERT_WORD_ROWS
    for j in range(tile_ref.shape[0]):
        tile_ref[j] = tile_t[:, j * PEER_TOKEN_TILE:(j + 1) * PEER_TOKEN_TILE]
    gate_ref[...] = jnp.concatenate(gate_rows, axis=0).T


def _route(x1, g, wq_bf16, keys, tt):
    T, D = x1.shape
    tt = min(tt, T)
    TU = PEER_TOKEN_TILE
    NQ = wq_bf16.shape[1]
    tok = lambda w: pl.BlockSpec((tt, w), lambda i: (i, 0))
    return pl.pallas_call(
        _route_kernel,
        grid=(T // tt,),
        in_specs=[tok(D), pl.BlockSpec((1, D), lambda i: (0, 0)),
                  pl.BlockSpec((D, NQ), lambda i: (0, 0)),
                  pl.BlockSpec(keys.shape, lambda i: (0, 0, 0, 0))],
        out_specs=[pl.BlockSpec((tt // TU, PEER_SLOTS, TU), lambda i: (i, 0, 0)),
                   tok(PEER_SLOTS)],
        out_shape=[jax.ShapeDtypeStruct((T // TU, PEER_SLOTS, TU), jnp.int32),
                   jax.ShapeDtypeStruct((T, PEER_SLOTS), F32)],
        compiler_params=pltpu.CompilerParams(
            dimension_semantics=("parallel",), vmem_limit_bytes=VMEM_LIMIT),
    )(x1, g.reshape(1, D), wq_bf16, keys)


def _gather_tiles(tile_s, t, tab_ref):
    def words(k):
        r = pl.multiple_of(tile_s.at[k][t], EXPERT_WORD_ROWS)
        return tab_ref[pl.ds(r, EXPERT_WORD_ROWS), :]

    return jnp.concatenate(
        [pltpu.bitcast(jnp.concatenate([words(k), words(k + 1)], axis=0), BF16)
         for k in range(0, PEER_SLOTS, 2)], axis=0)


def _tile_fetch(tile_hbm, step, bufs, sems, slot):
    return pltpu.make_async_copy(tile_hbm.at[step], bufs[slot], sems.at[slot])


def _with_tile_block(tile_hbm, bufs, sems, run):
    i = pl.program_id(0)
    for slot in range(2):
        @pl.when(i % 2 == slot)
        def _(slot=slot):
            @pl.when(i + 1 < pl.num_programs(0))
            def _():
                _tile_fetch(tile_hbm, i + 1, bufs, sems, 1 - slot).start()

            _tile_fetch(tile_hbm, i, bufs, sems, slot).wait()
            run(bufs[slot])


def _hi_lo(a):
    hi = a.astype(BF16)
    return hi, (a - hi.astype(F32)).astype(BF16)


def _row_to_chunks(row):
    return jnp.concatenate([row[:, c * LANES:(c + 1) * LANES]
                            for c in range(row.shape[1] // LANES)], axis=0)


def _chunks_to_row(x8):
    return jnp.concatenate([x8[c:c + 1, :] for c in range(x8.shape[0])], axis=1)


def _token_loop(tu, token):
    def body(i, carry):
        for j in range(TOKENS_PER_ITER):
            token(i * TOKENS_PER_ITER + j)
        return carry

    lax.fori_loop(0, tu // TOKENS_PER_ITER, body, 0)


def _peer_u_kernel(tile_hbm, x_ref, ng_ref, gate_ref, tab_ref, g_ref, cm_ref, w_ref,
                   h_ref, r_ref, tile_a, tile_b, sems):
    tu = gate_ref.shape[0]
    bufs = (tile_a, tile_b)

    @pl.when(pl.program_id(0) == 0)
    def _():
        _tile_fetch(tile_hbm, 0, bufs, sems, 0).start()

    h_ref[...] = _rms(x_ref[...], ng_ref[...])

    def run(tile_s):
        def token(t):
            m = _gather_tiles(tile_s, t, tab_ref)
            hi, lo = _hi_lo(_row_to_chunks(h_ref[pl.ds(t, 1), :]))
            r = lax.dot_general(jnp.concatenate([hi, lo], axis=0), m, NT_DIMS,
                                preferred_element_type=F32)
            r8 = (r[0:8, :] + r[8:16, :]) * cm_ref[...]
            r_ref[pl.ds(t, 1), :] = jnp.sum(r8, axis=0, keepdims=True)

        _token_loop(tu, token)

    _with_tile_block(tile_hbm, bufs, sems, run)
    hi, lo = _hi_lo(r_ref[...])
    a = (jnp.dot(hi, g_ref[...], preferred_element_type=F32)
         + jnp.dot(lo, g_ref[...], preferred_element_type=F32))
    w_ref[...] = gate_ref[...] * (0.5 * a * (1.0 + lax.erf(a * (2.0 ** -0.5))))


def _peer_v_kernel(tile_hbm, w_ref, x_ref, fg_ref, tab_ref, e_ref, cm_ref, o_ref, wexp_ref,
                   tile_a, tile_b, sems):
    tu = w_ref.shape[0]
    bufs = (tile_a, tile_b)

    @pl.when(pl.program_id(0) == 0)
    def _():
        _tile_fetch(tile_hbm, 0, bufs, sems, 0).start()

    hi, lo = _hi_lo(w_ref[...])
    wexp_ref[...] = (jnp.dot(hi, e_ref[...], preferred_element_type=F32)
                     + jnp.dot(lo, e_ref[...], preferred_element_type=F32))

    def run(tile_s):
        def token(t):
            m = _gather_tiles(tile_s, t, tab_ref)
            hi8, lo8 = _hi_lo(wexp_ref[pl.ds(t, 1), :] * cm_ref[...])
            o = jnp.dot(jnp.concatenate([hi8, lo8], axis=0), m, preferred_element_type=F32)
            x2 = _row_to_chunks(x_ref[pl.ds(t, 1), :]) + o[0:8, :] + o[8:16, :]
            ss = jnp.sum(jnp.sum(x2 * x2, axis=1, keepdims=True), axis=0, keepdims=True)
            y8 = x2 * lax.rsqrt(ss * (1.0 / x_ref.shape[1]) + EPS) * fg_ref[...]
            o_ref[pl.ds(t, 1), :] = _chunks_to_row(y8)

        _token_loop(tu, token)

    _with_tile_block(tile_hbm, bufs, sems, run)


def _peer_consts():
    row = jnp.arange(PEER_SLOTS * 8)
    slot = row // 8
    chunk = row % 8 // 2 + 4 * (row % 2)
    expand = (jnp.arange(PEER_SLOTS)[:, None] == slot[None, :]).astype(BF16)
    chunk8 = (jnp.arange(8)[:, None] == chunk[None, :]).astype(F32)
    return expand, expand.T, chunk8


def _pack_table_kernel(w_ref, o_ref):
    tm, d = w_ref.shape
    w = w_ref[...]
    lo = pltpu.bitcast(w[:, :d // 2].astype(BF16).astype(F32), jnp.uint32) >> 16
    hi = pltpu.bitcast(w[:, d // 2:].astype(BF16).astype(F32), jnp.uint32) & jnp.uint32(0xFFFF0000)
    words = pltpu.bitcast(hi | lo, jnp.int32)
    for j in range(EXPERT_WORD_ROWS):
        o_ref[pl.ds(j, tm, stride=EXPERT_WORD_ROWS), :] = words[:, j * LANES:(j + 1) * LANES]


def _peer_table(w):
    n, d = w.shape
    tm = min(PACK_ROWS, n)
    return pl.pallas_call(
        _pack_table_kernel,
        grid=(n // tm,),
        in_specs=[pl.BlockSpec((tm, d), lambda i: (i, 0))],
        out_specs=pl.BlockSpec((tm * EXPERT_WORD_ROWS, LANES), lambda i: (i, 0)),
        out_shape=jax.ShapeDtypeStruct((n * EXPERT_WORD_ROWS, LANES), jnp.int32),
        compiler_params=pltpu.CompilerParams(dimension_semantics=("parallel",)),
    )(w)


def _peer_specs(tu):
    smem = pl.BlockSpec(memory_space=pl.ANY)
    tok = pl.BlockSpec((tu, PEER_SLOTS), lambda i: (i, 0))
    full = lambda a: pl.BlockSpec(a.shape, lambda i: (0,) * a.ndim)
    resident = lambda a: pl.BlockSpec(a.shape, lambda i: (0,) * a.ndim,
                                      pipeline_mode=pl.Buffered(1))
    return smem, tok, full, resident


def _peer_u(tile, x1, ng, gate, tab):
    T, D = x1.shape
    tu = PEER_TOKEN_TILE
    _, group, chunk8 = _peer_consts()
    smem, tok, full, resident = _peer_specs(tu)
    return pl.pallas_call(
        _peer_u_kernel,
        grid=(T // tu,),
        in_specs=[smem, pl.BlockSpec((tu, D), lambda i: (i, 0)),
                  pl.BlockSpec((1, D), lambda i: (0, 0)), tok,
                  resident(tab), full(group), full(chunk8)],
        out_specs=tok,
        out_shape=jax.ShapeDtypeStruct((T, PEER_SLOTS), F32),
        scratch_shapes=[pltpu.VMEM((tu, D), F32), pltpu.VMEM((tu, PEER_SLOTS * 8), F32),
                        pltpu.SMEM((PEER_SLOTS, tu), jnp.int32), pltpu.SMEM((PEER_SLOTS, tu), jnp.int32),
                        pltpu.SemaphoreType.DMA((2,))],
        compiler_params=pltpu.CompilerParams(
            dimension_semantics=("arbitrary",), vmem_limit_bytes=VMEM_LIMIT),
    )(tile, x1, ng.reshape(1, D), gate, tab, group, chunk8)


def _peer_v(tile, w, x1, fg, tab):
    T, D = x1.shape
    tu = PEER_TOKEN_TILE
    expand, _, chunk8 = _peer_consts()
    smem, tok, full, resident = _peer_specs(tu)
    return pl.pallas_call(
        _peer_v_kernel,
        grid=(T // tu,),
        in_specs=[smem, tok, pl.BlockSpec((tu, D), lambda i: (i, 0)),
                  pl.BlockSpec((D // LANES, LANES), lambda i: (0, 0)),
                  resident(tab), full(expand), full(chunk8)],
        out_specs=pl.BlockSpec((tu, D), lambda i: (i, 0)),
        out_shape=jax.ShapeDtypeStruct((T, D), F32),
        scratch_shapes=[pltpu.VMEM((tu, PEER_SLOTS * 8), F32),
                        pltpu.SMEM((PEER_SLOTS, tu), jnp.int32), pltpu.SMEM((PEER_SLOTS, tu), jnp.int32),
                        pltpu.SemaphoreType.DMA((2,))],
        compiler_params=pltpu.CompilerParams(
            dimension_semantics=("arbitrary",), vmem_limit_bytes=VMEM_LIMIT),
    )(tile, w, x1, fg.reshape(D // LANES, LANES), tab, expand, chunk8)


def kernel(x, mem, norm_mix_g, w_in, hg_lb, hg_norm_g, sc_conv_w, mem_norm_g, w_mem_kv,
           w_branch, w_out, norm_ffn_g, peer_w_q, peer_sub_keys, peer_u, peer_v, final_norm_g):
    B, S, D = x.shape
    T = B * S
    assert w_in.shape[0] == 1, "single-layer block"
    n_main = 9 * BRANCH_WIDTH
    lb_table = jnp.cumsum(jax.nn.softmax(hg_lb.astype(F32), axis=0), axis=0)
    xf = x.reshape(T, D)
    memf = mem.reshape(B * N_MEM, D)
    w_in_b = w_in[0].astype(BF16)
    proj = _norm_matmul(xf, norm_mix_g[0], w_in_b[:, :n_main], 2048, 768)
    gates = _norm_matmul(xf, norm_mix_g[0], w_in_b[:, n_main:], 2048, 768)
    kv = _norm_matmul(memf, mem_norm_g[0], w_mem_kv[0].astype(BF16), 1024, 512)
    y_hg = _hgrn(proj, lb_table[0], hg_norm_g[0], B, S)
    x1 = _mix(xf, proj, gates, y_hg, kv, sc_conv_w[0], w_branch[0].astype(BF16),
              w_out[0].astype(BF16), B, S, 512)
    tile, gate = _route(x1, norm_ffn_g[0], peer_w_q[0].astype(BF16), peer_sub_keys[0], 256)
    w = _peer_u(tile, x1, norm_ffn_g[0], gate, _peer_table(peer_u[0]))
    return _peer_v(tile, w, x1, final_norm_g, _peer_table(peer_v[0])).reshape(B, S, D)
```

```python
import jax
import jax.numpy as jnp
from jax import lax
from jax.experimental import pallas as pl
from jax.experimental.pallas import tpu as pltpu

F32 = jnp.float32
BF16 = jnp.bfloat16

EPS = 1e-6
LANES = 128
HG_HEADS = 4
HG_CHUNK = 64
HG_CHUNKS_PER_ITER = 8
HG_STATE_UNROLL = 16
MX_HEADS = 4
BRANCH_WIDTH = 512
N_MEM = 256
PEER_HEADS = 8
PEER_NKEYS = 128
PEER_TOPK = 16
PEER_SLOTS = PEER_HEADS * PEER_TOPK
EXPERT_WORD_ROWS = 4
PACK_ROWS = 512
PEER_TOKEN_TILE = 128
TOKENS_PER_ITER = 64
VMEM_LIMIT = 48 * 1024 * 1024

NT_DIMS = (((1,), (1,)), ((), ()))
TN_DIMS = (((0,), (0,)), ((), ()))


def _rms(x, g):
    return x * lax.rsqrt(jnp.mean(x * x, axis=-1, keepdims=True) + EPS) * g


def _norm_matmul_kernel(x_ref, g_ref, w_ref, o_ref, h_ref):
    @pl.when(pl.program_id(1) == 0)
    def _():
        h_ref[...] = _rms(x_ref[...], g_ref[...]).astype(BF16)

    o_ref[...] = jnp.dot(h_ref[...], w_ref[...], preferred_element_type=F32).astype(o_ref.dtype)


def _norm_matmul(x, g, w_bf16, tm, tn, out_dtype=F32):
    T, D = x.shape
    N = w_bf16.shape[1]
    tm = min(tm, T)
    return pl.pallas_call(
        _norm_matmul_kernel,
        grid=(T // tm, N // tn),
        in_specs=[
            pl.BlockSpec((tm, D), lambda i, j: (i, 0)),
            pl.BlockSpec((1, D), lambda i, j: (0, 0)),
            pl.BlockSpec((D, tn), lambda i, j: (0, j)),
        ],
        out_specs=pl.BlockSpec((tm, tn), lambda i, j: (i, j)),
        out_shape=jax.ShapeDtypeStruct((T, N), out_dtype),
        scratch_shapes=[pltpu.VMEM((tm, D), BF16)],
        compiler_params=pltpu.CompilerParams(
            dimension_semantics=("parallel", "arbitrary"), vmem_limit_bytes=VMEM_LIMIT),
    )(x, g.reshape(1, D), w_bf16)


def _hgrn_kernel(q_ref, v_ref, zf_ref, zb_ref, g_ref, lb_ref, ng_ref, o_ref,
                 st_ref, out_ref, qs_ref, kk_ref, cum_ref, sc_ref, qe_ref, dst_ref, dec_ref):
    S = q_ref.shape[0]
    C = HG_CHUNK
    n = S // C
    row = lax.broadcasted_iota(jnp.int32, (C, C), 0)
    col = lax.broadcasted_iota(jnp.int32, (C, C), 1)
    keeps = (col <= row, col >= row)
    z_refs = (zf_ref, zb_ref)
    unroll = min(HG_CHUNKS_PER_ITER, n)
    state_unroll = min(HG_STATE_UNROLL, n)

    def chunks(fn):
        def body(i, carry):
            for j in range(unroll):
                sl = pl.ds(pl.multiple_of((i * unroll + j) * C, C), C)
                for d in range(2):
                    fn(i * unroll + j, sl, d)
            return carry
        lax.fori_loop(0, n // unroll, body, 0)

    qs_ref[...] = jax.nn.silu(q_ref[...])

    def decay_pass(c, sl, d):
        lb = lb_ref[d:d + 1, :]
        z = z_refs[d][sl, :]
        lf = jnp.log(lb + (1.0 - lb) * jax.nn.sigmoid(z))
        kk_ref[d, sl, :] = (1.0 - lb) * jax.nn.sigmoid(-z)
        tri = keeps[d].astype(BF16)
        hi = lf.astype(BF16)
        r1 = lf - hi.astype(F32)
        mid = r1.astype(BF16)
        lo = (r1 - mid.astype(F32)).astype(BF16)
        cum_ref[d, sl, :] = (jnp.dot(tri, hi, preferred_element_type=F32)
                             + jnp.dot(tri, mid, preferred_element_type=F32)
                             + jnp.dot(tri, lo, preferred_element_type=F32))

    def score_pass(c, sl, d):
        G = cum_ref[d, sl, :]
        kk = kk_ref[d, sl, :]
        q = qs_ref[sl, :]
        mid = C // 2 if d else C // 2 - 1
        last = 0 if d else C - 1
        g_mid = G[mid:mid + 1, :]
        g_last = G[last:last + 1, :]
        q_r = q * jnp.exp(G - g_mid)
        k_r = kk * jnp.exp(g_mid - G)
        scores = lax.dot_general(q_r, k_r, NT_DIMS, preferred_element_type=F32)
        sc_ref[d, sl, :] = jnp.where(keeps[d], scores, 0.0)
        qe_ref[d, sl, :] = q * jnp.exp(G)
        dst_ref[d, c] = lax.dot_general(v_ref[sl, :], kk * jnp.exp(g_last - G), TN_DIMS,
                                        preferred_element_type=F32)
        dec_ref[d, c] = jnp.broadcast_to(jnp.exp(g_last), (8, LANES))

    def intra_pass(c, sl, d):
        out_ref[d, sl, :] = jnp.dot(sc_ref[d, sl, :], v_ref[sl, :], preferred_element_type=F32)

    chunks(decay_pass)
    chunks(score_pass)
    chunks(intra_pass)
    st_ref[...] = jnp.zeros_like(st_ref)

    def state_body(i, carry):
        for j in range(state_unroll):
            step = i * state_unroll + j
            for d, c in ((0, step), (1, n - 1 - step)):
                sl = pl.ds(pl.multiple_of(c * C, C), C)
                st = st_ref[d]
                out_ref[d, sl, :] += lax.dot_general(qe_ref[d, sl, :], st, NT_DIMS,
                                                     preferred_element_type=F32)
                st_ref[d] = st * dec_ref[d, c][0:1, :] + dst_ref[d, c]
        return carry

    lax.fori_loop(0, n // state_unroll, state_body, 0)
    o = out_ref[0] + out_ref[1]
    o = o * lax.rsqrt(jnp.mean(o * o, axis=-1, keepdims=True) + EPS)
    o_ref[...] = o * ng_ref[...] * jax.nn.silu(g_ref[...])


def _hgrn(proj, lb, ng, B, S):
    T = B * S
    H = HG_HEADS
    sec = lambda s: pl.BlockSpec((S, LANES), lambda b, h, s=s: (b, s * H + h))
    return pl.pallas_call(
        _hgrn_kernel,
        grid=(B, H),
        in_specs=[sec(0), sec(1), sec(2), sec(3), sec(4),
                  pl.BlockSpec((2, LANES), lambda b, h: (0, h)),
                  pl.BlockSpec((1, LANES), lambda b, h: (0, h))],
        out_specs=pl.BlockSpec((S, LANES), lambda b, h: (b, h)),
        out_shape=jax.ShapeDtypeStruct((T, H * LANES), F32),
        scratch_shapes=[pltpu.VMEM((2, LANES, LANES), F32),
                        pltpu.VMEM((2, S, LANES), F32),
                        pltpu.VMEM((S, LANES), F32),
                        pltpu.VMEM((2, S, LANES), F32),
                        pltpu.VMEM((2, S, LANES), F32),
                        pltpu.VMEM((2, S, HG_CHUNK), F32),
                        pltpu.VMEM((2, S, LANES), F32),
                        pltpu.VMEM((2, S // HG_CHUNK, LANES, LANES), F32),
                        pltpu.VMEM((2, S // HG_CHUNK, 8, LANES), F32)],
        compiler_params=pltpu.CompilerParams(
            dimension_semantics=("parallel", "parallel"), vmem_limit_bytes=VMEM_LIMIT),
    )(proj, proj, proj, proj, proj, lb, ng.reshape(1, -1))


def _mix_kernel(x_ref, sb_ref, sc_ref, sh_ref, scp_ref, shp_ref, scn_ref, shn_ref, mq_ref,
                gates_ref, yhg_ref, kv_ref, cw_ref, wb_ref, wo_ref, o_ref):
    i = pl.program_id(1)
    ni = pl.num_programs(1)
    ts = x_ref.shape[0]
    W = BRANCH_WIDTH
    u = sc_ref[...] * sh_ref[...]
    u_before = jnp.where(i > 0, scp_ref[7:8, :] * shp_ref[7:8, :], 0.0)
    u_after = jnp.where(i < ni - 1, scn_ref[0:1, :] * shn_ref[0:1, :], 0.0)
    rows = lax.broadcasted_iota(jnp.int32, (ts, W), 0)
    u_prev = jnp.where(rows == 0, u_before, pltpu.roll(u, 1, 0))
    u_next = jnp.where(rows == ts - 1, u_after, pltpu.roll(u, ts - 1, 0))
    cw = cw_ref[...]
    y_sc = sb_ref[...] * (cw[0:1, :] * u_prev + cw[1:2, :] * u + cw[2:3, :] * u_next)

    heads = []
    for h in range(MX_HEADS):
        qh = mq_ref[:, h * LANES:(h + 1) * LANES]
        kh = kv_ref[:, h * LANES:(h + 1) * LANES]
        vh = kv_ref[:, W + h * LANES:W + (h + 1) * LANES]
        logits = lax.dot_general(qh, kh, NT_DIMS, preferred_element_type=F32) * (LANES ** -0.5)
        e = jnp.exp(logits - jnp.max(logits, axis=-1, keepdims=True))
        p = e / jnp.sum(e, axis=-1, keepdims=True)
        heads.append(jnp.dot(p, vh, preferred_element_type=F32))
    y_mx = jnp.concatenate(heads, axis=1)

    D = x_ref.shape[1]
    merged = jnp.zeros((ts, D), F32)
    for n, y in enumerate((yhg_ref[...], y_sc, y_mx)):
        pr = jnp.dot(y.astype(BF16), wb_ref[n], preferred_element_type=F32)
        merged = merged + jax.nn.sigmoid(gates_ref[:, n * D:(n + 1) * D].astype(F32)) * pr
    out = jnp.dot(merged.astype(BF16), wo_ref[...], preferred_element_type=F32)
    o_ref[...] = x_ref[...] + out


def _mix(x, proj, gates, y_hg, kv, conv_w, wb_bf16, wo_bf16, B, S, ts):
    T, D = x.shape
    W = BRANCH_WIDTH
    ts = min(ts, S)
    nt = S // ts
    r8 = ts // 8
    tile = lambda b, i: b * nt + i
    sec = lambda s: pl.BlockSpec((ts, W), lambda b, i, s=s: (tile(b, i), s))
    before = lambda s: pl.BlockSpec(
        (8, W), lambda b, i, s=s: (jnp.maximum(tile(b, i) * r8 - 1, 0), s))
    after = lambda s: pl.BlockSpec(
        (8, W), lambda b, i, s=s: (jnp.minimum((tile(b, i) + 1) * r8, T // 8 - 1), s))
    return pl.pallas_call(
        _mix_kernel,
        grid=(B, nt),
        in_specs=[
            pl.BlockSpec((ts, D), lambda b, i: (tile(b, i), 0)),
            sec(5), sec(6), sec(7), before(6), before(7), after(6), after(7), sec(8),
            pl.BlockSpec((ts, 3 * D), lambda b, i: (tile(b, i), 0)),
            pl.BlockSpec((ts, W), lambda b, i: (tile(b, i), 0)),
            pl.BlockSpec((N_MEM, 2 * W), lambda b, i: (b, 0)),
            pl.BlockSpec((3, W), lambda b, i: (0, 0)),
            pl.BlockSpec((3, W, D), lambda b, i: (0, 0, 0)),
            pl.BlockSpec((D, D), lambda b, i: (0, 0)),
        ],
        out_specs=pl.BlockSpec((ts, D), lambda b, i: (tile(b, i), 0)),
        out_shape=jax.ShapeDtypeStruct((T, D), F32),
        compiler_params=pltpu.CompilerParams(
            dimension_semantics=("parallel", "parallel"), vmem_limit_bytes=VMEM_LIMIT),
    )(x, proj, proj, proj, proj, proj, proj, proj, proj, gates, y_hg, kv, conv_w, wb_bf16, wo_bf16)


def _topk_rows(s, k, payload=None):
    n = s.shape[0]
    iota = lax.broadcasted_iota(jnp.int32, s.shape, 0)
    vals, ids = [], []
    for _ in range(k):
        m = jnp.max(s, axis=0, keepdims=True)
        am = jnp.min(jnp.where(s == m, iota, n), axis=0, keepdims=True)
        hit = iota == am
        vals.append(m)
        if payload is None:
            ids.append(am)
        else:
            ids.append(jnp.sum(jnp.where(hit, payload, 0), axis=0, keepdims=True))
        s = jnp.where(hit, -jnp.inf, s)
    return jnp.concatenate(vals, axis=0), jnp.concatenate(ids, axis=0)


def _pair_candidates(top_s, top_i):
    K = PEER_TOPK
    assert K == 16
    (s0, s1), (i0, i1) = top_s, top_i
    blocks_s = [s0[0:1, :] + s1]
    blocks_i = [i0[0:1, :] * PEER_NKEYS + i1]
    b_row = lax.broadcasted_iota(jnp.int32, (8, s0.shape[1]), 0)
    for a in range(1, 8):
        blocks_s.append(jnp.where(b_row < K // (a + 1), s0[a:a + 1, :] + s1[0:8, :], -jnp.inf))
        blocks_i.append(i0[a:a + 1, :] * PEER_NKEYS + i1[0:8, :])
    blocks_s.append(s0[8:16, :] + s1[0:1, :])
    blocks_i.append(i0[8:16, :] * PEER_NKEYS + i1[0:1, :])
    return jnp.concatenate(blocks_s, axis=0), jnp.concatenate(blocks_i, axis=0)


def _route_kernel(x_ref, g_ref, wq_ref, keys_ref, tile_ref, gate_ref):
    K = PEER_TOPK
    h = _rms(x_ref[...], g_ref[...])
    q = jnp.dot(h.astype(BF16), wq_ref[...], preferred_element_type=F32)
    idx_rows, gate_rows = [], []
    for hd in range(PEER_HEADS):
        top_s, top_i = [], []
        for p in range(2):
            c0 = (hd * 2 + p) * LANES
            s_t = lax.dot_general(keys_ref[hd, p], q[:, c0:c0 + LANES], NT_DIMS,
                                  preferred_element_type=F32)
            vs, is_ = _topk_rows(s_t, K)
            top_s.append(vs)
            top_i.append(is_)
        cand_s, cand_i = _pair_candidates(top_s, top_i)
        best_s, best_i = _topk_rows(cand_s, K, payload=cand_i)
        e = jnp.exp(best_s - best_s[0:1, :])
        gate_rows.append(e / jnp.sum(e, axis=0, keepdims=True))
        idx_rows.append(best_i)
    tile_t = jnp.concatenate(idx_rows, axis=0) * EXPERT_WORD_ROWS
    for j in range(tile_ref.shape[0]):
        tile_ref[j] = tile_t[:, j * PEER_TOKEN_TILE:(j + 1) * PEER_TOKEN_TILE]
    gate_ref[...] = jnp.concatenate(gate_rows, axis=0).T


def _route(x1, g, wq_bf16, keys, tt):
    T, D = x1.shape
    tt = min(tt, T)
    TU = PEER_TOKEN_TILE
    NQ = wq_bf16.shape[1]
    tok = lambda w: pl.BlockSpec((tt, w), lambda i: (i, 0))
    return pl.pallas_call(
        _route_kernel,
        grid=(T // tt,),
        in_specs=[tok(D), pl.BlockSpec((1, D), lambda i: (0, 0)),
                  pl.BlockSpec((D, NQ), lambda i: (0, 0)),
                  pl.BlockSpec(keys.shape, lambda i: (0, 0, 0, 0))],
        out_specs=[pl.BlockSpec((tt // TU, PEER_SLOTS, TU), lambda i: (i, 0, 0)),
                   tok(PEER_SLOTS)],
        out_shape=[jax.ShapeDtypeStruct((T // TU, PEER_SLOTS, TU), jnp.int32),
                   jax.ShapeDtypeStruct((T, PEER_SLOTS), F32)],
        compiler_params=pltpu.CompilerParams(
            dimension_semantics=("parallel",), vmem_limit_bytes=VMEM_LIMIT),
    )(x1, g.reshape(1, D), wq_bf16, keys)


def _gather_tiles(tile_s, t, tab_ref):
    def words(k):
        r = pl.multiple_of(tile_s.at[k][t], EXPERT_WORD_ROWS)
        return tab_ref[pl.ds(r, EXPERT_WORD_ROWS), :]

    return jnp.concatenate(
        [pltpu.bitcast(jnp.concatenate([words(k), words(k + 1)], axis=0), BF16)
         for k in range(0, PEER_SLOTS, 2)], axis=0)


def _tile_fetch(tile_hbm, step, bufs, sems, slot):
    return pltpu.make_async_copy(tile_hbm.at[step], bufs[slot], sems.at[slot])


def _with_tile_block(tile_hbm, bufs, sems, run):
    i = pl.program_id(0)
    for slot in range(2):
        @pl.when(i % 2 == slot)
        def _(slot=slot):
            @pl.when(i + 1 < pl.num_programs(0))
            def _():
                _tile_fetch(tile_hbm, i + 1, bufs, sems, 1 - slot).start()

            _tile_fetch(tile_hbm, i, bufs, sems, slot).wait()
            run(bufs[slot])


def _hi_lo(a):
    hi = a.astype(BF16)
    return hi, (a - hi.astype(F32)).astype(BF16)


def _row_to_chunks(row):
    return jnp.concatenate([row[:, c * LANES:(c + 1) * LANES]
                            for c in range(row.shape[1] // LANES)], axis=0)


def _chunks_to_row(x8):
    return jnp.concatenate([x8[c:c + 1, :] for c in range(x8.shape[0])], axis=1)


def _token_loop(tu, token):
    def body(i, carry):
        for j in range(TOKENS_PER_ITER):
            token(i * TOKENS_PER_ITER + j)
        return carry

    lax.fori_loop(0, tu // TOKENS_PER_ITER, body, 0)


def _peer_u_kernel(tile_hbm, x_ref, ng_ref, gate_ref, tab_ref, g_ref, cm_ref, w_ref,
                   h_ref, r_ref, tile_a, tile_b, sems):
    tu = gate_ref.shape[0]
    bufs = (tile_a, tile_b)

    @pl.when(pl.program_id(0) == 0)
    def _():
        _tile_fetch(tile_hbm, 0, bufs, sems, 0).start()

    h_ref[...] = _rms(x_ref[...], ng_ref[...])

    def run(tile_s):
        def token(t):
            m = _gather_tiles(tile_s, t, tab_ref)
            hi, lo = _hi_lo(_row_to_chunks(h_ref[pl.ds(t, 1), :]))
            r = lax.dot_general(jnp.concatenate([hi, lo], axis=0), m, NT_DIMS,
                                preferred_element_type=F32)
            r8 = (r[0:8, :] + r[8:16, :]) * cm_ref[...]
            r_ref[pl.ds(t, 1), :] = jnp.sum(r8, axis=0, keepdims=True)

        _token_loop(tu, token)

    _with_tile_block(tile_hbm, bufs, sems, run)
    hi, lo = _hi_lo(r_ref[...])
    a = (jnp.dot(hi, g_ref[...], preferred_element_type=F32)
         + jnp.dot(lo, g_ref[...], preferred_element_type=F32))
    w_ref[...] = gate_ref[...] * (0.5 * a * (1.0 + lax.erf(a * (2.0 ** -0.5))))


def _peer_v_kernel(tile_hbm, w_ref, x_ref, fg_ref, tab_ref, e_ref, cm_ref, o_ref, wexp_ref,
                   tile_a, tile_b, sems):
    tu = w_ref.shape[0]
    bufs = (tile_a, tile_b)

    @pl.when(pl.program_id(0) == 0)
    def _():
        _tile_fetch(tile_hbm, 0, bufs, sems, 0).start()

    hi, lo = _hi_lo(w_ref[...])
    wexp_ref[...] = (jnp.dot(hi, e_ref[...], preferred_element_type=F32)
                     + jnp.dot(lo, e_ref[...], preferred_element_type=F32))

    def run(tile_s):
        def token(t):
            m = _gather_tiles(tile_s, t, tab_ref)
            hi8, lo8 = _hi_lo(wexp_ref[pl.ds(t, 1), :] * cm_ref[...])
            o = jnp.dot(jnp.concatenate([hi8, lo8], axis=0), m, preferred_element_type=F32)
            x2 = _row_to_chunks(x_ref[pl.ds(t, 1), :]) + o[0:8, :] + o[8:16, :]
            ss = jnp.sum(jnp.sum(x2 * x2, axis=1, keepdims=True), axis=0, keepdims=True)
            y8 = x2 * lax.rsqrt(ss * (1.0 / x_ref.shape[1]) + EPS) * fg_ref[...]
            o_ref[pl.ds(t, 1), :] = _chunks_to_row(y8)

        _token_loop(tu, token)

    _with_tile_block(tile_hbm, bufs, sems, run)


def _peer_consts():
    row = jnp.arange(PEER_SLOTS * 8)
    slot = row // 8
    chunk = row % 8 // 2 + 4 * (row % 2)
    expand = (jnp.arange(PEER_SLOTS)[:, None] == slot[None, :]).astype(BF16)
    chunk8 = (jnp.arange(8)[:, None] == chunk[None, :]).astype(F32)
    return expand, expand.T, chunk8


def _pack_table_kernel(w_ref, o_ref):
    tm, d = w_ref.shape
    w = w_ref[...]
    lo = pltpu.bitcast(w[:, :d // 2].astype(BF16).astype(F32), jnp.uint32) >> 16
    hi = pltpu.bitcast(w[:, d // 2:].astype(BF16).astype(F32), jnp.uint32) & jnp.uint32(0xFFFF0000)
    words = pltpu.bitcast(hi | lo, jnp.int32)
    for j in range(EXPERT_WORD_ROWS):
        o_ref[pl.ds(j, tm, stride=EXPERT_WORD_ROWS), :] = words[:, j * LANES:(j + 1) * LANES]


def _peer_table(w):
    n, d = w.shape
    tm = min(PACK_ROWS, n)
    return pl.pallas_call(
        _pack_table_kernel,
        grid=(n // tm,),
        in_specs=[pl.BlockSpec((tm, d), lambda i: (i, 0))],
        out_specs=pl.BlockSpec((tm * EXPERT_WORD_ROWS, LANES), lambda i: (i, 0)),
        out_shape=jax.ShapeDtypeStruct((n * EXPERT_WORD_ROWS, LANES), jnp.int32),
        compiler_params=pltpu.CompilerParams(dimension_semantics=("parallel",)),
    )(w)


def _peer_specs(tu):
    smem = pl.BlockSpec(memory_space=pl.ANY)
    tok = pl.BlockSpec((tu, PEER_SLOTS), lambda i: (i, 0))
    full = lambda a: pl.BlockSpec(a.shape, lambda i: (0,) * a.ndim)
    resident = lambda a: pl.BlockSpec(a.shape, lambda i: (0,) * a.ndim,
                                      pipeline_mode=pl.Buffered(1))
    return smem, tok, full, resident


def _peer_u(tile, x1, ng, gate, tab):
    T, D = x1.shape
    tu = PEER_TOKEN_TILE
    _, group, chunk8 = _peer_consts()
    smem, tok, full, resident = _peer_specs(tu)
    return pl.pallas_call(
        _peer_u_kernel,
        grid=(T // tu,),
        in_specs=[smem, pl.BlockSpec((tu, D), lambda i: (i, 0)),
                  pl.BlockSpec((1, D), lambda i: (0, 0)), tok,
                  resident(tab), full(group), full(chunk8)],
        out_specs=tok,
        out_shape=jax.ShapeDtypeStruct((T, PEER_SLOTS), F32),
        scratch_shapes=[pltpu.VMEM((tu, D), F32), pltpu.VMEM((tu, PEER_SLOTS * 8), F32),
                        pltpu.SMEM((PEER_SLOTS, tu), jnp.int32), pltpu.SMEM((PEER_SLOTS, tu), jnp.int32),
                        pltpu.SemaphoreType.DMA((2,))],
        compiler_params=pltpu.CompilerParams(
            dimension_semantics=("arbitrary",), vmem_limit_bytes=VMEM_LIMIT),
    )(tile, x1, ng.reshape(1, D), gate, tab, group, chunk8)


def _peer_v(tile, w, x1, fg, tab):
    T, D = x1.shape
    tu = PEER_TOKEN_TILE
    expand, _, chunk8 = _peer_consts()
    smem, tok, full, resident = _peer_specs(tu)
    return pl.pallas_call(
        _peer_v_kernel,
        grid=(T // tu,),
        in_specs=[smem, tok, pl.BlockSpec((tu, D), lambda i: (i, 0)),
                  pl.BlockSpec((D // LANES, LANES), lambda i: (0, 0)),
                  resident(tab), full(expand), full(chunk8)],
        out_specs=pl.BlockSpec((tu, D), lambda i: (i, 0)),
        out_shape=jax.ShapeDtypeStruct((T, D), F32),
        scratch_shapes=[pltpu.VMEM((tu, PEER_SLOTS * 8), F32),
                        pltpu.SMEM((PEER_SLOTS, tu), jnp.int32), pltpu.SMEM((PEER_SLOTS, tu), jnp.int32),
                        pltpu.SemaphoreType.DMA((2,))],
        compiler_params=pltpu.CompilerParams(
            dimension_semantics=("arbitrary",), vmem_limit_bytes=VMEM_LIMIT),
    )(tile, w, x1, fg.reshape(D // LANES, LANES), tab, expand, chunk8)


def kernel(x, mem, norm_mix_g, w_in, hg_lb, hg_norm_g, sc_conv_w, mem_norm_g, w_mem_kv,
           w_branch, w_out, norm_ffn_g, peer_w_q, peer_sub_keys, peer_u, peer_v, final_norm_g):
    B, S, D = x.shape
    T = B * S
    assert w_in.shape[0] == 1, "single-layer block"
    n_main = 9 * BRANCH_WIDTH
    lb_table = jnp.cumsum(jax.nn.softmax(hg_lb.astype(F32), axis=0), axis=0)
    xf = x.reshape(T, D)
    memf = mem.reshape(B * N_MEM, D)
    w_in_b = w_in[0].astype(BF16)
    proj = _norm_matmul(xf, norm_mix_g[0], w_in_b[:, :n_main], 2048, 768)
    gates = _norm_matmul(xf, norm_mix_g[0], w_in_b[:, n_main:], 2048, 768, BF16)
    kv = _norm_matmul(memf, mem_norm_g[0], w_mem_kv[0].astype(BF16), 1024, 512)
    y_hg = _hgrn(proj, lb_table[0], hg_norm_g[0], B, S)
    x1 = _mix(xf, proj, gates, y_hg, kv, sc_conv_w[0], w_branch[0].astype(BF16),
              w_out[0].astype(BF16), B, S, 512)
    tile, gate = _route(x1, norm_ffn_g[0], peer_w_q[0].astype(BF16), peer_sub_keys[0], 256)
    w = _peer_u(tile, x1, norm_ffn_g[0], gate, _peer_table(peer_u[0]))
    return _peer_v(tile, w, x1, final_norm_g, _peer_table(peer_v[0])).reshape(B, S, D)
```

```python
import jax
import jax.numpy as jnp
from jax import lax
from jax.experimental import pallas as pl
from jax.experimental.pallas import tpu as pltpu

F32 = jnp.float32
BF16 = jnp.bfloat16

EPS = 1e-6
LANES = 128
SUBLANES = 8
HG_HEADS = 4
HG_CHUNK = 64
HG_CHUNKS_PER_ITER = 16
HG_STATE_UNROLL = 32
MX_HEADS = 4
BRANCH_WIDTH = 512
N_MEM = 256
PEER_HEADS = 8
PEER_NKEYS = 128
PEER_TOPK = 16
PEER_SLOTS = PEER_HEADS * PEER_TOPK
EXPERT_WORD_ROWS = 4
PACK_ROWS = 512
PEER_TOKEN_TILE = 128
TOKENS_PER_ITER = 128
VMEM_LIMIT = 48 * 1024 * 1024
PROJ_ROW_TILE, PROJ_COL_TILE = 2048, 768
KV_ROW_TILE, KV_COL_TILE = 1024, 512
MIX_TOKEN_TILE = 512
ROUTE_TOKEN_TILE = 256

NT_DIMS = (((1,), (1,)), ((), ()))
TN_DIMS = (((0,), (0,)), ((), ()))


def _rms(x, g):
    return x * lax.rsqrt(jnp.mean(x * x, axis=-1, keepdims=True) + EPS) * g


def _norm_matmul_kernel(x_ref, g_ref, w_ref, o_ref, h_ref):
    @pl.when(pl.program_id(1) == 0)
    def _():
        h_ref[...] = _rms(x_ref[...], g_ref[...]).astype(BF16)

    o_ref[...] = jnp.dot(h_ref[...], w_ref[...], preferred_element_type=F32).astype(o_ref.dtype)


def _norm_matmul(x, g, w_bf16, tm, tn, out_dtype=F32):
    T, D = x.shape
    N = w_bf16.shape[1]
    tm = min(tm, T)
    return pl.pallas_call(
        _norm_matmul_kernel,
        grid=(T // tm, N // tn),
        in_specs=[
            pl.BlockSpec((tm, D), lambda i, j: (i, 0)),
            pl.BlockSpec((1, D), lambda i, j: (0, 0)),
            pl.BlockSpec((D, tn), lambda i, j: (0, j)),
        ],
        out_specs=pl.BlockSpec((tm, tn), lambda i, j: (i, j)),
        out_shape=jax.ShapeDtypeStruct((T, N), out_dtype),
        scratch_shapes=[pltpu.VMEM((tm, D), BF16)],
        compiler_params=pltpu.CompilerParams(
            dimension_semantics=("parallel", "arbitrary"), vmem_limit_bytes=VMEM_LIMIT),
    )(x, g.reshape(1, D), w_bf16)


def _hgrn_kernel(q_ref, v_ref, zf_ref, zb_ref, g_ref, lb_ref, ng_ref, o_ref,
                 st_ref, out_ref, qs_ref, kk_ref, cum_ref, sc_ref, qe_ref, dst_ref, dec_ref):
    S = q_ref.shape[0]
    C = HG_CHUNK
    n = S // C
    row = lax.broadcasted_iota(jnp.int32, (C, C), 0)
    col = lax.broadcasted_iota(jnp.int32, (C, C), 1)
    keeps = (col <= row, col >= row)
    z_refs = (zf_ref, zb_ref)
    unroll = min(HG_CHUNKS_PER_ITER, n)
    state_unroll = min(HG_STATE_UNROLL, n)

    def chunks(fn):
        def body(i, carry):
            for j in range(unroll):
                sl = pl.ds(pl.multiple_of((i * unroll + j) * C, C), C)
                for d in range(2):
                    fn(i * unroll + j, sl, d)
            return carry
        lax.fori_loop(0, n // unroll, body, 0)

    qs_ref[...] = jax.nn.silu(q_ref[...])

    def decay_pass(c, sl, d):
        lb = lb_ref[d:d + 1, :]
        z = z_refs[d][sl, :]
        lf = jnp.log(lb + (1.0 - lb) * jax.nn.sigmoid(z))
        kk_ref[d, sl, :] = (1.0 - lb) * jax.nn.sigmoid(-z)
        tri = keeps[d].astype(BF16)
        hi = lf.astype(BF16)
        r1 = lf - hi.astype(F32)
        mid = r1.astype(BF16)
        lo = (r1 - mid.astype(F32)).astype(BF16)
        cum_ref[d, sl, :] = (jnp.dot(tri, hi, preferred_element_type=F32)
                             + jnp.dot(tri, mid, preferred_element_type=F32)
                             + jnp.dot(tri, lo, preferred_element_type=F32))

    def score_pass(c, sl, d):
        G = cum_ref[d, sl, :]
        kk = kk_ref[d, sl, :]
        q = qs_ref[sl, :]
        mid = C // 2 if d else C // 2 - 1
        last = 0 if d else C - 1
        g_mid = G[mid:mid + 1, :]
        g_last = G[last:last + 1, :]
        q_r = q * jnp.exp(G - g_mid)
        k_r = kk * jnp.exp(g_mid - G)
        scores = lax.dot_general(q_r, k_r, NT_DIMS, preferred_element_type=F32)
        sc_ref[d, sl, :] = jnp.where(keeps[d], scores, 0.0)
        qe_ref[d, sl, :] = q * jnp.exp(G)
        dst_ref[d, c] = lax.dot_general(v_ref[sl, :], kk * jnp.exp(g_last - G), TN_DIMS,
                                        preferred_element_type=F32)
        dec_ref[d, c] = jnp.broadcast_to(jnp.exp(g_last), (SUBLANES, LANES))

    def intra_pass(c, sl, d):
        out_ref[d, sl, :] = jnp.dot(sc_ref[d, sl, :], v_ref[sl, :], preferred_element_type=F32)

    chunks(decay_pass)
    chunks(score_pass)
    chunks(intra_pass)
    st_ref[...] = jnp.zeros_like(st_ref)

    def state_body(i, carry):
        for j in range(state_unroll):
            step = i * state_unroll + j
            for d, c in ((0, step), (1, n - 1 - step)):
                sl = pl.ds(pl.multiple_of(c * C, C), C)
                st = st_ref[d]
                out_ref[d, sl, :] += lax.dot_general(qe_ref[d, sl, :], st, NT_DIMS,
                                                     preferred_element_type=F32)
                st_ref[d] = st * dec_ref[d, c][0:1, :] + dst_ref[d, c]
        return carry

    lax.fori_loop(0, n // state_unroll, state_body, 0)
    o = out_ref[0] + out_ref[1]
    o = o * lax.rsqrt(jnp.mean(o * o, axis=-1, keepdims=True) + EPS)
    o_ref[...] = o * ng_ref[...] * jax.nn.silu(g_ref[...])


def _hgrn(proj, lb, ng, B, S):
    T = B * S
    H = HG_HEADS
    sec = lambda s: pl.BlockSpec((S, LANES), lambda b, h, s=s: (b, s * H + h))
    return pl.pallas_call(
        _hgrn_kernel,
        grid=(B, H),
        in_specs=[sec(0), sec(1), sec(2), sec(3), sec(4),
                  pl.BlockSpec((2, LANES), lambda b, h: (0, h)),
                  pl.BlockSpec((1, LANES), lambda b, h: (0, h))],
        out_specs=pl.BlockSpec((S, LANES), lambda b, h: (b, h)),
        out_shape=jax.ShapeDtypeStruct((T, H * LANES), F32),
        scratch_shapes=[pltpu.VMEM((2, LANES, LANES), F32),
                        pltpu.VMEM((2, S, LANES), F32),
                        pltpu.VMEM((S, LANES), F32),
                        pltpu.VMEM((2, S, LANES), F32),
                        pltpu.VMEM((2, S, LANES), F32),
                        pltpu.VMEM((2, S, HG_CHUNK), F32),
                        pltpu.VMEM((2, S, LANES), F32),
                        pltpu.VMEM((2, S // HG_CHUNK, LANES, LANES), F32),
                        pltpu.VMEM((2, S // HG_CHUNK, SUBLANES, LANES), F32)],
        compiler_params=pltpu.CompilerParams(
            dimension_semantics=("parallel", "parallel"), vmem_limit_bytes=VMEM_LIMIT),
    )(proj, proj, proj, proj, proj, lb, ng.reshape(1, -1))


def _mix_kernel(x_ref, sb_ref, sc_ref, sh_ref, scp_ref, shp_ref, scn_ref, shn_ref, mq_ref,
                gates_ref, yhg_ref, kv_ref, cw_ref, wb_ref, wo_ref, o_ref):
    i = pl.program_id(1)
    ni = pl.num_programs(1)
    ts = x_ref.shape[0]
    W = BRANCH_WIDTH
    u = sc_ref[...] * sh_ref[...]
    u_before = jnp.where(i > 0, scp_ref[SUBLANES - 1:, :] * shp_ref[SUBLANES - 1:, :], 0.0)
    u_after = jnp.where(i < ni - 1, scn_ref[0:1, :] * shn_ref[0:1, :], 0.0)
    rows = lax.broadcasted_iota(jnp.int32, (ts, W), 0)
    u_prev = jnp.where(rows == 0, u_before, pltpu.roll(u, 1, 0))
    u_next = jnp.where(rows == ts - 1, u_after, pltpu.roll(u, ts - 1, 0))
    cw = cw_ref[...]
    y_sc = sb_ref[...] * (cw[0:1, :] * u_prev + cw[1:2, :] * u + cw[2:3, :] * u_next)

    heads = []
    for h in range(MX_HEADS):
        qh = mq_ref[:, h * LANES:(h + 1) * LANES]
        kh = kv_ref[:, h * LANES:(h + 1) * LANES]
        vh = kv_ref[:, W + h * LANES:W + (h + 1) * LANES]
        logits = lax.dot_general(qh, kh, NT_DIMS, preferred_element_type=F32) * (LANES ** -0.5)
        e = jnp.exp(logits - jnp.max(logits, axis=-1, keepdims=True))
        p = e / jnp.sum(e, axis=-1, keepdims=True)
        heads.append(jnp.dot(p, vh, preferred_element_type=F32))
    y_mx = jnp.concatenate(heads, axis=1)

    D = x_ref.shape[1]
    merged = jnp.zeros((ts, D), F32)
    for n, y in enumerate((yhg_ref[...], y_sc, y_mx)):
        pr = jnp.dot(y.astype(BF16), wb_ref[n], preferred_element_type=F32)
        merged = merged + jax.nn.sigmoid(gates_ref[:, n * D:(n + 1) * D].astype(F32)) * pr
    out = jnp.dot(merged.astype(BF16), wo_ref[...], preferred_element_type=F32)
    o_ref[...] = x_ref[...] + out


def _mix(x, proj, gates, y_hg, kv, conv_w, wb_bf16, wo_bf16, B, S, ts):
    T, D = x.shape
    W = BRANCH_WIDTH
    ts = min(ts, S)
    nt = S // ts
    r8 = ts // SUBLANES
    tile = lambda b, i: b * nt + i
    sec = lambda s: pl.BlockSpec((ts, W), lambda b, i, s=s: (tile(b, i), s))
    before = lambda s: pl.BlockSpec(
        (SUBLANES, W), lambda b, i, s=s: (jnp.maximum(tile(b, i) * r8 - 1, 0), s))
    after = lambda s: pl.BlockSpec(
        (SUBLANES, W),
        lambda b, i, s=s: (jnp.minimum((tile(b, i) + 1) * r8, T // SUBLANES - 1), s))
    return pl.pallas_call(
        _mix_kernel,
        grid=(B, nt),
        in_specs=[
            pl.BlockSpec((ts, D), lambda b, i: (tile(b, i), 0)),
            sec(5), sec(6), sec(7), before(6), before(7), after(6), after(7), sec(8),
            pl.BlockSpec((ts, 3 * D), lambda b, i: (tile(b, i), 0)),
            pl.BlockSpec((ts, W), lambda b, i: (tile(b, i), 0)),
            pl.BlockSpec((N_MEM, 2 * W), lambda b, i: (b, 0)),
            pl.BlockSpec((3, W), lambda b, i: (0, 0)),
            pl.BlockSpec((3, W, D), lambda b, i: (0, 0, 0)),
            pl.BlockSpec((D, D), lambda b, i: (0, 0)),
        ],
        out_specs=pl.BlockSpec((ts, D), lambda b, i: (tile(b, i), 0)),
        out_shape=jax.ShapeDtypeStruct((T, D), F32),
        compiler_params=pltpu.CompilerParams(
            dimension_semantics=("parallel", "parallel"), vmem_limit_bytes=VMEM_LIMIT),
    )(x, proj, proj, proj, proj, proj, proj, proj, proj, gates, y_hg, kv, conv_w, wb_bf16, wo_bf16)


def _topk_rows(s, k, payload=None):
    n = s.shape[0]
    iota = lax.broadcasted_iota(jnp.int32, s.shape, 0)
    vals, ids = [], []
    for _ in range(k):
        m = jnp.max(s, axis=0, keepdims=True)
        am = jnp.min(jnp.where(s == m, iota, n), axis=0, keepdims=True)
        hit = iota == am
        vals.append(m)
        if payload is None:
            ids.append(am)
        else:
            ids.append(jnp.sum(jnp.where(hit, payload, 0), axis=0, keepdims=True))
        s = jnp.where(hit, -jnp.inf, s)
    return jnp.concatenate(vals, axis=0), jnp.concatenate(ids, axis=0)


def _pair_candidates(top_s, top_i):
    K = PEER_TOPK
    assert K == 16
    (s0, s1), (i0, i1) = top_s, top_i
    blocks_s = [s0[0:1, :] + s1]
    blocks_i = [i0[0:1, :] * PEER_NKEYS + i1]
    b_row = lax.broadcasted_iota(jnp.int32, (8, s0.shape[1]), 0)
    for a in range(1, 8):
        blocks_s.append(jnp.where(b_row < K // (a + 1), s0[a:a + 1, :] + s1[0:8, :], -jnp.inf))
        blocks_i.append(i0[a:a + 1, :] * PEER_NKEYS + i1[0:8, :])
    blocks_s.append(s0[8:16, :] + s1[0:1, :])
    blocks_i.append(i0[8:16, :] * PEER_NKEYS + i1[0:1, :])
    return jnp.concatenate(blocks_s, axis=0), jnp.concatenate(blocks_i, axis=0)


def _route_kernel(x_ref, g_ref, wq_ref, keys_ref, tile_ref, gate_ref):
    K = PEER_TOPK
    h = _rms(x_ref[...], g_ref[...])
    q = jnp.dot(h.astype(BF16), wq_ref[...], preferred_element_type=F32)
    idx_rows, gate_rows = [], []
    for hd in range(PEER_HEADS):
        top_s, top_i = [], []
        for p in range(2):
            c0 = (hd * 2 + p) * LANES
            s_t = lax.dot_general(keys_ref[hd, p], q[:, c0:c0 + LANES], NT_DIMS,
                                  preferred_element_type=F32)
            vs, is_ = _topk_rows(s_t, K)
            top_s.append(vs)
            top_i.append(is_)
        cand_s, cand_i = _pair_candidates(top_s, top_i)
        best_s, best_i = _topk_rows(cand_s, K, payload=cand_i)
        e = jnp.exp(best_s - best_s[0:1, :])
        gate_rows.append(e / jnp.sum(e, axis=0, keepdims=True))
        idx_rows.append(best_i)
    tile_t = jnp.concatenate(idx_rows, axis=0) * EXPERT_WORD_ROWS
    for j in range(tile_ref.shape[0]):
        tile_ref[j] = tile_t[:, j * PEER_TOKEN_TILE:(j + 1) * PEER_TOKEN_TILE]
    gate_ref[...] = jnp.concatenate(gate_rows, axis=0).T


def _route(x1, g, wq_bf16, keys, tt):
    T, D = x1.shape
    tt = min(tt, T)
    TU = PEER_TOKEN_TILE
    NQ = wq_bf16.shape[1]
    tok = lambda w: pl.BlockSpec((tt, w), lambda i: (i, 0))
    return pl.pallas_call(
        _route_kernel,
        grid=(T // tt,),
        in_specs=[tok(D), pl.BlockSpec((1, D), lambda i: (0, 0)),
                  pl.BlockSpec((D, NQ), lambda i: (0, 0)),
                  pl.BlockSpec(keys.shape, lambda i: (0, 0, 0, 0))],
        out_specs=[pl.BlockSpec((tt // TU, PEER_SLOTS, TU), lambda i: (i, 0, 0)),
                   tok(PEER_SLOTS)],
        out_shape=[jax.ShapeDtypeStruct((T // TU, PEER_SLOTS, TU), jnp.int32),
                   jax.ShapeDtypeStruct((T, PEER_SLOTS), F32)],
        compiler_params=pltpu.CompilerParams(
            dimension_semantics=("parallel",), vmem_limit_bytes=VMEM_LIMIT),
    )(x1, g.reshape(1, D), wq_bf16, keys)


def _gather_tiles(tile_s, t, tab_ref):
    def words(k):
        r = pl.multiple_of(tile_s.at[k][t], EXPERT_WORD_ROWS)
        return tab_ref[pl.ds(r, EXPERT_WORD_ROWS), :]

    return jnp.concatenate(
        [pltpu.bitcast(jnp.concatenate([words(k), words(k + 1)], axis=0), BF16)
         for k in range(0, PEER_SLOTS, 2)], axis=0)


def _tile_fetch(tile_hbm, step, bufs, sems, slot):
    return pltpu.make_async_copy(tile_hbm.at[step], bufs[slot], sems.at[slot])


def _with_tile_block(tile_hbm, bufs, sems, run):
    i = pl.program_id(0)
    for slot in range(2):
        @pl.when(i % 2 == slot)
        def _(slot=slot):
            @pl.when(i + 1 < pl.num_programs(0))
            def _():
                _tile_fetch(tile_hbm, i + 1, bufs, sems, 1 - slot).start()

            _tile_fetch(tile_hbm, i, bufs, sems, slot).wait()
            run(bufs[slot])


def _hi_lo(a):
    hi = a.astype(BF16)
    return hi, (a - hi.astype(F32)).astype(BF16)


def _row_to_chunks(row):
    return jnp.concatenate([row[:, c * LANES:(c + 1) * LANES]
                            for c in range(row.shape[1] // LANES)], axis=0)


def _chunks_to_row(x8):
    return jnp.concatenate([x8[c:c + 1, :] for c in range(x8.shape[0])], axis=1)


def _token_loop(tu, token):
    def body(i, carry):
        for j in range(TOKENS_PER_ITER):
            token(i * TOKENS_PER_ITER + j)
        return carry

    lax.fori_loop(0, tu // TOKENS_PER_ITER, body, 0)


def _peer_u_kernel(tile_hbm, x_ref, ng_ref, gate_ref, tab_ref, g_ref, cm_ref, w_ref,
                   h_ref, r_ref, tile_a, tile_b, sems):
    tu = gate_ref.shape[0]
    bufs = (tile_a, tile_b)

    @pl.when(pl.program_id(0) == 0)
    def _():
        _tile_fetch(tile_hbm, 0, bufs, sems, 0).start()

    h_ref[...] = _rms(x_ref[...], ng_ref[...])

    def run(tile_s):
        def token(t):
            m = _gather_tiles(tile_s, t, tab_ref)
            hi, lo = _hi_lo(_row_to_chunks(h_ref[pl.ds(t, 1), :]))
            r = lax.dot_general(jnp.concatenate([hi, lo], axis=0), m, NT_DIMS,
                                preferred_element_type=F32)
            r8 = (r[0:8, :] + r[8:16, :]) * cm_ref[...]
            r_ref[pl.ds(t, 1), :] = jnp.sum(r8, axis=0, keepdims=True)

        _token_loop(tu, token)

    _with_tile_block(tile_hbm, bufs, sems, run)
    hi, lo = _hi_lo(r_ref[...])
    a = (jnp.dot(hi, g_ref[...], preferred_element_type=F32)
         + jnp.dot(lo, g_ref[...], preferred_element_type=F32))
    w_ref[...] = gate_ref[...] * (0.5 * a * (1.0 + lax.erf(a * (2.0 ** -0.5))))


def _peer_v_kernel(tile_hbm, w_ref, x_ref, fg_ref, tab_ref, e_ref, cm_ref, o_ref, wexp_ref,
                   tile_a, tile_b, sems):
    tu = w_ref.shape[0]
    bufs = (tile_a, tile_b)

    @pl.when(pl.program_id(0) == 0)
    def _():
        _tile_fetch(tile_hbm, 0, bufs, sems, 0).start()

    hi, lo = _hi_lo(w_ref[...])
    wexp_ref[...] = (jnp.dot(hi, e_ref[...], preferred_element_type=F32)
                     + jnp.dot(lo, e_ref[...], preferred_element_type=F32))

    def run(tile_s):
        def token(t):
            m = _gather_tiles(tile_s, t, tab_ref)
            hi8, lo8 = _hi_lo(wexp_ref[pl.ds(t, 1), :] * cm_ref[...])
            o = jnp.dot(jnp.concatenate([hi8, lo8], axis=0), m, preferred_element_type=F32)
            x2 = _row_to_chunks(x_ref[pl.ds(t, 1), :]) + o[0:8, :] + o[8:16, :]
            ss = jnp.sum(jnp.sum(x2 * x2, axis=1, keepdims=True), axis=0, keepdims=True)
            y8 = x2 * lax.rsqrt(ss * (1.0 / x_ref.shape[1]) + EPS) * fg_ref[...]
            o_ref[pl.ds(t, 1), :] = _chunks_to_row(y8)

        _token_loop(tu, token)

    _with_tile_block(tile_hbm, bufs, sems, run)


def _peer_consts():
    row = jnp.arange(PEER_SLOTS * 8)
    slot = row // 8
    chunk = row % 8 // 2 + 4 * (row % 2)
    expand = (jnp.arange(PEER_SLOTS)[:, None] == slot[None, :]).astype(BF16)
    chunk8 = (jnp.arange(8)[:, None] == chunk[None, :]).astype(F32)
    return expand, expand.T, chunk8


def _pack_table_kernel(w_ref, o_ref):
    tm, d = w_ref.shape
    w = w_ref[...]
    lo = pltpu.bitcast(w[:, :d // 2].astype(BF16).astype(F32), jnp.uint32) >> 16
    hi = pltpu.bitcast(w[:, d // 2:].astype(BF16).astype(F32), jnp.uint32) & jnp.uint32(0xFFFF0000)
    words = pltpu.bitcast(hi | lo, jnp.int32)
    for j in range(EXPERT_WORD_ROWS):
        o_ref[pl.ds(j, tm, stride=EXPERT_WORD_ROWS), :] = words[:, j * LANES:(j + 1) * LANES]


def _peer_table(w):
    n, d = w.shape
    tm = min(PACK_ROWS, n)
    return pl.pallas_call(
        _pack_table_kernel,
        grid=(n // tm,),
        in_specs=[pl.BlockSpec((tm, d), lambda i: (i, 0))],
        out_specs=pl.BlockSpec((tm * EXPERT_WORD_ROWS, LANES), lambda i: (i, 0)),
        out_shape=jax.ShapeDtypeStruct((n * EXPERT_WORD_ROWS, LANES), jnp.int32),
        compiler_params=pltpu.CompilerParams(dimension_semantics=("parallel",)),
    )(w)


def _peer_specs(tu):
    smem = pl.BlockSpec(memory_space=pl.ANY)
    tok = pl.BlockSpec((tu, PEER_SLOTS), lambda i: (i, 0))
    full = lambda a: pl.BlockSpec(a.shape, lambda i: (0,) * a.ndim)
    resident = lambda a: pl.BlockSpec(a.shape, lambda i: (0,) * a.ndim,
                                      pipeline_mode=pl.Buffered(1))
    return smem, tok, full, resident


def _peer_u(tile, x1, ng, gate, tab):
    T, D = x1.shape
    tu = PEER_TOKEN_TILE
    _, group, chunk8 = _peer_consts()
    smem, tok, full, resident = _peer_specs(tu)
    return pl.pallas_call(
        _peer_u_kernel,
        grid=(T // tu,),
        in_specs=[smem, pl.BlockSpec((tu, D), lambda i: (i, 0)),
                  pl.BlockSpec((1, D), lambda i: (0, 0)), tok,
                  resident(tab), full(group), full(chunk8)],
        out_specs=tok,
        out_shape=jax.ShapeDtypeStruct((T, PEER_SLOTS), F32),
        scratch_shapes=[pltpu.VMEM((tu, D), F32), pltpu.VMEM((tu, PEER_SLOTS * 8), F32),
                        pltpu.SMEM((PEER_SLOTS, tu), jnp.int32), pltpu.SMEM((PEER_SLOTS, tu), jnp.int32),
                        pltpu.SemaphoreType.DMA((2,))],
        compiler_params=pltpu.CompilerParams(
            dimension_semantics=("arbitrary",), vmem_limit_bytes=VMEM_LIMIT),
    )(tile, x1, ng.reshape(1, D), gate, tab, group, chunk8)


def _peer_v(tile, w, x1, fg, tab):
    T, D = x1.shape
    tu = PEER_TOKEN_TILE
    expand, _, chunk8 = _peer_consts()
    smem, tok, full, resident = _peer_specs(tu)
    return pl.pallas_call(
        _peer_v_kernel,
        grid=(T // tu,),
        in_specs=[smem, tok, pl.BlockSpec((tu, D), lambda i: (i, 0)),
                  pl.BlockSpec((D // LANES, LANES), lambda i: (0, 0)),
                  resident(tab), full(expand), full(chunk8)],
        out_specs=pl.BlockSpec((tu, D), lambda i: (i, 0)),
        out_shape=jax.ShapeDtypeStruct((T, D), F32),
        scratch_shapes=[pltpu.VMEM((tu, PEER_SLOTS * 8), F32),
                        pltpu.SMEM((PEER_SLOTS, tu), jnp.int32), pltpu.SMEM((PEER_SLOTS, tu), jnp.int32),
                        pltpu.SemaphoreType.DMA((2,))],
        compiler_params=pltpu.CompilerParams(
            dimension_semantics=("arbitrary",), vmem_limit_bytes=VMEM_LIMIT),
    )(tile, w, x1, fg.reshape(D // LANES, LANES), tab, expand, chunk8)


def kernel(x, mem, norm_mix_g, w_in, hg_lb, hg_norm_g, sc_conv_w, mem_norm_g, w_mem_kv,
           w_branch, w_out, norm_ffn_g, peer_w_q, peer_sub_keys, peer_u, peer_v, final_norm_g):
    B, S, D = x.shape
    T = B * S
    assert w_in.shape[0] == 1, "single-layer block"
    n_main = 9 * BRANCH_WIDTH
    lb_table = jnp.cumsum(jax.nn.softmax(hg_lb.astype(F32), axis=0), axis=0)
    xf = x.reshape(T, D)
    memf = mem.reshape(B * N_MEM, D)
    w_in_b = w_in[0].astype(BF16)
    proj = _norm_matmul(xf, norm_mix_g[0], w_in_b[:, :n_main], PROJ_ROW_TILE, PROJ_COL_TILE)
    gates = _norm_matmul(xf, norm_mix_g[0], w_in_b[:, n_main:], PROJ_ROW_TILE, PROJ_COL_TILE, BF16)
    kv = _norm_matmul(memf, mem_norm_g[0], w_mem_kv[0].astype(BF16), KV_ROW_TILE, KV_COL_TILE)
    y_hg = _hgrn(proj, lb_table[0], hg_norm_g[0], B, S)
    x1 = _mix(xf, proj, gates, y_hg, kv, sc_conv_w[0], w_branch[0].astype(BF16),
              w_out[0].astype(BF16), B, S, MIX_TOKEN_TILE)
    tile, gate = _route(x1, norm_ffn_g[0], peer_w_q[0].astype(BF16), peer_sub_keys[0],
                        ROUTE_TOKEN_TILE)
    w = _peer_u(tile, x1, norm_ffn_g[0], gate, _peer_table(peer_u[0]))
    return _peer_v(tile, w, x1, final_norm_g, _peer_table(peer_v[0])).reshape(B, S, D)
```

```python
import jax
import jax.numpy as jnp
from jax import lax
from jax.experimental import pallas as pl
from jax.experimental.pallas import tpu as pltpu

F32 = jnp.float32
BF16 = jnp.bfloat16

EPS = 1e-6
LANES = 128
SUBLANES = 8
HG_HEADS = 4
HG_CHUNK = 64
HG_CHUNKS_PER_ITER = 16
HG_STATE_UNROLL = 32
MX_HEADS = 4
BRANCH_WIDTH = 512
N_MEM = 256
PEER_HEADS = 8
PEER_NKEYS = 128
PEER_TOPK = 16
PEER_SLOTS = PEER_HEADS * PEER_TOPK
EXPERT_WORD_ROWS = 4
PACK_ROWS = 512
PEER_TOKEN_TILE = 128
TOKENS_PER_ITER = 128
VMEM_LIMIT = 48 * 1024 * 1024
PROJ_ROW_TILE, PROJ_COL_TILE = 2048, 768
KV_ROW_TILE, KV_COL_TILE = 1024, 512
MIX_TOKEN_TILE = 512
ROUTE_TOKEN_TILE = 256

NT_DIMS = (((1,), (1,)), ((), ()))
TN_DIMS = (((0,), (0,)), ((), ()))


def _rms(x, g):
    return x * lax.rsqrt(jnp.mean(x * x, axis=-1, keepdims=True) + EPS) * g


def _norm_matmul_kernel(x_ref, g_ref, w_ref, o_ref, h_ref):
    @pl.when(pl.program_id(1) == 0)
    def _():
        h_ref[...] = _rms(x_ref[...], g_ref[...]).astype(BF16)

    o_ref[...] = jnp.dot(h_ref[...], w_ref[...], preferred_element_type=F32).astype(o_ref.dtype)


def _norm_matmul(x, g, w_bf16, tm, tn, out_dtype=F32):
    T, D = x.shape
    N = w_bf16.shape[1]
    tm = min(tm, T)
    return pl.pallas_call(
        _norm_matmul_kernel,
        grid=(T // tm, N // tn),
        in_specs=[
            pl.BlockSpec((tm, D), lambda i, j: (i, 0)),
            pl.BlockSpec((1, D), lambda i, j: (0, 0)),
            pl.BlockSpec((D, tn), lambda i, j: (0, j)),
        ],
        out_specs=pl.BlockSpec((tm, tn), lambda i, j: (i, j)),
        out_shape=jax.ShapeDtypeStruct((T, N), out_dtype),
        scratch_shapes=[pltpu.VMEM((tm, D), BF16)],
        compiler_params=pltpu.CompilerParams(
            dimension_semantics=("parallel", "arbitrary"), vmem_limit_bytes=VMEM_LIMIT),
    )(x, g.reshape(1, D), w_bf16)


def _hgrn_kernel(q_ref, v_ref, zf_ref, zb_ref, g_ref, lb_ref, ng_ref, o_ref,
                 st_ref, out_ref, qs_ref, kk_ref, cum_ref, sc_ref, qe_ref, dst_ref, dec_ref):
    S = q_ref.shape[0]
    C = HG_CHUNK
    n = S // C
    row = lax.broadcasted_iota(jnp.int32, (C, C), 0)
    col = lax.broadcasted_iota(jnp.int32, (C, C), 1)
    keeps = (col <= row, col >= row)
    z_refs = (zf_ref, zb_ref)
    unroll = min(HG_CHUNKS_PER_ITER, n)
    state_unroll = min(HG_STATE_UNROLL, n)

    def chunks(fn):
        def body(i, carry):
            for j in range(unroll):
                sl = pl.ds(pl.multiple_of((i * unroll + j) * C, C), C)
                for d in range(2):
                    fn(i * unroll + j, sl, d)
            return carry
        lax.fori_loop(0, n // unroll, body, 0)

    qs_ref[...] = jax.nn.silu(q_ref[...])

    def decay_pass(c, sl, d):
        lb = lb_ref[d:d + 1, :]
        z = z_refs[d][sl, :]
        lf = jnp.log(lb + (1.0 - lb) * jax.nn.sigmoid(z))
        kk_ref[d, sl, :] = (1.0 - lb) * jax.nn.sigmoid(-z)
        tri = keeps[d].astype(BF16)
        hi = lf.astype(BF16)
        r1 = lf - hi.astype(F32)
        mid = r1.astype(BF16)
        lo = (r1 - mid.astype(F32)).astype(BF16)
        cum_ref[d, sl, :] = (jnp.dot(tri, hi, preferred_element_type=F32)
                             + jnp.dot(tri, mid, preferred_element_type=F32)
                             + jnp.dot(tri, lo, preferred_element_type=F32))

    def score_pass(c, sl, d):
        G = cum_ref[d, sl, :]
        kk = kk_ref[d, sl, :]
        q = qs_ref[sl, :]
        mid = C // 2 if d else C // 2 - 1
        last = 0 if d else C - 1
        g_mid = G[mid:mid + 1, :]
        g_last = G[last:last + 1, :]
        q_r = q * jnp.exp(G - g_mid)
        k_r = kk * jnp.exp(g_mid - G)
        scores = lax.dot_general(q_r, k_r, NT_DIMS, preferred_element_type=F32)
        sc_ref[d, sl, :] = jnp.where(keeps[d], scores, 0.0)
        qe_ref[d, sl, :] = q * jnp.exp(G)
        dst_ref[d, c] = lax.dot_general(v_ref[sl, :], kk * jnp.exp(g_last - G), TN_DIMS,
                                        preferred_element_type=F32)
        dec_ref[d, c] = jnp.broadcast_to(jnp.exp(g_last), (SUBLANES, LANES))

    def intra_pass(c, sl, d):
        out_ref[d, sl, :] = jnp.dot(sc_ref[d, sl, :], v_ref[sl, :], preferred_element_type=F32)

    chunks(decay_pass)
    chunks(score_pass)
    chunks(intra_pass)
    st_ref[...] = jnp.zeros_like(st_ref)

    def state_body(i, carry):
        for j in range(state_unroll):
            step = i * state_unroll + j
            for d, c in ((0, step), (1, n - 1 - step)):
                sl = pl.ds(pl.multiple_of(c * C, C), C)
                st = st_ref[d]
                out_ref[d, sl, :] += lax.dot_general(qe_ref[d, sl, :], st, NT_DIMS,
                                                     preferred_element_type=F32)
                st_ref[d] = st * dec_ref[d, c][0:1, :] + dst_ref[d, c]
        return carry

    lax.fori_loop(0, n // state_unroll, state_body, 0)
    o = out_ref[0] + out_ref[1]
    o = o * lax.rsqrt(jnp.mean(o * o, axis=-1, keepdims=True) + EPS)
    o_ref[...] = o * ng_ref[...] * jax.nn.silu(g_ref[...])


def _hgrn(proj, lb, ng, B, S):
    T = B * S
    H = HG_HEADS
    sec = lambda s: pl.BlockSpec((S, LANES), lambda b, h, s=s: (b, s * H + h))
    return pl.pallas_call(
        _hgrn_kernel,
        grid=(B, H),
        in_specs=[sec(0), sec(1), sec(2), sec(3), sec(4),
                  pl.BlockSpec((2, LANES), lambda b, h: (0, h)),
                  pl.BlockSpec((1, LANES), lambda b, h: (0, h))],
        out_specs=pl.BlockSpec((S, LANES), lambda b, h: (b, h)),
        out_shape=jax.ShapeDtypeStruct((T, H * LANES), F32),
        scratch_shapes=[pltpu.VMEM((2, LANES, LANES), F32),
                        pltpu.VMEM((2, S, LANES), F32),
                        pltpu.VMEM((S, LANES), F32),
                        pltpu.VMEM((2, S, LANES), F32),
                        pltpu.VMEM((2, S, LANES), F32),
                        pltpu.VMEM((2, S, HG_CHUNK), F32),
                        pltpu.VMEM((2, S, LANES), F32),
                        pltpu.VMEM((2, S // HG_CHUNK, LANES, LANES), F32),
                        pltpu.VMEM((2, S // HG_CHUNK, SUBLANES, LANES), F32)],
        compiler_params=pltpu.CompilerParams(
            dimension_semantics=("parallel", "parallel"), vmem_limit_bytes=VMEM_LIMIT),
    )(proj, proj, proj, proj, proj, lb, ng.reshape(1, -1))


def _mix_kernel(x_ref, sb_ref, sc_ref, sh_ref, scp_ref, shp_ref, scn_ref, shn_ref, mq_ref,
                gates_ref, yhg_ref, kv_ref, cw_ref, wb_ref, wo_ref, o_ref):
    i = pl.program_id(1)
    ni = pl.num_programs(1)
    ts = x_ref.shape[0]
    W = BRANCH_WIDTH
    u = sc_ref[...] * sh_ref[...]
    u_before = jnp.where(i > 0, scp_ref[SUBLANES - 1:, :] * shp_ref[SUBLANES - 1:, :], 0.0)
    u_after = jnp.where(i < ni - 1, scn_ref[0:1, :] * shn_ref[0:1, :], 0.0)
    rows = lax.broadcasted_iota(jnp.int32, (ts, W), 0)
    u_prev = jnp.where(rows == 0, u_before, pltpu.roll(u, 1, 0))
    u_next = jnp.where(rows == ts - 1, u_after, pltpu.roll(u, ts - 1, 0))
    cw = cw_ref[...]
    y_sc = sb_ref[...] * (cw[0:1, :] * u_prev + cw[1:2, :] * u + cw[2:3, :] * u_next)

    heads = []
    for h in range(MX_HEADS):
        qh = mq_ref[:, h * LANES:(h + 1) * LANES]
        kh = kv_ref[:, h * LANES:(h + 1) * LANES]
        vh = kv_ref[:, W + h * LANES:W + (h + 1) * LANES]
        logits = lax.dot_general(qh, kh, NT_DIMS, preferred_element_type=F32) * (LANES ** -0.5)
        e = jnp.exp(logits - jnp.max(logits, axis=-1, keepdims=True))
        p = e / jnp.sum(e, axis=-1, keepdims=True)
        heads.append(jnp.dot(p, vh, preferred_element_type=F32))
    y_mx = jnp.concatenate(heads, axis=1)

    D = x_ref.shape[1]
    merged = jnp.zeros((ts, D), F32)
    for n, y in enumerate((yhg_ref[...], y_sc, y_mx)):
        pr = jnp.dot(y.astype(BF16), wb_ref[n], preferred_element_type=F32)
        merged = merged + jax.nn.sigmoid(gates_ref[:, n * D:(n + 1) * D].astype(F32)) * pr
    out = jnp.dot(merged.astype(BF16), wo_ref[...], preferred_element_type=F32)
    o_ref[...] = x_ref[...] + out


def _mix(x, proj, gates, y_hg, kv, conv_w, wb_bf16, wo_bf16, B, S, ts):
    T, D = x.shape
    W = BRANCH_WIDTH
    ts = min(ts, S)
    nt = S // ts
    r8 = ts // SUBLANES
    tile = lambda b, i: b * nt + i
    sec = lambda s: pl.BlockSpec((ts, W), lambda b, i, s=s: (tile(b, i), s))
    before = lambda s: pl.BlockSpec(
        (SUBLANES, W), lambda b, i, s=s: (jnp.maximum(tile(b, i) * r8 - 1, 0), s))
    after = lambda s: pl.BlockSpec(
        (SUBLANES, W),
        lambda b, i, s=s: (jnp.minimum((tile(b, i) + 1) * r8, T // SUBLANES - 1), s))
    return pl.pallas_call(
        _mix_kernel,
        grid=(B, nt),
        in_specs=[
            pl.BlockSpec((ts, D), lambda b, i: (tile(b, i), 0)),
            sec(5), sec(6), sec(7), before(6), before(7), after(6), after(7), sec(8),
            pl.BlockSpec((ts, 3 * D), lambda b, i: (tile(b, i), 0)),
            pl.BlockSpec((ts, W), lambda b, i: (tile(b, i), 0)),
            pl.BlockSpec((N_MEM, 2 * W), lambda b, i: (b, 0)),
            pl.BlockSpec((3, W), lambda b, i: (0, 0)),
            pl.BlockSpec((3, W, D), lambda b, i: (0, 0, 0)),
            pl.BlockSpec((D, D), lambda b, i: (0, 0)),
        ],
        out_specs=pl.BlockSpec((ts, D), lambda b, i: (tile(b, i), 0)),
        out_shape=jax.ShapeDtypeStruct((T, D), F32),
        compiler_params=pltpu.CompilerParams(
            dimension_semantics=("parallel", "parallel"), vmem_limit_bytes=VMEM_LIMIT),
    )(x, proj, proj, proj, proj, proj, proj, proj, proj, gates, y_hg, kv, conv_w, wb_bf16, wo_bf16)


def _topk_rows(s, k, payload=None):
    n = s.shape[0]
    iota = lax.broadcasted_iota(jnp.int32, s.shape, 0).astype(F32)
    vals, ids = [], []
    for _ in range(k):
        m = jnp.max(s, axis=0, keepdims=True)
        am = jnp.min(jnp.where(s == m, iota, float(n)), axis=0, keepdims=True)
        hit = iota == am
        vals.append(m)
        if payload is None:
            ids.append(am)
        else:
            ids.append(jnp.sum(jnp.where(hit, payload, 0), axis=0, keepdims=True))
        s = jnp.where(hit, -jnp.inf, s)
    ids = jnp.concatenate(ids, axis=0)
    return jnp.concatenate(vals, axis=0), ids.astype(jnp.int32)


def _pair_candidates(top_s, top_i):
    K = PEER_TOPK
    assert K == 16
    (s0, s1), (i0, i1) = top_s, top_i
    blocks_s = [s0[0:1, :] + s1]
    blocks_i = [i0[0:1, :] * PEER_NKEYS + i1]
    b_row = lax.broadcasted_iota(jnp.int32, (8, s0.shape[1]), 0)
    for a in range(1, 8):
        blocks_s.append(jnp.where(b_row < K // (a + 1), s0[a:a + 1, :] + s1[0:8, :], -jnp.inf))
        blocks_i.append(i0[a:a + 1, :] * PEER_NKEYS + i1[0:8, :])
    blocks_s.append(s0[8:16, :] + s1[0:1, :])
    blocks_i.append(i0[8:16, :] * PEER_NKEYS + i1[0:1, :])
    return jnp.concatenate(blocks_s, axis=0), jnp.concatenate(blocks_i, axis=0)


def _route_kernel(x_ref, g_ref, wq_ref, keys_ref, tile_ref, gate_ref):
    K = PEER_TOPK
    h = _rms(x_ref[...], g_ref[...])
    q = jnp.dot(h.astype(BF16), wq_ref[...], preferred_element_type=F32)
    idx_rows, gate_rows = [], []
    for hd in range(PEER_HEADS):
        top_s, top_i = [], []
        for p in range(2):
            c0 = (hd * 2 + p) * LANES
            s_t = lax.dot_general(keys_ref[hd, p], q[:, c0:c0 + LANES], NT_DIMS,
                                  preferred_element_type=F32)
            vs, is_ = _topk_rows(s_t, K)
            top_s.append(vs)
            top_i.append(is_)
        cand_s, cand_i = _pair_candidates(top_s, top_i)
        best_s, best_i = _topk_rows(cand_s, K, payload=cand_i)
        e = jnp.exp(best_s - best_s[0:1, :])
        gate_rows.append(e / jnp.sum(e, axis=0, keepdims=True))
        idx_rows.append(best_i)
    tile_t = jnp.concatenate(idx_rows, axis=0) * EXPERT_WORD_ROWS
    for j in range(tile_ref.shape[0]):
        tile_ref[j] = tile_t[:, j * PEER_TOKEN_TILE:(j + 1) * PEER_TOKEN_TILE]
    gate_ref[...] = jnp.concatenate(gate_rows, axis=0).T


def _route(x1, g, wq_bf16, keys, tt):
    T, D = x1.shape
    tt = min(tt, T)
    TU = PEER_TOKEN_TILE
    NQ = wq_bf16.shape[1]
    tok = lambda w: pl.BlockSpec((tt, w), lambda i: (i, 0))
    return pl.pallas_call(
        _route_kernel,
        grid=(T // tt,),
        in_specs=[tok(D), pl.BlockSpec((1, D), lambda i: (0, 0)),
                  pl.BlockSpec((D, NQ), lambda i: (0, 0)),
                  pl.BlockSpec(keys.shape, lambda i: (0, 0, 0, 0))],
        out_specs=[pl.BlockSpec((tt // TU, PEER_SLOTS, TU), lambda i: (i, 0, 0)),
                   tok(PEER_SLOTS)],
        out_shape=[jax.ShapeDtypeStruct((T // TU, PEER_SLOTS, TU), jnp.int32),
                   jax.ShapeDtypeStruct((T, PEER_SLOTS), F32)],
        compiler_params=pltpu.CompilerParams(
            dimension_semantics=("parallel",), vmem_limit_bytes=VMEM_LIMIT),
    )(x1, g.reshape(1, D), wq_bf16, keys)


def _gather_tiles(tile_s, t, tab_ref):
    def words(k):
        r = pl.multiple_of(tile_s.at[k][t], EXPERT_WORD_ROWS)
        return tab_ref[pl.ds(r, EXPERT_WORD_ROWS), :]

    return jnp.concatenate(
        [pltpu.bitcast(jnp.concatenate([words(k), words(k + 1)], axis=0), BF16)
         for k in range(0, PEER_SLOTS, 2)], axis=0)


def _tile_fetch(tile_hbm, step, bufs, sems, slot):
    return pltpu.make_async_copy(tile_hbm.at[step], bufs[slot], sems.at[slot])


def _with_tile_block(tile_hbm, bufs, sems, run):
    i = pl.program_id(0)
    for slot in range(2):
        @pl.when(i % 2 == slot)
        def _(slot=slot):
            @pl.when(i + 1 < pl.num_programs(0))
            def _():
                _tile_fetch(tile_hbm, i + 1, bufs, sems, 1 - slot).start()

            _tile_fetch(tile_hbm, i, bufs, sems, slot).wait()
            run(bufs[slot])


def _hi_lo(a):
    hi = a.astype(BF16)
    return hi, (a - hi.astype(F32)).astype(BF16)


def _row_to_chunks(row):
    return jnp.concatenate([row[:, c * LANES:(c + 1) * LANES]
                            for c in range(row.shape[1] // LANES)], axis=0)


def _chunks_to_row(x8):
    return jnp.concatenate([x8[c:c + 1, :] for c in range(x8.shape[0])], axis=1)


def _token_loop(tu, token):
    def body(i, carry):
        for j in range(TOKENS_PER_ITER):
            token(i * TOKENS_PER_ITER + j)
        return carry

    lax.fori_loop(0, tu // TOKENS_PER_ITER, body, 0)


def _peer_u_kernel(tile_hbm, x_ref, ng_ref, gate_ref, tab_ref, g_ref, cm_ref, w_ref,
                   h_ref, r_ref, tile_a, tile_b, sems):
    tu = gate_ref.shape[0]
    bufs = (tile_a, tile_b)

    @pl.when(pl.program_id(0) == 0)
    def _():
        _tile_fetch(tile_hbm, 0, bufs, sems, 0).start()

    h_ref[...] = _rms(x_ref[...], ng_ref[...])

    def run(tile_s):
        def token(t):
            m = _gather_tiles(tile_s, t, tab_ref)
            hi, lo = _hi_lo(_row_to_chunks(h_ref[pl.ds(t, 1), :]))
            r = lax.dot_general(jnp.concatenate([hi, lo], axis=0), m, NT_DIMS,
                                preferred_element_type=F32)
            r8 = (r[0:8, :] + r[8:16, :]) * cm_ref[...]
            r_ref[pl.ds(t, 1), :] = jnp.sum(r8, axis=0, keepdims=True)

        _token_loop(tu, token)

    _with_tile_block(tile_hbm, bufs, sems, run)
    hi, lo = _hi_lo(r_ref[...])
    a = (jnp.dot(hi, g_ref[...], preferred_element_type=F32)
         + jnp.dot(lo, g_ref[...], preferred_element_type=F32))
    w_ref[...] = gate_ref[...] * (0.5 * a * (1.0 + lax.erf(a * (2.0 ** -0.5))))


def _peer_v_kernel(tile_hbm, w_ref, x_ref, fg_ref, tab_ref, e_ref, cm_ref, o_ref, wexp_ref,
                   tile_a, tile_b, sems):
    tu = w_ref.shape[0]
    bufs = (tile_a, tile_b)

    @pl.when(pl.program_id(0) == 0)
    def _():
        _tile_fetch(tile_hbm, 0, bufs, sems, 0).start()

    hi, lo = _hi_lo(w_ref[...])
    wexp_ref[...] = (jnp.dot(hi, e_ref[...], preferred_element_type=F32)
                     + jnp.dot(lo, e_ref[...], preferred_element_type=F32))

    def run(tile_s):
        def token(t):
            m = _gather_tiles(tile_s, t, tab_ref)
            hi8, lo8 = _hi_lo(wexp_ref[pl.ds(t, 1), :] * cm_ref[...])
            o = jnp.dot(jnp.concatenate([hi8, lo8], axis=0), m, preferred_element_type=F32)
            x2 = _row_to_chunks(x_ref[pl.ds(t, 1), :]) + o[0:8, :] + o[8:16, :]
            ss = jnp.sum(jnp.sum(x2 * x2, axis=1, keepdims=True), axis=0, keepdims=True)
            y8 = x2 * lax.rsqrt(ss * (1.0 / x_ref.shape[1]) + EPS) * fg_ref[...]
            o_ref[pl.ds(t, 1), :] = _chunks_to_row(y8)

        _token_loop(tu, token)

    _with_tile_block(tile_hbm, bufs, sems, run)


def _peer_consts():
    row = jnp.arange(PEER_SLOTS * 8)
    slot = row // 8
    chunk = row % 8 // 2 + 4 * (row % 2)
    expand = (jnp.arange(PEER_SLOTS)[:, None] == slot[None, :]).astype(BF16)
    chunk8 = (jnp.arange(8)[:, None] == chunk[None, :]).astype(F32)
    return expand, expand.T, chunk8


def _pack_table_kernel(w_ref, o_ref):
    tm, d = w_ref.shape
    w = w_ref[...]
    lo = pltpu.bitcast(w[:, :d // 2].astype(BF16).astype(F32), jnp.uint32) >> 16
    hi = pltpu.bitcast(w[:, d // 2:].astype(BF16).astype(F32), jnp.uint32) & jnp.uint32(0xFFFF0000)
    words = pltpu.bitcast(hi | lo, jnp.int32)
    for j in range(EXPERT_WORD_ROWS):
        o_ref[pl.ds(j, tm, stride=EXPERT_WORD_ROWS), :] = words[:, j * LANES:(j + 1) * LANES]


def _peer_table(w):
    n, d = w.shape
    tm = min(PACK_ROWS, n)
    return pl.pallas_call(
        _pack_table_kernel,
        grid=(n // tm,),
        in_specs=[pl.BlockSpec((tm, d), lambda i: (i, 0))],
        out_specs=pl.BlockSpec((tm * EXPERT_WORD_ROWS, LANES), lambda i: (i, 0)),
        out_shape=jax.ShapeDtypeStruct((n * EXPERT_WORD_ROWS, LANES), jnp.int32),
        compiler_params=pltpu.CompilerParams(dimension_semantics=("parallel",)),
    )(w)


def _peer_specs(tu):
    smem = pl.BlockSpec(memory_space=pl.ANY)
    tok = pl.BlockSpec((tu, PEER_SLOTS), lambda i: (i, 0))
    full = lambda a: pl.BlockSpec(a.shape, lambda i: (0,) * a.ndim)
    resident = lambda a: pl.BlockSpec(a.shape, lambda i: (0,) * a.ndim,
                                      pipeline_mode=pl.Buffered(1))
    return smem, tok, full, resident


def _peer_u(tile, x1, ng, gate, tab):
    T, D = x1.shape
    tu = PEER_TOKEN_TILE
    _, group, chunk8 = _peer_consts()
    smem, tok, full, resident = _peer_specs(tu)
    return pl.pallas_call(
        _peer_u_kernel,
        grid=(T // tu,),
        in_specs=[smem, pl.BlockSpec((tu, D), lambda i: (i, 0)),
                  pl.BlockSpec((1, D), lambda i: (0, 0)), tok,
                  resident(tab), full(group), full(chunk8)],
        out_specs=tok,
        out_shape=jax.ShapeDtypeStruct((T, PEER_SLOTS), F32),
        scratch_shapes=[pltpu.VMEM((tu, D), F32), pltpu.VMEM((tu, PEER_SLOTS * 8), F32),
                        pltpu.SMEM((PEER_SLOTS, tu), jnp.int32), pltpu.SMEM((PEER_SLOTS, tu), jnp.int32),
                        pltpu.SemaphoreType.DMA((2,))],
        compiler_params=pltpu.CompilerParams(
            dimension_semantics=("arbitrary",), vmem_limit_bytes=VMEM_LIMIT),
    )(tile, x1, ng.reshape(1, D), gate, tab, group, chunk8)


def _peer_v(tile, w, x1, fg, tab):
    T, D = x1.shape
    tu = PEER_TOKEN_TILE
    expand, _, chunk8 = _peer_consts()
    smem, tok, full, resident = _peer_specs(tu)
    return pl.pallas_call(
        _peer_v_kernel,
        grid=(T // tu,),
        in_specs=[smem, tok, pl.BlockSpec((tu, D), lambda i: (i, 0)),
                  pl.BlockSpec((D // LANES, LANES), lambda i: (0, 0)),
                  resident(tab), full(expand), full(chunk8)],
        out_specs=pl.BlockSpec((tu, D), lambda i: (i, 0)),
        out_shape=jax.ShapeDtypeStruct((T, D), F32),
        scratch_shapes=[pltpu.VMEM((tu, PEER_SLOTS * 8), F32),
                        pltpu.SMEM((PEER_SLOTS, tu), jnp.int32), pltpu.SMEM((PEER_SLOTS, tu), jnp.int32),
                        pltpu.SemaphoreType.DMA((2,))],
        compiler_params=pltpu.CompilerParams(
            dimension_semantics=("arbitrary",), vmem_limit_bytes=VMEM_LIMIT),
    )(tile, w, x1, fg.reshape(D // LANES, LANES), tab, expand, chunk8)


def kernel(x, mem, norm_mix_g, w_in, hg_lb, hg_norm_g, sc_conv_w, mem_norm_g, w_mem_kv,
           w_branch, w_out, norm_ffn_g, peer_w_q, peer_sub_keys, peer_u, peer_v, final_norm_g):
    B, S, D = x.shape
    T = B * S
    assert w_in.shape[0] == 1, "single-layer block"
    n_main = 9 * BRANCH_WIDTH
    lb_table = jnp.cumsum(jax.nn.softmax(hg_lb.astype(F32), axis=0), axis=0)
    xf = x.reshape(T, D)
    memf = mem.reshape(B * N_MEM, D)
    w_in_b = w_in[0].astype(BF16)
    proj = _norm_matmul(xf, norm_mix_g[0], w_in_b[:, :n_main], PROJ_ROW_TILE, PROJ_COL_TILE)
    gates = _norm_matmul(xf, norm_mix_g[0], w_in_b[:, n_main:], PROJ_ROW_TILE, PROJ_COL_TILE, BF16)
    kv = _norm_matmul(memf, mem_norm_g[0], w_mem_kv[0].astype(BF16), KV_ROW_TILE, KV_COL_TILE)
    y_hg = _hgrn(proj, lb_table[0], hg_norm_g[0], B, S)
    x1 = _mix(xf, proj, gates, y_hg, kv, sc_conv_w[0], w_branch[0].astype(BF16),
              w_out[0].astype(BF16), B, S, MIX_TOKEN_TILE)
    tile, gate = _route(x1, norm_ffn_g[0], peer_w_q[0].astype(BF16), peer_sub_keys[0],
                        ROUTE_TOKEN_TILE)
    w = _peer_u(tile, x1, norm_ffn_g[0], gate, _peer_table(peer_u[0]))
    return _peer_v(tile, w, x1, final_norm_g, _peer_table(peer_v[0])).reshape(B, S, D)
```

```python
import jax
import jax.numpy as jnp
from jax import lax
from jax.experimental import pallas as pl
from jax.experimental.pallas import tpu as pltpu

F32 = jnp.float32
BF16 = jnp.bfloat16

EPS = 1e-6
LANES = 128
SUBLANES = 8
HG_HEADS = 4
HG_CHUNK = 64
HG_CHUNKS_PER_ITER = 16
HG_STATE_UNROLL = 32
MX_HEADS = 4
BRANCH_WIDTH = 512
N_MEM = 256
PEER_HEADS = 8
PEER_NKEYS = 128
PEER_TOPK = 16
PEER_SLOTS = PEER_HEADS * PEER_TOPK
EXPERT_WORD_ROWS = 4
PACK_ROWS = 512
PEER_TOKEN_TILE = 128
TOKENS_PER_ITER = 128
VMEM_LIMIT = 48 * 1024 * 1024
PROJ_ROW_TILE, PROJ_COL_TILE = 2048, 768
KV_ROW_TILE, KV_COL_TILE = 1024, 512
MIX_TOKEN_TILE = 512
ROUTE_TOKEN_TILE = 256

NT_DIMS = (((1,), (1,)), ((), ()))
TN_DIMS = (((0,), (0,)), ((), ()))


def _rms(x, g):
    return x * lax.rsqrt(jnp.mean(x * x, axis=-1, keepdims=True) + EPS) * g


def _norm_matmul_kernel(x_ref, g_ref, w_ref, o_ref, h_ref):
    @pl.when(pl.program_id(1) == 0)
    def _():
        h_ref[...] = _rms(x_ref[...], g_ref[...]).astype(BF16)

    o_ref[...] = jnp.dot(h_ref[...], w_ref[...], preferred_element_type=F32).astype(o_ref.dtype)


def _norm_matmul(x, g, w_bf16, tm, tn, out_dtype=F32):
    T, D = x.shape
    N = w_bf16.shape[1]
    tm = min(tm, T)
    return pl.pallas_call(
        _norm_matmul_kernel,
        grid=(T // tm, N // tn),
        in_specs=[
            pl.BlockSpec((tm, D), lambda i, j: (i, 0)),
            pl.BlockSpec((1, D), lambda i, j: (0, 0)),
            pl.BlockSpec((D, tn), lambda i, j: (0, j)),
        ],
        out_specs=pl.BlockSpec((tm, tn), lambda i, j: (i, j)),
        out_shape=jax.ShapeDtypeStruct((T, N), out_dtype),
        scratch_shapes=[pltpu.VMEM((tm, D), BF16)],
        compiler_params=pltpu.CompilerParams(
            dimension_semantics=("parallel", "arbitrary"), vmem_limit_bytes=VMEM_LIMIT),
    )(x, g.reshape(1, D), w_bf16)


def _hgrn_kernel(q_ref, v_ref, zf_ref, zb_ref, g_ref, lb_ref, ng_ref, o_ref,
                 st_ref, out_ref, qs_ref, kk_ref, cum_ref, sc_ref, qe_ref, dst_ref, dec_ref):
    S = q_ref.shape[0]
    C = HG_CHUNK
    n = S // C
    row = lax.broadcasted_iota(jnp.int32, (C, C), 0)
    col = lax.broadcasted_iota(jnp.int32, (C, C), 1)
    keeps = (col <= row, col >= row)
    z_refs = (zf_ref, zb_ref)
    unroll = min(HG_CHUNKS_PER_ITER, n)
    state_unroll = min(HG_STATE_UNROLL, n)

    def chunks(fn):
        def body(i, carry):
            for j in range(unroll):
                sl = pl.ds(pl.multiple_of((i * unroll + j) * C, C), C)
                for d in range(2):
                    fn(i * unroll + j, sl, d)
            return carry
        lax.fori_loop(0, n // unroll, body, 0)

    qs_ref[...] = jax.nn.silu(q_ref[...])

    def decay_pass(c, sl, d):
        lb = lb_ref[d:d + 1, :]
        z = z_refs[d][sl, :]
        lf = jnp.log(lb + (1.0 - lb) * jax.nn.sigmoid(z))
        kk_ref[d, sl, :] = (1.0 - lb) * jax.nn.sigmoid(-z)
        tri = keeps[d].astype(BF16)
        hi = lf.astype(BF16)
        r1 = lf - hi.astype(F32)
        mid = r1.astype(BF16)
        lo = (r1 - mid.astype(F32)).astype(BF16)
        cum_ref[d, sl, :] = (jnp.dot(tri, hi, preferred_element_type=F32)
                             + jnp.dot(tri, mid, preferred_element_type=F32)
                             + jnp.dot(tri, lo, preferred_element_type=F32))

    def score_pass(c, sl, d):
        G = cum_ref[d, sl, :]
        kk = kk_ref[d, sl, :]
        q = qs_ref[sl, :]
        mid = C // 2 if d else C // 2 - 1
        last = 0 if d else C - 1
        g_mid = G[mid:mid + 1, :]
        g_last = G[last:last + 1, :]
        q_r = q * jnp.exp(G - g_mid)
        k_r = kk * jnp.exp(g_mid - G)
        scores = lax.dot_general(q_r, k_r, NT_DIMS, preferred_element_type=F32)
        sc_ref[d, sl, :] = jnp.where(keeps[d], scores, 0.0)
        qe_ref[d, sl, :] = q * jnp.exp(G)
        dst_ref[d, c] = lax.dot_general(v_ref[sl, :], kk * jnp.exp(g_last - G), TN_DIMS,
                                        preferred_element_type=F32)
        dec_ref[d, c] = jnp.broadcast_to(jnp.exp(g_last), (SUBLANES, LANES))

    def intra_pass(c, sl, d):
        out_ref[d, sl, :] = jnp.dot(sc_ref[d, sl, :], v_ref[sl, :], preferred_element_type=F32)

    chunks(decay_pass)
    chunks(score_pass)
    chunks(intra_pass)
    st_ref[...] = jnp.zeros_like(st_ref)

    def state_body(i, carry):
        for j in range(state_unroll):
            step = i * state_unroll + j
            for d, c in ((0, step), (1, n - 1 - step)):
                sl = pl.ds(pl.multiple_of(c * C, C), C)
                st = st_ref[d]
                out_ref[d, sl, :] += lax.dot_general(qe_ref[d, sl, :], st, NT_DIMS,
                                                     preferred_element_type=F32)
                st_ref[d] = st * dec_ref[d, c][0:1, :] + dst_ref[d, c]
        return carry

    lax.fori_loop(0, n // state_unroll, state_body, 0)
    o = out_ref[0] + out_ref[1]
    o = o * lax.rsqrt(jnp.mean(o * o, axis=-1, keepdims=True) + EPS)
    o_ref[...] = o * ng_ref[...] * jax.nn.silu(g_ref[...])


def _hgrn(proj, lb, ng, B, S):
    T = B * S
    H = HG_HEADS
    sec = lambda s: pl.BlockSpec((S, LANES), lambda b, h, s=s: (b, s * H + h))
    return pl.pallas_call(
        _hgrn_kernel,
        grid=(B, H),
        in_specs=[sec(0), sec(1), sec(2), sec(3), sec(4),
                  pl.BlockSpec((2, LANES), lambda b, h: (0, h)),
                  pl.BlockSpec((1, LANES), lambda b, h: (0, h))],
        out_specs=pl.BlockSpec((S, LANES), lambda b, h: (b, h)),
        out_shape=jax.ShapeDtypeStruct((T, H * LANES), F32),
        scratch_shapes=[pltpu.VMEM((2, LANES, LANES), F32),
                        pltpu.VMEM((2, S, LANES), F32),
                        pltpu.VMEM((S, LANES), F32),
                        pltpu.VMEM((2, S, LANES), F32),
                        pltpu.VMEM((2, S, LANES), F32),
                        pltpu.VMEM((2, S, HG_CHUNK), F32),
                        pltpu.VMEM((2, S, LANES), F32),
                        pltpu.VMEM((2, S // HG_CHUNK, LANES, LANES), F32),
                        pltpu.VMEM((2, S // HG_CHUNK, SUBLANES, LANES), F32)],
        compiler_params=pltpu.CompilerParams(
            dimension_semantics=("parallel", "parallel"), vmem_limit_bytes=VMEM_LIMIT),
    )(proj, proj, proj, proj, proj, lb, ng.reshape(1, -1))


def _mix_kernel(x_ref, sb_ref, sc_ref, sh_ref, scp_ref, shp_ref, scn_ref, shn_ref, mq_ref,
                gates_ref, yhg_ref, kv_ref, cw_ref, wb_ref, wo_ref, o_ref):
    i = pl.program_id(1)
    ni = pl.num_programs(1)
    ts = x_ref.shape[0]
    W = BRANCH_WIDTH
    u = sc_ref[...] * sh_ref[...]
    u_before = jnp.where(i > 0, scp_ref[SUBLANES - 1:, :] * shp_ref[SUBLANES - 1:, :], 0.0)
    u_after = jnp.where(i < ni - 1, scn_ref[0:1, :] * shn_ref[0:1, :], 0.0)
    rows = lax.broadcasted_iota(jnp.int32, (ts, W), 0)
    u_prev = jnp.where(rows == 0, u_before, pltpu.roll(u, 1, 0))
    u_next = jnp.where(rows == ts - 1, u_after, pltpu.roll(u, ts - 1, 0))
    cw = cw_ref[...]
    y_sc = sb_ref[...] * (cw[0:1, :] * u_prev + cw[1:2, :] * u + cw[2:3, :] * u_next)

    heads = []
    for h in range(MX_HEADS):
        qh = mq_ref[:, h * LANES:(h + 1) * LANES]
        kh = kv_ref[:, h * LANES:(h + 1) * LANES]
        vh = kv_ref[:, W + h * LANES:W + (h + 1) * LANES]
        logits = lax.dot_general(qh, kh, NT_DIMS, preferred_element_type=F32) * (LANES ** -0.5)
        e = jnp.exp(logits - jnp.max(logits, axis=-1, keepdims=True))
        p = e / jnp.sum(e, axis=-1, keepdims=True)
        heads.append(jnp.dot(p, vh, preferred_element_type=F32))
    y_mx = jnp.concatenate(heads, axis=1)

    D = x_ref.shape[1]
    merged = jnp.zeros((ts, D), F32)
    for n, y in enumerate((yhg_ref[...], y_sc, y_mx)):
        pr = jnp.dot(y.astype(BF16), wb_ref[n], preferred_element_type=F32)
        merged = merged + jax.nn.sigmoid(gates_ref[:, n * D:(n + 1) * D].astype(F32)) * pr
    out = jnp.dot(merged.astype(BF16), wo_ref[...], preferred_element_type=F32)
    o_ref[...] = x_ref[...] + out


def _mix(x, proj, gates, y_hg, kv, conv_w, wb_bf16, wo_bf16, B, S, ts):
    T, D = x.shape
    W = BRANCH_WIDTH
    ts = min(ts, S)
    nt = S // ts
    r8 = ts // SUBLANES
    tile = lambda b, i: b * nt + i
    sec = lambda s: pl.BlockSpec((ts, W), lambda b, i, s=s: (tile(b, i), s))
    before = lambda s: pl.BlockSpec(
        (SUBLANES, W), lambda b, i, s=s: (jnp.maximum(tile(b, i) * r8 - 1, 0), s))
    after = lambda s: pl.BlockSpec(
        (SUBLANES, W),
        lambda b, i, s=s: (jnp.minimum((tile(b, i) + 1) * r8, T // SUBLANES - 1), s))
    return pl.pallas_call(
        _mix_kernel,
        grid=(B, nt),
        in_specs=[
            pl.BlockSpec((ts, D), lambda b, i: (tile(b, i), 0)),
            sec(5), sec(6), sec(7), before(6), before(7), after(6), after(7), sec(8),
            pl.BlockSpec((ts, 3 * D), lambda b, i: (tile(b, i), 0)),
            pl.BlockSpec((ts, W), lambda b, i: (tile(b, i), 0)),
            pl.BlockSpec((N_MEM, 2 * W), lambda b, i: (b, 0)),
            pl.BlockSpec((3, W), lambda b, i: (0, 0)),
            pl.BlockSpec((3, W, D), lambda b, i: (0, 0, 0)),
            pl.BlockSpec((D, D), lambda b, i: (0, 0)),
        ],
        out_specs=pl.BlockSpec((ts, D), lambda b, i: (tile(b, i), 0)),
        out_shape=jax.ShapeDtypeStruct((T, D), F32),
        compiler_params=pltpu.CompilerParams(
            dimension_semantics=("parallel", "parallel"), vmem_limit_bytes=VMEM_LIMIT),
    )(x, proj, proj, proj, proj, proj, proj, proj, proj, gates, y_hg, kv, conv_w, wb_bf16, wo_bf16)


def _topk_rows(s, k, payload=None):
    n = s.shape[0]
    iota = lax.broadcasted_iota(jnp.int32, s.shape, 0).astype(F32)
    vals, ids = [], []
    for _ in range(k):
        m = jnp.max(s, axis=0, keepdims=True)
        am = jnp.min(jnp.where(s == m, iota, float(n)), axis=0, keepdims=True)
        hit = iota == am
        vals.append(m)
        if payload is None:
            ids.append(am)
        else:
            ids.append(jnp.sum(jnp.where(hit, payload, 0), axis=0, keepdims=True))
        s = jnp.where(hit, -jnp.inf, s)
    ids = jnp.concatenate(ids, axis=0)
    return jnp.concatenate(vals, axis=0), ids.astype(jnp.int32)


def _pair_candidates(top_s, top_i):
    K = PEER_TOPK
    assert K == 16
    (s0, s1), (i0, i1) = top_s, top_i
    row = lax.broadcasted_iota(jnp.int32, (8, s0.shape[1]), 0)

    def block(groups):
        a0 = groups[0][0]
        a_s, a_i = s0[a0:a0 + 1, :], i0[a0:a0 + 1, :]
        b_s, b_i, used = [], [], 0
        for a, n in groups:
            if used:
                a_s = jnp.where(row >= used, s0[a:a + 1, :], a_s)
                a_i = jnp.where(row >= used, i0[a:a + 1, :], a_i)
            b_s.append(s1[0:n, :])
            b_i.append(i1[0:n, :])
            used += n
        if used < 8:
            b_s.append(s1[0:8 - used, :])
            b_i.append(i1[0:8 - used, :])
        sums = a_s + jnp.concatenate(b_s, axis=0)
        ids = a_i * PEER_NKEYS + jnp.concatenate(b_i, axis=0)
        return (sums if used == 8 else jnp.where(row < used, sums, -jnp.inf)), ids

    blocks = [(s0[0:1, :] + s1, i0[0:1, :] * PEER_NKEYS + i1)]
    for groups in (((1, 8),), ((2, 5),), ((3, 4), (4, 3)), ((5, 2), (6, 2), (7, 2))):
        assert all(n == K // (a + 1) for a, n in groups)
        blocks.append(block(groups))
    blocks.append((s0[8:16, :] + s1[0:1, :], i0[8:16, :] * PEER_NKEYS + i1[0:1, :]))
    return (jnp.concatenate([b[0] for b in blocks], axis=0),
            jnp.concatenate([b[1] for b in blocks], axis=0))


def _route_kernel(x_ref, g_ref, wq_ref, keys_ref, tile_ref, gate_ref):
    K = PEER_TOPK
    h = _rms(x_ref[...], g_ref[...])
    q = jnp.dot(h.astype(BF16), wq_ref[...], preferred_element_type=F32)
    idx_rows, gate_rows = [], []
    for hd in range(PEER_HEADS):
        top_s, top_i = [], []
        for p in range(2):
            c0 = (hd * 2 + p) * LANES
            s_t = lax.dot_general(keys_ref[hd, p], q[:, c0:c0 + LANES], NT_DIMS,
                                  preferred_element_type=F32)
            vs, is_ = _topk_rows(s_t, K)
            top_s.append(vs)
            top_i.append(is_)
        cand_s, cand_i = _pair_candidates(top_s, top_i)
        best_s, best_i = _topk_rows(cand_s, K, payload=cand_i)
        e = jnp.exp(best_s - best_s[0:1, :])
        gate_rows.append(e / jnp.sum(e, axis=0, keepdims=True))
        idx_rows.append(best_i)
    tile_t = jnp.concatenate(idx_rows, axis=0) * EXPERT_WORD_ROWS
    for j in range(tile_ref.shape[0]):
        tile_ref[j] = tile_t[:, j * PEER_TOKEN_TILE:(j + 1) * PEER_TOKEN_TILE]
    gate_ref[...] = jnp.concatenate(gate_rows, axis=0).T


def _route(x1, g, wq_bf16, keys, tt):
    T, D = x1.shape
    tt = min(tt, T)
    TU = PEER_TOKEN_TILE
    NQ = wq_bf16.shape[1]
    tok = lambda w: pl.BlockSpec((tt, w), lambda i: (i, 0))
    return pl.pallas_call(
        _route_kernel,
        grid=(T // tt,),
        in_specs=[tok(D), pl.BlockSpec((1, D), lambda i: (0, 0)),
                  pl.BlockSpec((D, NQ), lambda i: (0, 0)),
                  pl.BlockSpec(keys.shape, lambda i: (0, 0, 0, 0))],
        out_specs=[pl.BlockSpec((tt // TU, PEER_SLOTS, TU), lambda i: (i, 0, 0)),
                   tok(PEER_SLOTS)],
        out_shape=[jax.ShapeDtypeStruct((T // TU, PEER_SLOTS, TU), jnp.int32),
                   jax.ShapeDtypeStruct((T, PEER_SLOTS), F32)],
        compiler_params=pltpu.CompilerParams(
            dimension_semantics=("parallel",), vmem_limit_bytes=VMEM_LIMIT),
    )(x1, g.reshape(1, D), wq_bf16, keys)


def _gather_tiles(tile_s, t, tab_ref):
    def words(k):
        r = pl.multiple_of(tile_s.at[k][t], EXPERT_WORD_ROWS)
        return tab_ref[pl.ds(r, EXPERT_WORD_ROWS), :]

    return jnp.concatenate(
        [pltpu.bitcast(jnp.concatenate([words(k), words(k + 1)], axis=0), BF16)
         for k in range(0, PEER_SLOTS, 2)], axis=0)


def _tile_fetch(tile_hbm, step, bufs, sems, slot):
    return pltpu.make_async_copy(tile_hbm.at[step], bufs[slot], sems.at[slot])


def _with_tile_block(tile_hbm, bufs, sems, run):
    i = pl.program_id(0)
    for slot in range(2):
        @pl.when(i % 2 == slot)
        def _(slot=slot):
            @pl.when(i + 1 < pl.num_programs(0))
            def _():
                _tile_fetch(tile_hbm, i + 1, bufs, sems, 1 - slot).start()

            _tile_fetch(tile_hbm, i, bufs, sems, slot).wait()
            run(bufs[slot])


def _hi_lo(a):
    hi = a.astype(BF16)
    return hi, (a - hi.astype(F32)).astype(BF16)


def _row_to_chunks(row):
    return jnp.concatenate([row[:, c * LANES:(c + 1) * LANES]
                            for c in range(row.shape[1] // LANES)], axis=0)


def _chunks_to_row(x8):
    return jnp.concatenate([x8[c:c + 1, :] for c in range(x8.shape[0])], axis=1)


def _token_loop(tu, token):
    def body(i, carry):
        for j in range(TOKENS_PER_ITER):
            token(i * TOKENS_PER_ITER + j)
        return carry

    lax.fori_loop(0, tu // TOKENS_PER_ITER, body, 0)


def _peer_u_kernel(tile_hbm, x_ref, ng_ref, gate_ref, tab_ref, g_ref, cm_ref, w_ref,
                   h_ref, r_ref, tile_a, tile_b, sems):
    tu = gate_ref.shape[0]
    bufs = (tile_a, tile_b)

    @pl.when(pl.program_id(0) == 0)
    def _():
        _tile_fetch(tile_hbm, 0, bufs, sems, 0).start()

    h_ref[...] = _rms(x_ref[...], ng_ref[...])

    def run(tile_s):
        def token(t):
            m = _gather_tiles(tile_s, t, tab_ref)
            hi, lo = _hi_lo(_row_to_chunks(h_ref[pl.ds(t, 1), :]))
            r = lax.dot_general(jnp.concatenate([hi, lo], axis=0), m, NT_DIMS,
                                preferred_element_type=F32)
            r8 = (r[0:8, :] + r[8:16, :]) * cm_ref[...]
            r_ref[pl.ds(t, 1), :] = jnp.sum(r8, axis=0, keepdims=True)

        _token_loop(tu, token)

    _with_tile_block(tile_hbm, bufs, sems, run)
    hi, lo = _hi_lo(r_ref[...])
    a = (jnp.dot(hi, g_ref[...], preferred_element_type=F32)
         + jnp.dot(lo, g_ref[...], preferred_element_type=F32))
    w_ref[...] = gate_ref[...] * (0.5 * a * (1.0 + lax.erf(a * (2.0 ** -0.5))))


def _peer_v_kernel(tile_hbm, w_ref, x_ref, fg_ref, tab_ref, e_ref, cm_ref, o_ref, wexp_ref,
                   tile_a, tile_b, sems):
    tu = w_ref.shape[0]
    bufs = (tile_a, tile_b)

    @pl.when(pl.program_id(0) == 0)
    def _():
        _tile_fetch(tile_hbm, 0, bufs, sems, 0).start()

    hi, lo = _hi_lo(w_ref[...])
    wexp_ref[...] = (jnp.dot(hi, e_ref[...], preferred_element_type=F32)
                     + jnp.dot(lo, e_ref[...], preferred_element_type=F32))

    def run(tile_s):
        def token(t):
            m = _gather_tiles(tile_s, t, tab_ref)
            hi8, lo8 = _hi_lo(wexp_ref[pl.ds(t, 1), :] * cm_ref[...])
            o = jnp.dot(jnp.concatenate([hi8, lo8], axis=0), m, preferred_element_type=F32)
            x2 = _row_to_chunks(x_ref[pl.ds(t, 1), :]) + o[0:8, :] + o[8:16, :]
            ss = jnp.sum(jnp.sum(x2 * x2, axis=1, keepdims=True), axis=0, keepdims=True)
            y8 = x2 * lax.rsqrt(ss * (1.0 / x_ref.shape[1]) + EPS) * fg_ref[...]
            o_ref[pl.ds(t, 1), :] = _chunks_to_row(y8)

        _token_loop(tu, token)

    _with_tile_block(tile_hbm, bufs, sems, run)


def _peer_consts():
    row = jnp.arange(PEER_SLOTS * 8)
    slot = row // 8
    chunk = row % 8 // 2 + 4 * (row % 2)
    expand = (jnp.arange(PEER_SLOTS)[:, None] == slot[None, :]).astype(BF16)
    chunk8 = (jnp.arange(8)[:, None] == chunk[None, :]).astype(F32)
    return expand, expand.T, chunk8


def _pack_table_kernel(w_ref, o_ref):
    tm, d = w_ref.shape
    w = w_ref[...]
    lo = pltpu.bitcast(w[:, :d // 2].astype(BF16).astype(F32), jnp.uint32) >> 16
    hi = pltpu.bitcast(w[:, d // 2:].astype(BF16).astype(F32), jnp.uint32) & jnp.uint32(0xFFFF0000)
    words = pltpu.bitcast(hi | lo, jnp.int32)
    for j in range(EXPERT_WORD_ROWS):
        o_ref[pl.ds(j, tm, stride=EXPERT_WORD_ROWS), :] = words[:, j * LANES:(j + 1) * LANES]


def _peer_table(w):
    n, d = w.shape
    tm = min(PACK_ROWS, n)
    return pl.pallas_call(
        _pack_table_kernel,
        grid=(n // tm,),
        in_specs=[pl.BlockSpec((tm, d), lambda i: (i, 0))],
        out_specs=pl.BlockSpec((tm * EXPERT_WORD_ROWS, LANES), lambda i: (i, 0)),
        out_shape=jax.ShapeDtypeStruct((n * EXPERT_WORD_ROWS, LANES), jnp.int32),
        compiler_params=pltpu.CompilerParams(dimension_semantics=("parallel",)),
    )(w)


def _peer_specs(tu):
    smem = pl.BlockSpec(memory_space=pl.ANY)
    tok = pl.BlockSpec((tu, PEER_SLOTS), lambda i: (i, 0))
    full = lambda a: pl.BlockSpec(a.shape, lambda i: (0,) * a.ndim)
    resident = lambda a: pl.BlockSpec(a.shape, lambda i: (0,) * a.ndim,
                                      pipeline_mode=pl.Buffered(1))
    return smem, tok, full, resident


def _peer_u(tile, x1, ng, gate, tab):
    T, D = x1.shape
    tu = PEER_TOKEN_TILE
    _, group, chunk8 = _peer_consts()
    smem, tok, full, resident = _peer_specs(tu)
    return pl.pallas_call(
        _peer_u_kernel,
        grid=(T // tu,),
        in_specs=[smem, pl.BlockSpec((tu, D), lambda i: (i, 0)),
                  pl.BlockSpec((1, D), lambda i: (0, 0)), tok,
                  resident(tab), full(group), full(chunk8)],
        out_specs=tok,
        out_shape=jax.ShapeDtypeStruct((T, PEER_SLOTS), F32),
        scratch_shapes=[pltpu.VMEM((tu, D), F32), pltpu.VMEM((tu, PEER_SLOTS * 8), F32),
                        pltpu.SMEM((PEER_SLOTS, tu), jnp.int32), pltpu.SMEM((PEER_SLOTS, tu), jnp.int32),
                        pltpu.SemaphoreType.DMA((2,))],
        compiler_params=pltpu.CompilerParams(
            dimension_semantics=("arbitrary",), vmem_limit_bytes=VMEM_LIMIT),
    )(tile, x1, ng.reshape(1, D), gate, tab, group, chunk8)


def _peer_v(tile, w, x1, fg, tab):
    T, D = x1.shape
    tu = PEER_TOKEN_TILE
    expand, _, chunk8 = _peer_consts()
    smem, tok, full, resident = _peer_specs(tu)
    return pl.pallas_call(
        _peer_v_kernel,
        grid=(T // tu,),
        in_specs=[smem, tok, pl.BlockSpec((tu, D), lambda i: (i, 0)),
                  pl.BlockSpec((D // LANES, LANES), lambda i: (0, 0)),
                  resident(tab), full(expand), full(chunk8)],
        out_specs=pl.BlockSpec((tu, D), lambda i: (i, 0)),
        out_shape=jax.ShapeDtypeStruct((T, D), F32),
        scratch_shapes=[pltpu.VMEM((tu, PEER_SLOTS * 8), F32),
                        pltpu.SMEM((PEER_SLOTS, tu), jnp.int32), pltpu.SMEM((PEER_SLOTS, tu), jnp.int32),
                        pltpu.SemaphoreType.DMA((2,))],
        compiler_params=pltpu.CompilerParams(
            dimension_semantics=("arbitrary",), vmem_limit_bytes=VMEM_LIMIT),
    )(tile, w, x1, fg.reshape(D // LANES, LANES), tab, expand, chunk8)


def kernel(x, mem, norm_mix_g, w_in, hg_lb, hg_norm_g, sc_conv_w, mem_norm_g, w_mem_kv,
           w_branch, w_out, norm_ffn_g, peer_w_q, peer_sub_keys, peer_u, peer_v, final_norm_g):
    B, S, D = x.shape
    T = B * S
    assert w_in.shape[0] == 1, "single-layer block"
    n_main = 9 * BRANCH_WIDTH
    lb_table = jnp.cumsum(jax.nn.softmax(hg_lb.astype(F32), axis=0), axis=0)
    xf = x.reshape(T, D)
    memf = mem.reshape(B * N_MEM, D)
    w_in_b = w_in[0].astype(BF16)
    proj = _norm_matmul(xf, norm_mix_g[0], w_in_b[:, :n_main], PROJ_ROW_TILE, PROJ_COL_TILE)
    gates = _norm_matmul(xf, norm_mix_g[0], w_in_b[:, n_main:], PROJ_ROW_TILE, PROJ_COL_TILE, BF16)
    kv = _norm_matmul(memf, mem_norm_g[0], w_mem_kv[0].astype(BF16), KV_ROW_TILE, KV_COL_TILE)
    y_hg = _hgrn(proj, lb_table[0], hg_norm_g[0], B, S)
    x1 = _mix(xf, proj, gates, y_hg, kv, sc_conv_w[0], w_branch[0].astype(BF16),
              w_out[0].astype(BF16), B, S, MIX_TOKEN_TILE)
    tile, gate = _route(x1, norm_ffn_g[0], peer_w_q[0].astype(BF16), peer_sub_keys[0],
                        ROUTE_TOKEN_TILE)
    w = _peer_u(tile, x1, norm_ffn_g[0], gate, _peer_table(peer_u[0]))
    return _peer_v(tile, w, x1, final_norm_g, _peer_table(peer_v[0])).reshape(B, S, D)
```
